```python
import math
import jax
import jax.numpy as jnp
from jax import lax
import numpy as np

D_MODEL = 2048
BATCH = 2
SEQ = 8192
DEPTH = 2

CTX_LEN = 256
GRID_W = 64
N_MOD = 6
RMS_EPS = 1e-6
CONV_W = 4
CONV_PAD_LEFT = CONV_W // 2

S5_WIDTH = D_MODEL // 2
S5_GROUP = 16
S5_GROUPS = S5_WIDTH // S5_GROUP
S5_STATE = 64
LRU_WIDTH = D_MODEL // 2
LRU_HEADS = 16
LRU_HEAD_DIM = LRU_WIDTH // LRU_HEADS
LRU_C = 8.0
HY_IN = S5_WIDTH + 2 * LRU_WIDTH
HY_OUT = S5_WIDTH + LRU_WIDTH

SSD_INNER = 2 * D_MODEL
SSD_HEAD_DIM = 64
SSD_HEADS = SSD_INNER // SSD_HEAD_DIM
SSD_GROUPS = 8
SSD_STATE = 128
SSD_CHUNK = 128
SSD_BC = SSD_GROUPS * SSD_STATE
SSD_CONV_CH = SSD_INNER + 2 * SSD_BC
SSD_IN = SSD_INNER + SSD_CONV_CH + 2 * SSD_HEADS

MOE_GROUPS = 4
MOE_PER_GROUP = 8
MOE_EXPERTS = MOE_GROUPS * MOE_PER_GROUP
MOE_TOPK = 2
MOE_FF = D_MODEL // 4
MOE_BLOCK = 128

N_EVEN = (DEPTH + 1) // 2
N_ODD = DEPTH // 2

kernel_name = 'hybrid_s5_rglru_ssd_hmoe_prefix_dit'


def rms_norm(x, g):
    xf = x.astype(jnp.float32)
    y = xf * lax.rsqrt(jnp.mean(xf * xf, axis=-1, keepdims=True) + RMS_EPS)
    return (y * g.astype(jnp.float32)).astype(x.dtype)


def modulate(h, shift, scale):
    return h * (1 + scale) + shift


def dwconv(t, w, b):
    ch = t.shape[-1]
    y = lax.conv_general_dilated(
        t, w[:, None, :].astype(t.dtype), window_strides=(1,),
        padding=[(CONV_PAD_LEFT, CONV_W - 1 - CONV_PAD_LEFT)],
        dimension_numbers=('NWC', 'WIO', 'NWC'), feature_group_count=ch)
    return y + b


def to_column_major(t, rows):
    bsz, n, d = t.shape
    return t.reshape(bsz, rows, GRID_W, d).transpose(0, 2, 1, 3).reshape(bsz, n, d)


def to_row_major(t, rows):
    bsz, n, d = t.shape
    return t.reshape(bsz, GRID_W, rows, d).transpose(0, 2, 1, 3).reshape(bsz, n, d)


def _flip(t, rev):
    return jnp.flip(t, axis=1) if rev else t


def _affine_combine(left, right):
    a1, b1 = left
    a2, b2 = right
    return a2 * a1, a2 * b1 + b2


def _complex_affine_combine(left, right):
    ar1, ai1, br1, bi1 = left
    ar2, ai2, br2, bi2 = right
    return (ar2 * ar1 - ai2 * ai1, ar2 * ai1 + ai2 * ar1,
            ar2 * br1 - ai2 * bi1 + br2, ar2 * bi1 + ai2 * br1 + bi2)


def s5_discretise(lam_re, lam_im, log_dt, b_re, b_im, dtype):
    f32 = jnp.float32
    lr, li = lam_re.astype(f32), lam_im.astype(f32)
    dt = jnp.exp(log_dt.astype(f32))[:, None]
    mag = jnp.exp(lr * dt)
    a_re, a_im = mag * jnp.cos(li * dt), mag * jnp.sin(li * dt)
    den = lr * lr + li * li
    k_re = ((a_re - 1) * lr + a_im * li) / den
    k_im = (a_im * lr - (a_re - 1) * li) / den
    br, bi = b_re.astype(f32), b_im.astype(f32)
    bb_re = k_re[..., None] * br - k_im[..., None] * bi
    bb_im = k_re[..., None] * bi + k_im[..., None] * br
    return (a_re.astype(dtype), a_im.astype(dtype), bb_re.astype(dtype), bb_im.astype(dtype))


def s5_states(u, a_re, a_im, bb_re, bb_im, s0, reverse):
    bsz, n, _ = u.shape
    ug = u.reshape(bsz, n, S5_GROUPS, S5_GROUP)
    bu_re = jnp.einsum('blgh,gph->blgp', ug, bb_re)
    bu_im = jnp.einsum('blgh,gph->blgp', ug, bb_im)
    shape = (1, n, S5_GROUPS, S5_STATE)
    cum_re, cum_im, h_re, h_im = lax.associative_scan(
        _complex_affine_combine,
        (jnp.broadcast_to(a_re, shape), jnp.broadcast_to(a_im, shape), bu_re, bu_im),
        reverse=reverse, axis=1)
    if s0 is not None:
        s_re, s_im = s0[0][:, None], s0[1][:, None]
        h_re = h_re + cum_re * s_re - cum_im * s_im
        h_im = h_im + cum_re * s_im + cum_im * s_re
    return h_re, h_im


def s5_readout(h_re, h_im, c_re, c_im):
    bsz, n = h_re.shape[:2]
    y = jnp.einsum('blgp,ghp->blgh', h_re, c_re) - jnp.einsum('blgp,ghp->blgh', h_im, c_im)
    return y.reshape(bsz, n, S5_WIDTH)


def s5_glu(y, w, b):
    z = jax.nn.gelu(y)
    return z * jax.nn.sigmoid(z @ w + b)


def s5_mixer(u_x, u_c, lam_re, lam_im, log_dt, b_re, b_im, c_re, c_im, d_skip, glu_w, glu_b,
             need_ctx_out):
    y_x = d_skip * u_x
    y_c = d_skip * u_c if need_ctx_out else None
    for dirn, rev in enumerate((False, True)):
        disc = s5_discretise(lam_re[dirn], lam_im[dirn], log_dt[dirn], b_re[dirn], b_im[dirn],
                             u_x.dtype)
        hc_re, hc_im = s5_states(u_c, *disc, None, rev)
        edge = 0 if rev else -1
        hx_re, hx_im = s5_states(u_x, *disc, (hc_re[:, edge], hc_im[:, edge]), rev)
        y_x = y_x + s5_readout(hx_re, hx_im, c_re[dirn], c_im[dirn])
        if need_ctx_out:
            y_c = y_c + s5_readout(hc_re, hc_im, c_re[dirn], c_im[dirn])
    out_c = s5_glu(y_c, glu_w, glu_b) if need_ctx_out else None
    return s5_glu(y_x, glu_w, glu_b), out_c


def rglru_states(v, wa, ba, wx, bx, lam, s0, reverse):
    bsz, n, w = v.shape
    vh = v.reshape(bsz, n, LRU_HEADS, LRU_HEAD_DIM)
    r = jax.nn.sigmoid(jnp.einsum('blhi,hij->blhj', vh, wa).reshape(bsz, n, w) + ba)
    i = jax.nn.sigmoid(jnp.einsum('blhi,hij->blhj', vh, wx).reshape(bsz, n, w) + bx)
    log_a = -LRU_C * r.astype(jnp.float32) * jax.nn.softplus(-lam.astype(jnp.float32))
    a = jnp.exp(log_a).astype(v.dtype)
    b = (jnp.sqrt(-jnp.expm1(2.0 * log_a)) * (i * v)).astype(v.dtype)
    cum, h = lax.associative_scan(_affine_combine, (a, b), reverse=reverse, axis=1)
    if s0 is not None:
        h = h + cum * s0[:, None]
    return h


def rglru_mixer(g_x, v_x, g_c, v_c, conv_w, conv_b, wa, ba, wx, bx, lam, need_ctx_out):
    v_x = dwconv(v_x, conv_w, conv_b)
    v_c = dwconv(v_c, conv_w, conv_b)
    hs_x, hs_c = [], []
    for dirn, rev in enumerate((False, True)):
        prm = (wa[dirn], ba[dirn], wx[dirn], bx[dirn], lam[dirn])
        h_c = rglru_states(v_c, *prm, None, rev)
        hs_x.append(rglru_states(v_x, *prm, h_c[:, 0 if rev else -1], rev))
        hs_c.append(h_c)
    out_x = jax.nn.gelu(g_x) * (hs_x[0] + hs_x[1])
    out_c = jax.nn.gelu(g_c) * (hs_c[0] + hs_c[1]) if need_ctx_out else None
    return out_x, out_c


def s5_lru_layer(hx, hc, w_in, w_out, lam_re, lam_im, log_dt, b_re, b_im, c_re, c_im, d_skip,
                 glu_w, glu_b, conv_w, conv_b, wa, ba, wx, bx, lam, need_ctx_out):
    cuts = [S5_WIDTH, S5_WIDTH + LRU_WIDTH]
    u_x, g_x, v_x = jnp.split(hx @ w_in, cuts, axis=-1)
    u_c, g_c, v_c = jnp.split(hc @ w_in, cuts, axis=-1)
    a_x, a_c = s5_mixer(u_x, u_c, lam_re, lam_im, log_dt, b_re, b_im, c_re, c_im, d_skip,
                        glu_w, glu_b, need_ctx_out)
    r_x, r_c = rglru_mixer(g_x, v_x, g_c, v_c, conv_w, conv_b, wa, ba, wx, bx, lam, need_ctx_out)
    out_x = jnp.concatenate([a_x, r_x], axis=-1) @ w_out
    out_c = jnp.concatenate([a_c, r_c], axis=-1) @ w_out if need_ctx_out else None
    return out_x, out_c


def segsum(a):
    t = a.shape[-1]
    rep = jnp.broadcast_to(a[..., None], a.shape + (t,))
    rep = jnp.where(jnp.tril(jnp.ones((t, t), bool), -1), rep, 0)
    s = jnp.cumsum(rep, axis=-2)
    return jnp.where(jnp.tril(jnp.ones((t, t), bool)), s, -jnp.inf)


def ssd_scan(xs, dt, b, c, a, init, compute_y):
    bsz, n, nh, hd = xs.shape
    ng, ns = b.shape[2], b.shape[3]
    hpg, q = nh // ng, SSD_CHUNK
    nc = n // q
    dtype = xs.dtype
    xq = (xs * dt[..., None].astype(dtype)).reshape(bsz, nc, q, ng, hpg, hd)
    adt = (dt * a).reshape(bsz, nc, q, ng, hpg).transpose(0, 3, 4, 1, 2)
    acs = jnp.cumsum(adt, axis=-1)
    bq = b.reshape(bsz, nc, q, ng, ns)
    cq = c.reshape(bsz, nc, q, ng, ns)
    decay_to_end = jnp.exp(acs[..., -1:] - acs).astype(dtype)
    states = jnp.einsum('bcqgn,bghcq,bcqghp->bcghpn', bq, decay_to_end, xq)
    if init is None:
        init = jnp.zeros((bsz, nh, hd, ns), dtype)
    states = jnp.concatenate([init.reshape(bsz, 1, ng, hpg, hd, ns).astype(dtype), states], axis=1)
    chunk_decay = jnp.exp(segsum(jnp.pad(acs[..., -1], [(0, 0)] * 3 + [(1, 0)]))).astype(dtype)
    states = jnp.einsum('bghzc,bcghpn->bzghpn', chunk_decay, states)
    final = states[:, -1].reshape(bsz, nh, hd, ns)
    if not compute_y:
        return None, final
    seg = jnp.exp(segsum(adt)).astype(dtype)
    cb = jnp.einsum('bcqgn,bcsgn->bgcqs', cq, bq)
    y = jnp.einsum('bgcqs,bghcqs,bcsghp->bcqghp', cb, seg, xq)
    y = y + jnp.einsum('bcqgn,bcghpn,bghcq->bcqghp', cq, states[:, :-1], jnp.exp(acs).astype(dtype))
    return y.reshape(bsz, n, nh, hd), final


def ssd_split(t):
    bsz, n, _ = t.shape
    xs, b, c = jnp.split(t, [SSD_INNER, SSD_INNER + SSD_BC], axis=-1)
    return (xs.reshape(bsz, n, SSD_HEADS, SSD_HEAD_DIM),
            b.reshape(bsz, n, SSD_GROUPS, SSD_STATE),
            c.reshape(bsz, n, SSD_GROUPS, SSD_STATE))


def ssd_layer(hx, hc, w_in, conv_w, conv_b, dt_bias, a_log, d_skip, norm_g, w_out, need_ctx_out):
    cuts = [SSD_INNER, SSD_INNER + SSD_CONV_CH]
    z_x, xbc_x, dt_x = jnp.split(hx @ w_in, cuts, axis=-1)
    if need_ctx_out:
        z_c, xbc_c, dt_c = jnp.split(hc @ w_in, cuts, axis=-1)
    else:
        xbc_c, dt_c = jnp.split(hc @ w_in[:, SSD_INNER:], [SSD_CONV_CH], axis=-1)
    lat = ssd_split(jax.nn.silu(dwconv(xbc_x, conv_w, conv_b)))
    cx = ssd_split(jax.nn.silu(dwconv(xbc_c, conv_w, conv_b)))
    y_x = d_skip[:, None] * lat[0]
    y_c = d_skip[:, None] * cx[0] if need_ctx_out else None
    for dirn, rev in enumerate((False, True)):
        a = -jnp.exp(a_log[dirn].astype(jnp.float32))
        heads = slice(dirn * SSD_HEADS, (dirn + 1) * SSD_HEADS)
        dtc = jax.nn.softplus((dt_c[..., heads] + dt_bias[dirn]).astype(jnp.float32))
        dtx = jax.nn.softplus((dt_x[..., heads] + dt_bias[dirn]).astype(jnp.float32))
        yc_d, s_c = ssd_scan(_flip(cx[0], rev), _flip(dtc, rev), _flip(cx[1], rev),
                             _flip(cx[2], rev), a, None, need_ctx_out)
        yx_d, _ = ssd_scan(_flip(lat[0], rev), _flip(dtx, rev), _flip(lat[1], rev),
                           _flip(lat[2], rev), a, s_c, True)
        y_x = y_x + _flip(yx_d, rev)
        if need_ctx_out:
            y_c = y_c + _flip(yc_d, rev)
    bsz, n = hx.shape[:2]
    out_x = rms_norm(y_x.reshape(bsz, n, SSD_INNER) * jax.nn.silu(z_x), norm_g) @ w_out
    out_c = None
    if need_ctx_out:
        y_c = y_c.reshape(bsz, hc.shape[1], SSD_INNER)
        out_c = rms_norm(y_c * jax.nn.silu(z_c), norm_g) @ w_out
    return out_x, out_c


def expert_dispatch(h, experts, gates, w1, w3, w2):
    n_tok, d = h.shape
    n_pairs = n_tok * MOE_TOPK
    flat_e = experts.reshape(-1)
    order = jnp.argsort(flat_e)
    sorted_e = flat_e[order]
    counts = jnp.bincount(flat_e, length=MOE_EXPERTS)
    padded = (counts + MOE_BLOCK - 1) // MOE_BLOCK * MOE_BLOCK
    pad_end = jnp.cumsum(padded)
    start = jnp.cumsum(counts) - counts
    dest = (pad_end - padded)[sorted_e] + jnp.arange(n_pairs) - start[sorted_e]
    n_blocks = -(-n_pairs // MOE_BLOCK) + MOE_EXPERTS
    n_slots = n_blocks * MOE_BLOCK
    slot_tok = jnp.full((n_slots,), n_tok, jnp.int32).at[dest].set((order // MOE_TOPK).astype(jnp.int32))
    slot_gate = jnp.zeros((n_slots,), h.dtype).at[dest].set(gates.reshape(-1)[order])
    block_e = jnp.minimum(jnp.searchsorted(pad_end, jnp.arange(n_blocks) * MOE_BLOCK, side='right'),
                          MOE_EXPERTS - 1)
    h_pad = jnp.concatenate([h, jnp.zeros((1, d), h.dtype)], axis=0)
    xb = h_pad[slot_tok].reshape(n_blocks, MOE_BLOCK, d)

    def expert_block(args):
        xblk, e = args
        return (jax.nn.silu(xblk @ w1[e]) * (xblk @ w3[e])) @ w2[e]

    yb = lax.map(expert_block, (xb, block_e)).reshape(n_slots, d)
    out = jnp.zeros((n_tok + 1, d), h.dtype).at[slot_tok].add(yb * slot_gate[:, None])
    return out[:n_tok]


def hier_moe(h, wg, bg, we, be, w1, w3, w2):
    n_tok = h.shape[0]
    tok = jnp.arange(n_tok)
    logit_g = (h @ wg + bg).astype(jnp.float32)
    p_g = jax.nn.softmax(logit_g, axis=-1)
    _, g_idx = lax.top_k(logit_g, 1)
    p_top = p_g[tok, g_idx[:, 0]][:, None]
    logit_e = (h @ we + be).astype(jnp.float32).reshape(n_tok, MOE_GROUPS, MOE_PER_GROUP)
    logit_in = logit_e[tok, g_idx[:, 0]]
    top_v, top_i = lax.top_k(logit_in, MOE_TOPK)
    gates = (p_top * jax.nn.softmax(top_v, axis=-1)).astype(h.dtype)
    experts = g_idx * MOE_PER_GROUP + top_i
    return expert_dispatch(h, experts, gates, w1, w3, w2)


def setup_inputs(seed: int = 0) -> dict:
    key = jax.random.key(seed)
    keys = iter(jax.random.split(key, 64))
    f32 = jnp.float32
    d = D_MODEL

    def normal(shape, std):
        return std * jax.random.normal(next(keys), shape, f32)

    def uniform(shape, lo, hi):
        return jax.random.uniform(next(keys), shape, f32, lo, hi)

    def gain(shape):
        return 1.0 + normal(shape, 0.02)

    lru_a = uniform((N_EVEN, 2, LRU_WIDTH), 0.9, 0.999) ** (1.0 / LRU_C)
    dt0 = jnp.exp(uniform((N_ODD, 2, SSD_HEADS), math.log(1e-3), math.log(1e-1)))
    s5_shape = (N_EVEN, 2, S5_GROUPS, S5_STATE)
    lru_gate_shape = (N_EVEN, 2, LRU_HEADS, LRU_HEAD_DIM, LRU_HEAD_DIM)
    return {
        'x': normal((BATCH, SEQ, d), 1.0),
        'c': normal((BATCH, d), 1.0),
        'ctx': normal((BATCH, CTX_LEN, d), 1.0),
        'c_ctx': normal((d,), 1.0),
        'norm_mix_g': gain((DEPTH, d)),
        'norm_ffn_g': gain((DEPTH, d)),
        'mod_w': normal((DEPTH, d, N_MOD * d), 0.5 * d ** -0.5),
        'mod_b': normal((DEPTH, N_MOD * d), 0.02),
        'hy_w_in': normal((N_EVEN, d, HY_IN), d ** -0.5),
        'hy_w_out': normal((N_EVEN, HY_OUT, d), HY_OUT ** -0.5),
        's5_lam_re': -0.5 + normal(s5_shape, 0.01),
        's5_lam_im': math.pi * jnp.arange(S5_STATE, dtype=f32) + normal(s5_shape, 0.01),
        's5_log_dt': uniform((N_EVEN, 2, S5_GROUPS), math.log(1e-3), math.log(1e-1)),
        's5_b_re': normal((N_EVEN, 2, S5_GROUPS, S5_STATE, S5_GROUP), (2 * S5_GROUP) ** -0.5),
        's5_b_im': normal((N_EVEN, 2, S5_GROUPS, S5_STATE, S5_GROUP), (2 * S5_GROUP) ** -0.5),
        's5_c_re': normal((N_EVEN, 2, S5_GROUPS, S5_GROUP, S5_STATE), S5_STATE ** -0.5),
        's5_c_im': normal((N_EVEN, 2, S5_GROUPS, S5_GROUP, S5_STATE), S5_STATE ** -0.5),
        's5_d': normal((N_EVEN, S5_WIDTH), 1.0),
        's5_glu_w': normal((N_EVEN, S5_WIDTH, S5_WIDTH), S5_WIDTH ** -0.5),
        's5_glu_b': normal((N_EVEN, S5_WIDTH), 0.02),
        'lru_conv_w': normal((N_EVEN, CONV_W, LRU_WIDTH), CONV_W ** -0.5),
        'lru_conv_b': normal((N_EVEN, LRU_WIDTH), 0.02),
        'lru_wa': normal(lru_gate_shape, LRU_HEAD_DIM ** -0.5),
        'lru_ba': normal((N_EVEN, 2, LRU_WIDTH), 0.02),
        'lru_wx': normal(lru_gate_shape, LRU_HEAD_DIM ** -0.5),
        'lru_bx': normal((N_EVEN, 2, LRU_WIDTH), 0.02),
        'lru_lam': jnp.log(lru_a) - jnp.log1p(-lru_a),
        'ssd_w_in': normal((N_ODD, d, SSD_IN), d ** -0.5),
        'ssd_conv_w': normal((N_ODD, CONV_W, SSD_CONV_CH), CONV_W ** -0.5),
        'ssd_conv_b': normal((N_ODD, SSD_CONV_CH), 0.02),
        'ssd_dt_bias': dt0 + jnp.log(-jnp.expm1(-dt0)),
        'ssd_a_log': jnp.log(uniform((N_ODD, 2, SSD_HEADS), 1.0, 16.0)),
        'ssd_d': gain((N_ODD, SSD_HEADS)),
        'ssd_norm_g': gain((N_ODD, SSD_INNER)),
        'ssd_w_out': normal((N_ODD, SSD_INNER, d), SSD_INNER ** -0.5),
        'moe_wg': normal((DEPTH, d, MOE_GROUPS), d ** -0.5),
        'moe_bg': normal((DEPTH, MOE_GROUPS), 0.01),
        'moe_we': normal((DEPTH, d, MOE_EXPERTS), d ** -0.5),
        'moe_be': normal((DEPTH, MOE_EXPERTS), 0.01),
        'moe_w1': normal((DEPTH, MOE_EXPERTS, d, MOE_FF), d ** -0.5),
        'moe_w3': normal((DEPTH, MOE_EXPERTS, d, MOE_FF), d ** -0.5),
        'moe_w2': normal((DEPTH, MOE_EXPERTS, MOE_FF, d), MOE_FF ** -0.5),
        'final_norm_g': gain((d,)),
    }


def reference(x, c, ctx, c_ctx, norm_mix_g, norm_ffn_g, mod_w, mod_b, hy_w_in, hy_w_out,
              s5_lam_re, s5_lam_im, s5_log_dt, s5_b_re, s5_b_im, s5_c_re, s5_c_im, s5_d,
              s5_glu_w, s5_glu_b, lru_conv_w, lru_conv_b, lru_wa, lru_ba, lru_wx, lru_bx, lru_lam,
              ssd_w_in, ssd_conv_w, ssd_conv_b, ssd_dt_bias, ssd_a_log, ssd_d, ssd_norm_g, ssd_w_out,
              moe_wg, moe_bg, moe_we, moe_be, moe_w1, moe_w3, moe_w2, final_norm_g):
    bsz, n_lat, d = x.shape
    rows = n_lat // GRID_W
    for layer in range(DEPTH):
        last = layer == DEPTH - 1
        mod_x = [m[:, None, :] for m in
                 jnp.split(jax.nn.silu(c) @ mod_w[layer] + mod_b[layer], N_MOD, axis=-1)]
        mod_c = jnp.split(jax.nn.silu(c_ctx) @ mod_w[layer] + mod_b[layer], N_MOD, axis=-1)
        hx = modulate(rms_norm(x, norm_mix_g[layer]), mod_x[0], mod_x[1])
        hc = modulate(rms_norm(ctx, norm_mix_g[layer]), mod_c[0], mod_c[1])
        if layer % 2 == 0:
            e = layer // 2
            dx, dc = s5_lru_layer(hx, hc, hy_w_in[e], hy_w_out[e], s5_lam_re[e], s5_lam_im[e],
                                  s5_log_dt[e], s5_b_re[e], s5_b_im[e], s5_c_re[e], s5_c_im[e],
                                  s5_d[e], s5_glu_w[e], s5_glu_b[e], lru_conv_w[e], lru_conv_b[e],
                                  lru_wa[e], lru_ba[e], lru_wx[e], lru_bx[e], lru_lam[e], not last)
        else:
            o = layer // 2
            dx, dc = ssd_layer(to_column_major(hx, rows), hc, ssd_w_in[o], ssd_conv_w[o],
                               ssd_conv_b[o], ssd_dt_bias[o], ssd_a_log[o], ssd_d[o],
                               ssd_norm_g[o], ssd_w_out[o], not last)
            dx = to_row_major(dx, rows)
        x = x + mod_x[2] * dx
        hx = modulate(rms_norm(x, norm_ffn_g[layer]), mod_x[3], mod_x[4]).reshape(bsz * n_lat, d)
        moe_p = (moe_wg[layer], moe_bg[layer], moe_we[layer], moe_be[layer],
                 moe_w1[layer], moe_w3[layer], moe_w2[layer])
        if last:
            x = x + mod_x[5] * hier_moe(hx, *moe_p).reshape(bsz, n_lat, d)
        else:
            ctx = ctx + mod_c[2] * dc
            hc = modulate(rms_norm(ctx, norm_ffn_g[layer]), mod_c[3], mod_c[4]).reshape(-1, d)
            f = hier_moe(jnp.concatenate([hx, hc], axis=0), *moe_p)
            x = x + mod_x[5] * f[:bsz * n_lat].reshape(bsz, n_lat, d)
            ctx = ctx + mod_c[5] * f[bsz * n_lat:].reshape(ctx.shape)
    return rms_norm(x, final_norm_g)
```

```python
import functools
import math

import jax
import jax.numpy as jnp
from jax import lax
from jax.experimental import pallas as pl
from jax.experimental.pallas import tpu as pltpu

F32 = jnp.float32
BF16 = jnp.bfloat16
HIGHEST = lax.Precision.HIGHEST

GRID_W = 64
N_MOD = 6
RMS_EPS = 1e-6
CONV_W = 4
CONV_PAD_LEFT = CONV_W // 2
S5_GROUP = 16
S5_STATE = 64
LRU_HEADS = 16
LRU_C = 8.0
SSD_HEAD_DIM = 64
SSD_GROUPS = 8
SSD_STATE = 128
SSD_CHUNK = 128
MOE_GROUPS = 4
MOE_PER_GROUP = 8
MOE_EXPERTS = MOE_GROUPS * MOE_PER_GROUP
MOE_TOPK = 2

V7X_LANES = 128
V7X_SUBLANES = 8
V7X_MXU_DIM = 256
V7X_VMEM_LIMIT_BYTES = 60000 * 1024

TOKEN_TILE = 512
S5_CHUNK = V7X_MXU_DIM // S5_GROUP
S5_PAIR_LANES = 2 * S5_STATE
S5_SCAN_LANES = 512
LRU_TILE = 256
LRU_GATE_BLOCK = V7X_MXU_DIM
HALO = V7X_SUBLANES
MOE_BLOCK = 256
ROUTE_LANES = V7X_LANES


def _params(sem):
    return pltpu.CompilerParams(dimension_semantics=sem, vmem_limit_bytes=V7X_VMEM_LIMIT_BYTES)


def _silu(v):
    return v * jax.nn.sigmoid(v)


def _gelu(v):
    return jax.nn.gelu(v, approximate=True)


def _mod_kernel(c_ref, w_ref, b_ref, o_ref):
    s = _silu(c_ref[...])
    o_ref[...] = jnp.dot(s, w_ref[...], preferred_element_type=F32, precision=HIGHEST) + b_ref[...]


def _mod_vectors(cvec, mod_w, mod_b):
    depth, d, n = mod_w.shape
    tn = n // 8
    return pl.pallas_call(
        _mod_kernel,
        grid=(depth, n // tn),
        in_specs=[pl.BlockSpec((8, d), lambda l, j: (0, 0)),
                  pl.BlockSpec((None, d, tn), lambda l, j: (l, 0, j)),
                  pl.BlockSpec((None, 1, tn), lambda l, j: (l, 0, j))],
        out_specs=pl.BlockSpec((None, 8, tn), lambda l, j: (l, 0, j)),
        out_shape=jax.ShapeDtypeStruct((depth, 8, n), F32),
        compiler_params=_params(("arbitrary", "arbitrary")),
        name="mod_vectors",
    )(cvec, mod_w, mod_b.reshape(depth, 1, n))


def _mod_slab(mods_layer, n_rows, d):
    m = mods_layer[:n_rows].reshape(n_rows, N_MOD, d)
    return jnp.pad(m, ((0, 0), (0, 8 - N_MOD), (0, 0)))


def _norm_mod(x, g, m_ref, shift_row, scale_row):
    ms = jnp.mean(x * x, axis=-1, keepdims=True)
    y = x * lax.rsqrt(ms + RMS_EPS) * g
    return y * (1.0 + m_ref[scale_row:scale_row + 1, :]) + m_ref[shift_row:shift_row + 1, :]


def _norm_mm_kernel(x_ref, g_ref, m_ref, w_ref, o_ref, xn_ref, *, shift_row, scale_row):
    @pl.when(pl.program_id(1) == 0)
    def _():
        xn_ref[...] = _norm_mod(x_ref[...], g_ref[...], m_ref, shift_row, scale_row).astype(BF16)

    o_ref[...] = jnp.dot(xn_ref[...], w_ref[...], preferred_element_type=F32)


def _norm_matmul(xa, g, slab, w, *, shift_row, scale_row, tn, tiles_per_batch):
    t, d = xa.shape
    n = w.shape[1]
    tm = TOKEN_TILE
    last_row = slab.shape[0] - 1
    return pl.pallas_call(
        functools.partial(_norm_mm_kernel, shift_row=shift_row, scale_row=scale_row),
        grid=(t // tm, n // tn),
        in_specs=[pl.BlockSpec((tm, d), lambda i, j: (i, 0)),
                  pl.BlockSpec((1, d), lambda i, j: (0, 0)),
                  pl.BlockSpec((None, 8, d),
                               lambda i, j: (jnp.minimum(i // tiles_per_batch, last_row), 0, 0)),
                  pl.BlockSpec((d, tn), lambda i, j: (0, j))],
        out_specs=pl.BlockSpec((tm, tn), lambda i, j: (i, j)),
        out_shape=jax.ShapeDtypeStruct((t, n), F32),
        scratch_shapes=[pltpu.VMEM((tm, d), BF16)],
        compiler_params=_params(("arbitrary", "arbitrary")),
        name="norm_matmul",
    )(xa, g.reshape(1, d), slab, w)


def _s5_weights(lam_re, lam_im, log_dt, b_re, b_im, c_re, c_im):
    q = S5_CHUNK
    ngrp, nst = lam_re.shape[1], lam_re.shape[2]
    nch = b_re.shape[-1]
    lr, li = lam_re.astype(F32), lam_im.astype(F32)
    dt = jnp.exp(log_dt.astype(F32))[..., None]
    mag = jnp.exp(lr * dt)
    a_re, a_im = mag * jnp.cos(li * dt), mag * jnp.sin(li * dt)
    den = lr * lr + li * li
    k_re = ((a_re - 1) * lr + a_im * li) / den
    k_im = (a_im * lr - (a_re - 1) * li) / den
    bb_re = k_re[..., None] * b_re - k_im[..., None] * b_im
    bb_im = k_re[..., None] * b_im + k_im[..., None] * b_re
    ks = jnp.arange(q + 1, dtype=F32)[:, None, None, None]
    pmag = jnp.exp(ks * (lr * dt))
    pw_re, pw_im = pmag * jnp.cos(ks * (li * dt)), pmag * jnp.sin(ks * (li * dt))
    ab_re = pw_re[..., None] * bb_re - pw_im[..., None] * bb_im
    ab_im = pw_re[..., None] * bb_im + pw_im[..., None] * bb_re
    kmat = (jnp.einsum('dgop,kdgpi->dgkio', c_re, ab_re, precision=HIGHEST)
            - jnp.einsum('dgop,kdgpi->dgkio', c_im, ab_im, precision=HIGHEST))
    s_idx = jnp.arange(q)[:, None]
    t_idx = jnp.arange(q)[None, :]
    lag_f = t_idx - s_idx
    lag_b = s_idx - t_idx
    kf = jnp.where((lag_f >= 0)[None, :, :, None, None], kmat[0][:, jnp.clip(lag_f, 0, q - 1)], 0.0)
    kb = jnp.where((lag_b >= 0)[None, :, :, None, None], kmat[1][:, jnp.clip(lag_b, 0, q - 1)], 0.0)
    toep = (kf + kb).transpose(0, 1, 3, 2, 4).reshape(ngrp, q * nch, q * nch)
    sf_re = ab_re[q - 1 - jnp.arange(q), 0]
    sf_im = ab_im[q - 1 - jnp.arange(q), 0]
    sb_re = ab_re[jnp.arange(q), 1]
    sb_im = ab_im[jnp.arange(q), 1]
    summ = jnp.stack([sf_re, sf_im, sb_re, sb_im], 0)
    summ = summ.transpose(2, 1, 4, 0, 3).reshape(ngrp, q * nch, 4, nst)
    npair = ngrp // 2
    summ = summ.reshape(npair, 2, q * nch, 4, nst)
    zero = jnp.zeros_like(summ[:, 0])
    wa = jnp.concatenate([jnp.concatenate([summ[:, 0], zero], -1),
                          jnp.concatenate([zero, summ[:, 1]], -1)], 1)
    wa = wa.reshape(npair, 2 * q * nch, 4 * 2 * nst)
    cp_re = c_re[None] * pw_re[:, :, :, None, :] - c_im[None] * pw_im[:, :, :, None, :]
    cp_im = c_re[None] * pw_im[:, :, :, None, :] + c_im[None] * pw_re[:, :, :, None, :]
    tf = jnp.arange(q) + 1
    tb = q - jnp.arange(q)
    cr = jnp.stack([cp_re[tf, 0], -cp_im[tf, 0], cp_re[tb, 1], -cp_im[tb, 1]], 0)
    cr = cr.transpose(2, 0, 4, 1, 3).reshape(ngrp, 4, nst, q * nch)
    cr = cr.reshape(npair, 2, 4, nst, q * nch)
    zc = jnp.zeros_like(cr[:, 0])
    wc0 = jnp.concatenate([cr[:, 0], zc], 2)
    wc1 = jnp.concatenate([zc, cr[:, 1]], 2)
    wc = jnp.stack([wc0, wc1], 1).reshape(ngrp, 4 * 2 * nst, q * nch)
    a16 = jnp.stack([pw_re[q, 0], pw_im[q, 0], pw_re[q, 1], pw_im[q, 1]], 0)
    a16 = jnp.pad(a16.reshape(4, ngrp * nst), ((0, 4), (0, 0)))
    return toep.astype(BF16), wa.astype(BF16), wc.astype(BF16), a16


def _s5_summary_kernel(u_ref, w_ref, o_ref):
    x = jnp.concatenate([u_ref[0], u_ref[1]], axis=1)
    r = jnp.dot(x, w_ref[...], preferred_element_type=F32)
    for qd in range(4):
        o_ref[qd] = r[:, qd * S5_PAIR_LANES:(qd + 1) * S5_PAIR_LANES]


def _s5_scan_kernel(s_ref, a_ref, h_ref, *, nb, n_lat, n_ctx):
    afr, afi = a_ref[0:1, :], a_ref[1:2, :]
    abr, abi = a_ref[2:3, :], a_ref[3:4, :]
    zero = jnp.zeros_like(afr)

    def step(rf, rb, carry):
        hfr, hfi, hbr, hbi = carry
        h_ref[0, pl.ds(rf, 1), :] = hfr
        h_ref[1, pl.ds(rf, 1), :] = hfi
        h_ref[2, pl.ds(rb, 1), :] = hbr
        h_ref[3, pl.ds(rb, 1), :] = hbi
        sfr, sfi = s_ref[0, pl.ds(rf, 1), :], s_ref[1, pl.ds(rf, 1), :]
        sbr, sbi = s_ref[2, pl.ds(rb, 1), :], s_ref[3, pl.ds(rb, 1), :]
        return (afr * hfr - afi * hfi + sfr, afr * hfi + afi * hfr + sfi,
                abr * hbr - abi * hbi + sbr, abr * hbi + abi * hbr + sbi)

    for b in range(nb):
        lat0 = b * n_lat
        ctx0 = nb * n_lat + b * n_ctx
        carry = lax.fori_loop(
            0, n_ctx, lambda i, c: step(ctx0 + i, ctx0 + n_ctx - 1 - i, c), (zero, zero, zero, zero))
        lax.fori_loop(0, n_lat, lambda i, c: step(lat0 + i, lat0 + n_lat - 1 - i, c), carry)


def _s5_output_kernel(u_ref, t_ref, h_ref, w_ref, o_ref):
    hcat = jnp.concatenate([h_ref[0], h_ref[1], h_ref[2], h_ref[3]], axis=1).astype(BF16)
    o_ref[...] = (jnp.dot(u_ref[...], t_ref[...], preferred_element_type=F32)
                  + jnp.dot(hcat, w_ref[...], preferred_element_type=F32))


def _s5_mixer(u, weights, *, nb, n_lat, n_ctx):
    toep, wa, wc, a16 = weights
    t, width = u.shape
    q, nch = S5_CHUNK, S5_GROUP
    ngrp = width // nch
    nc = t // q
    cw = q * nch
    lanes = ngrp * S5_STATE
    ug = u.astype(BF16).reshape(nc, q, ngrp, nch).transpose(2, 0, 1, 3).reshape(ngrp, nc, cw)
    summ = pl.pallas_call(
        _s5_summary_kernel,
        grid=(ngrp // 2,),
        in_specs=[pl.BlockSpec((2, nc, cw), lambda p: (p, 0, 0)),
                  pl.BlockSpec((None, 2 * cw, 4 * S5_PAIR_LANES), lambda p: (p, 0, 0))],
        out_specs=pl.BlockSpec((4, nc, S5_PAIR_LANES), lambda p: (0, 0, p)),
        out_shape=jax.ShapeDtypeStruct((4, nc, lanes), F32),
        compiler_params=_params(("arbitrary",)),
        name="s5_summary",
    )(ug, wa)
    wl = S5_SCAN_LANES
    carry = pl.pallas_call(
        functools.partial(_s5_scan_kernel, nb=nb, n_lat=n_lat // q, n_ctx=n_ctx // q),
        grid=(lanes // wl,),
        in_specs=[pl.BlockSpec((4, nc, wl), lambda j: (0, 0, j)),
                  pl.BlockSpec((8, wl), lambda j: (0, j))],
        out_specs=pl.BlockSpec((4, nc, wl), lambda j: (0, 0, j)),
        out_shape=jax.ShapeDtypeStruct((4, nc, lanes), F32),
        compiler_params=_params(("arbitrary",)),
        name="s5_scan",
    )(summ, a16)
    yg = pl.pallas_call(
        _s5_output_kernel,
        grid=(ngrp,),
        in_specs=[pl.BlockSpec((None, nc, cw), lambda g: (g, 0, 0)),
                  pl.BlockSpec((None, cw, cw), lambda g: (g, 0, 0)),
                  pl.BlockSpec((4, nc, S5_PAIR_LANES), lambda g: (0, 0, g // 2)),
                  pl.BlockSpec((None, 4 * S5_PAIR_LANES, cw), lambda g: (g, 0, 0))],
        out_specs=pl.BlockSpec((None, nc, cw), lambda g: (g, 0, 0)),
        out_shape=jax.ShapeDtypeStruct((ngrp, nc, cw), F32),
        compiler_params=_params(("arbitrary",)),
        name="s5_output",
    )(ug, toep, carry, wc)
    return yg.reshape(ngrp, nc, q, nch).transpose(1, 2, 0, 3).reshape(t, width)


def _dwconv_tile(pad_ref, main, prev, nxt, first, last, cw_ref, cb, rows):
    pad_ref[0:HALO, :] = jnp.where(first, 0.0, prev)
    pad_ref[HALO:HALO + rows, :] = main
    pad_ref[HALO + rows:2 * HALO + rows, :] = jnp.where(last, 0.0, nxt)
    acc = cb
    for k in range(CONV_W):
        off = HALO - CONV_PAD_LEFT + k
        acc = acc + cw_ref[k:k + 1, :] * pad_ref[off:off + rows, :]
    return acc


def _lru_kernel(v_ref, vp_ref, vn_ref, cw_ref, cb_ref, wa_ref, wx_ref, ba_ref, bx_ref, sp_ref,
                o_ref, pad_ref, a_ref, b_ref, h_ref, *, n_lat_tiles):
    d = pl.program_id(0)
    k = pl.program_id(2)
    tc = LRU_TILE
    j = jnp.where(d == 0, k - 1, n_lat_tiles - k)
    first = jnp.logical_or(k == 0, j == 0)
    last = jnp.logical_or(k == 0, j == n_lat_tiles - 1)
    vc = _dwconv_tile(pad_ref, v_ref[...], vp_ref[...], vn_ref[...], first, last, cw_ref,
                      cb_ref[...], tc)
    vcb = vc.astype(BF16)
    nblk = vc.shape[1] // LRU_GATE_BLOCK

    def gate(w_ref, bias):
        parts = [jnp.dot(vcb[:, m * LRU_GATE_BLOCK:(m + 1) * LRU_GATE_BLOCK], w_ref[m],
                         preferred_element_type=F32) for m in range(nblk)]
        return jax.nn.sigmoid(jnp.concatenate(parts, axis=1) + bias)

    r = gate(wa_ref, ba_ref[...])
    i = gate(wx_ref, bx_ref[...])
    a = jnp.exp(-LRU_C * r * sp_ref[...])
    a_ref[...] = a
    b_ref[...] = jnp.sqrt(1.0 - a * a) * (i * vc)

    @pl.when(k == 0)
    def _():
        h_ref[...] = jnp.zeros_like(h_ref)

    def body(t, h):
        tt = jnp.where(d == 0, t, tc - 1 - t)
        h = a_ref[pl.ds(tt, 1), :] * h + b_ref[pl.ds(tt, 1), :]
        o_ref[pl.ds(tt, 1), :] = h
        return h

    h_ref[0:1, :] = lax.fori_loop(0, tc, body, h_ref[0:1, :], unroll=8)


def _lru_scan(hz, col_block, conv_w, conv_b, wa, wx, ba, bx, sp, *, nb, n_lat, n_ctx):
    t = hz.shape[0]
    width = conv_w.shape[1]
    tc = LRU_TILE
    assert n_ctx == tc and n_lat % tc == 0
    nlt = n_lat // tc
    hb = tc // HALO
    n_halo_blocks = t // HALO

    def row_block(d, b, k):
        j = jnp.where(d == 0, k - 1, nlt - k)
        return jnp.where(k == 0, nb * nlt + b, b * nlt + j)

    cw = jnp.pad(conv_w, ((0, 8 - CONV_W), (0, 0)))
    nblk = width // LRU_GATE_BLOCK
    vec = lambda: pl.BlockSpec((None, 1, width), lambda d, b, k: (d, 0, 0))
    return pl.pallas_call(
        functools.partial(_lru_kernel, n_lat_tiles=nlt),
        grid=(2, nb, nlt + 1),
        in_specs=[pl.BlockSpec((tc, width), lambda d, b, k: (row_block(d, b, k), col_block)),
                  pl.BlockSpec((HALO, width),
                               lambda d, b, k: (jnp.maximum(row_block(d, b, k) * hb - 1, 0), col_block)),
                  pl.BlockSpec((HALO, width),
                               lambda d, b, k: (jnp.minimum(row_block(d, b, k) * hb + hb,
                                                            n_halo_blocks - 1), col_block)),
                  pl.BlockSpec((8, width), lambda d, b, k: (0, 0)),
                  pl.BlockSpec((1, width), lambda d, b, k: (0, 0)),
                  pl.BlockSpec((None, nblk, LRU_GATE_BLOCK, LRU_GATE_BLOCK), lambda d, b, k: (d, 0, 0, 0)),
                  pl.BlockSpec((None, nblk, LRU_GATE_BLOCK, LRU_GATE_BLOCK), lambda d, b, k: (d, 0, 0, 0)),
                  vec(), vec(), vec()],
        out_specs=pl.BlockSpec((None, tc, width), lambda d, b, k: (d, row_block(d, b, k), 0)),
        out_shape=jax.ShapeDtypeStruct((2, t, width), F32),
        scratch_shapes=[pltpu.VMEM((tc + 2 * HALO, width), F32),
                        pltpu.VMEM((tc, width), F32),
                        pltpu.VMEM((tc, width), F32),
                        pltpu.VMEM((8, width), F32)],
        compiler_params=_params(("arbitrary", "arbitrary", "arbitrary")),
        name="rglru_scan",
    )(hz, hz, hz, cw, conv_b.reshape(1, width), wa, wx, ba, bx, sp)


def _block_diag_gates(w):
    ndir, heads, hd, _ = w.shape
    per = LRU_GATE_BLOCK // hd
    w = w.reshape(ndir, heads // per, per, hd, hd)
    eye = jnp.eye(per, dtype=w.dtype)
    full = jnp.einsum('dmhij,hk->dmhikj', w, eye)
    return full.reshape(ndir, heads // per, per * hd, per * hd).astype(BF16)


def _out0_kernel(ys_ref, u_ref, g_ref, hf_ref, hb_ref, x_ref, m_ref, d_ref, gw_ref, gb_ref,
                 wt_ref, wb_ref, o_ref):
    y = ys_ref[...] + d_ref[...] * u_ref[...]
    z = _gelu(y)
    gate = jax.nn.sigmoid(jnp.dot(z.astype(BF16), gw_ref[...], preferred_element_type=F32) + gb_ref[...])
    a = z * gate
    r = _gelu(g_ref[...]) * (hf_ref[...] + hb_ref[...])
    dx = (jnp.dot(a.astype(BF16), wt_ref[...], preferred_element_type=F32)
          + jnp.dot(r.astype(BF16), wb_ref[...], preferred_element_type=F32))
    o_ref[...] = x_ref[...] + m_ref[2:3, :] * dx


def _out0(ys, hz, hfb, xa, slab, d_skip, glu_w, glu_b, w_out, *, tiles_per_batch):
    t, d = xa.shape
    width = ys.shape[1]
    tm = TOKEN_TILE // 2
    last_row = slab.shape[0] - 1
    tpb = tiles_per_batch * (TOKEN_TILE // tm)
    const = lambda shape: pl.BlockSpec(shape, lambda i: tuple(0 for _ in shape))
    return pl.pallas_call(
        _out0_kernel,
        grid=(t // tm,),
        in_specs=[pl.BlockSpec((tm, width), lambda i: (i, 0)),
                  pl.BlockSpec((tm, width), lambda i: (i, 0)),
                  pl.BlockSpec((tm, width), lambda i: (i, 1)),
                  pl.BlockSpec((None, tm, width), lambda i: (0, i, 0)),
                  pl.BlockSpec((None, tm, width), lambda i: (1, i, 0)),
                  pl.BlockSpec((tm, d), lambda i: (i, 0)),
                  pl.BlockSpec((None, 8, d), lambda i: (jnp.minimum(i // tpb, last_row), 0, 0)),
                  const((1, width)), const((width, width)), const((1, width)),
                  const((width, d)), const((width, d))],
        out_specs=pl.BlockSpec((tm, d), lambda i: (i, 0)),
        out_shape=jax.ShapeDtypeStruct((t, d), F32),
        compiler_params=_params(("arbitrary",)),
        name="hybrid_out",
    )(ys, hz, hz, hfb, hfb, xa, slab, d_skip.reshape(1, width), glu_w.astype(BF16),
      glu_b.reshape(1, width), w_out[:width].astype(BF16), w_out[width:].astype(BF16))


def _route_kernel(x_ref, g_ref, m_ref, w_ref, b_ref, h_ref, r_ref):
    h = _norm_mod(x_ref[...], g_ref[...], m_ref, 3, 4)
    h_ref[...] = h
    logits = jnp.dot(h, w_ref[...], preferred_element_type=F32, precision=HIGHEST) + b_ref[...]
    lane = lax.broadcasted_iota(jnp.int32, logits.shape, 1).astype(F32)
    ninf = -jnp.inf
    big = float(ROUTE_LANES)
    lg = jnp.where(lane < MOE_GROUPS, logits, ninf)
    mg = jnp.max(lg, axis=-1, keepdims=True)
    gidx = jnp.min(jnp.where(lg == mg, lane, big), axis=-1, keepdims=True)
    p_top = 1.0 / jnp.sum(jnp.exp(lg - mg), axis=-1, keepdims=True)
    lo = MOE_GROUPS + gidx * MOE_PER_GROUP
    le = jnp.where(jnp.logical_and(lane >= lo, lane < lo + MOE_PER_GROUP), logits, ninf)
    v1 = jnp.max(le, axis=-1, keepdims=True)
    i1 = jnp.min(jnp.where(le == v1, lane, big), axis=-1, keepdims=True)
    le2 = jnp.where(lane == i1, ninf, le)
    v2 = jnp.max(le2, axis=-1, keepdims=True)
    i2 = jnp.min(jnp.where(le2 == v2, lane, big), axis=-1, keepdims=True)
    tt = jnp.exp(v2 - v1)
    g1 = p_top / (1.0 + tt)
    g2 = p_top * tt / (1.0 + tt)
    out = jnp.where(lane == 0, i1 - MOE_GROUPS, 0.0)
    out = jnp.where(lane == 1, i2 - MOE_GROUPS, out)
    out = jnp.where(lane == 2, g1, out)
    out = jnp.where(lane == 3, g2, out)
    r_ref[...] = out


def _route(xa, g, slab, wg, bg, we, be, *, tiles_per_batch):
    t, d = xa.shape
    tm = TOKEN_TILE
    nl = MOE_GROUPS + MOE_EXPERTS
    wr = jnp.pad(jnp.concatenate([wg, we], axis=1), ((0, 0), (0, ROUTE_LANES - nl)))
    br = jnp.pad(jnp.concatenate([bg, be], axis=0), (0, ROUTE_LANES - nl)).reshape(1, ROUTE_LANES)
    last_row = slab.shape[0] - 1
    return pl.pallas_call(
        _route_kernel,
        grid=(t // tm,),
        in_specs=[pl.BlockSpec((tm, d), lambda i: (i, 0)),
                  pl.BlockSpec((1, d), lambda i: (0, 0)),
                  pl.BlockSpec((None, 8, d), lambda i: (jnp.minimum(i // tiles_per_batch, last_row), 0, 0)),
                  pl.BlockSpec((d, ROUTE_LANES), lambda i: (0, 0)),
                  pl.BlockSpec((1, ROUTE_LANES), lambda i: (0, 0))],
        out_specs=[pl.BlockSpec((tm, d), lambda i: (i, 0)),
                   pl.BlockSpec((tm, ROUTE_LANES), lambda i: (i, 0))],
        out_shape=[jax.ShapeDtypeStruct((t, d), F32),
                   jax.ShapeDtypeStruct((t, ROUTE_LANES), F32)],
        compiler_params=_params(("arbitrary",)),
        name="moe_route",
    )(xa, g.reshape(1, d), slab, wr, br)


def _dispatch(experts):
    n_pairs = experts.size
    bm = MOE_BLOCK
    flat_e = experts.reshape(-1)
    onehot = (flat_e[:, None] == jnp.arange(MOE_EXPERTS, dtype=jnp.int32)[None, :]).astype(jnp.int32)
    csum = jnp.cumsum(onehot, axis=0)
    rank = jnp.take_along_axis(csum, flat_e[:, None], axis=1)[:, 0] - 1
    counts = csum[-1]
    padded = (counts + bm - 1) // bm * bm
    pad_end = jnp.cumsum(padded)
    dest = (pad_end - padded)[flat_e] + rank
    n_blocks = -(-n_pairs // bm) + MOE_EXPERTS
    slot_tok = jnp.zeros((n_blocks * bm,), jnp.int32).at[dest].set(
        jnp.arange(n_pairs, dtype=jnp.int32) // MOE_TOPK)
    block_e = jnp.minimum(
        jnp.searchsorted(pad_end, jnp.arange(n_blocks, dtype=jnp.int32) * bm, side='right'),
        MOE_EXPERTS - 1).astype(jnp.int32)
    n_used = (pad_end[-1] // bm).astype(jnp.int32).reshape(1)
    return dest.astype(jnp.int32), slot_tok, block_e, n_used, n_blocks


def _row_copy(src_hbm, row, dst, slot, r, sem):
    return pltpu.make_async_copy(src_hbm.at[pl.ds(row, 1), :], dst.at[slot, pl.ds(r, 1), :], sem.at[slot])


def _moe_kernel(be_ref, nu_ref, tok_ref, tokn_ref, h_hbm, w1_ref, w3_ref, w2_ref, o_ref,
                xbuf, sem, w1b, w3b, w2b, prev_ref):
    b = pl.program_id(0)
    bm = MOE_BLOCK
    n_used = nu_ref[0]
    slot = lax.rem(b, 2)

    def start_rows(idx_ref, s):
        def body(r, c):
            _row_copy(h_hbm, idx_ref[0, r], xbuf, s, r, sem).start()
            return c
        lax.fori_loop(0, bm, body, 0)

    @pl.when(b == 0)
    def _():
        prev_ref[0] = -1
        start_rows(tok_ref, 0)

    @pl.when(b + 1 < n_used)
    def _():
        start_rows(tokn_ref, 1 - slot)

    @pl.when(b < n_used)
    def _():
        e = be_ref[b]

        @pl.when(e != prev_ref[0])
        def _():
            w1b[...] = w1_ref[...].astype(BF16)
            w3b[...] = w3_ref[...].astype(BF16)
            w2b[...] = w2_ref[...].astype(BF16)
            prev_ref[0] = e

        def wait_body(r, c):
            _row_copy(h_hbm, 0, xbuf, slot, r, sem).wait()
            return c
        lax.fori_loop(0, bm, wait_body, 0)
        x = xbuf[slot].astype(BF16)
        h1 = jnp.dot(x, w1b[...], preferred_element_type=F32)
        h3 = jnp.dot(x, w3b[...], preferred_element_type=F32)
        act = (_silu(h1) * h3).astype(BF16)
        o_ref[...] = jnp.dot(act, w2b[...], preferred_element_type=F32)

    @pl.when(b >= n_used)
    def _():
        o_ref[...] = jnp.zeros_like(o_ref)


def _moe_experts(h, slot_tok, block_e, n_used, n_blocks, w1, w3, w2):
    t, d = h.shape
    ff = w1.shape[-1]
    bm = MOE_BLOCK
    tok3 = slot_tok.reshape(n_blocks, 1, bm)
    smem_block = lambda fn: pl.BlockSpec((None, 1, bm), fn, memory_space=pltpu.SMEM)
    grid_spec = pltpu.PrefetchScalarGridSpec(
        num_scalar_prefetch=2,
        grid=(n_blocks,),
        in_specs=[smem_block(lambda b, be, nu: (b, 0, 0)),
                  smem_block(lambda b, be, nu: (jnp.minimum(b + 1, n_blocks - 1), 0, 0)),
                  pl.BlockSpec(memory_space=pl.ANY),
                  pl.BlockSpec((None, d, ff), lambda b, be, nu: (be[b], 0, 0)),
                  pl.BlockSpec((None, d, ff), lambda b, be, nu: (be[b], 0, 0)),
                  pl.BlockSpec((None, ff, d), lambda b, be, nu: (be[b], 0, 0))],
        out_specs=pl.BlockSpec((bm, d), lambda b, be, nu: (b, 0)),
        scratch_shapes=[pltpu.VMEM((2, bm, d), F32),
                        pltpu.SemaphoreType.DMA((2,)),
                        pltpu.VMEM((d, ff), BF16),
                        pltpu.VMEM((d, ff), BF16),
                        pltpu.VMEM((ff, d), BF16),
                        pltpu.SMEM((1,), jnp.int32)],
    )
    return pl.pallas_call(
        _moe_kernel,
        grid_spec=grid_spec,
        out_shape=jax.ShapeDtypeStruct((n_blocks * bm, d), F32),
        compiler_params=_params(("arbitrary",)),
        name="moe_experts",
    )(block_e, n_used, tok3, tok3, h, w1, w3, w2)


def _combine_kernel(d_ref, dn_ref, yb_hbm, x_ref, r_ref, m_ref, o_ref, ybuf, sem, *, n_tiles):
    i = pl.program_id(0)
    tm = x_ref.shape[0]
    slot = lax.rem(i, 2)

    def start_rows(idx_ref, s):
        def body(r, c):
            _row_copy(yb_hbm, idx_ref[0, r], ybuf, s, r, sem).start()
            return c
        lax.fori_loop(0, MOE_TOPK * tm, body, 0)

    @pl.when(i == 0)
    def _():
        start_rows(d_ref, 0)

    @pl.when(i + 1 < n_tiles)
    def _():
        start_rows(dn_ref, 1 - slot)

    def wait_body(r, c):
        _row_copy(yb_hbm, 0, ybuf, slot, r, sem).wait()
        return c
    lax.fori_loop(0, MOE_TOPK * tm, wait_body, 0)
    r = r_ref[...]
    y = r[:, 2:3] * ybuf[slot, 0:tm, :] + r[:, 3:4] * ybuf[slot, tm:2 * tm, :]
    o_ref[...] = x_ref[...] + m_ref[5:6, :] * y


def _moe_combine(yb, dest, xa, route, slab, *, tiles_per_batch):
    t, d = xa.shape
    tm = TOKEN_TILE // 2
    n_tiles = t // tm
    tpb = tiles_per_batch * (TOKEN_TILE // tm)
    last_row = slab.shape[0] - 1
    d3 = dest.reshape(n_tiles, tm, MOE_TOPK).transpose(0, 2, 1).reshape(n_tiles, 1, MOE_TOPK * tm)
    smem_block = lambda fn: pl.BlockSpec((None, 1, MOE_TOPK * tm), fn, memory_space=pltpu.SMEM)
    return pl.pallas_call(
        functools.partial(_combine_kernel, n_tiles=n_tiles),
        grid=(n_tiles,),
        in_specs=[smem_block(lambda i: (i, 0, 0)),
                  smem_block(lambda i: (jnp.minimum(i + 1, n_tiles - 1), 0, 0)),
                  pl.BlockSpec(memory_space=pl.ANY),
                  pl.BlockSpec((tm, d), lambda i: (i, 0)),
                  pl.BlockSpec((tm, ROUTE_LANES), lambda i: (i, 0)),
                  pl.BlockSpec((None, 8, d), lambda i: (jnp.minimum(i // tpb, last_row), 0, 0))],
        out_specs=pl.BlockSpec((tm, d), lambda i: (i, 0)),
        out_shape=jax.ShapeDtypeStruct((t, d), F32),
        scratch_shapes=[pltpu.VMEM((2, MOE_TOPK * tm, d), F32),
                        pltpu.SemaphoreType.DMA((2,))],
        compiler_params=_params(("arbitrary",)),
        name="moe_combine",
    )(d3, d3, yb, xa, route, slab)


def _moe_layer(xa, g, slab, wg, bg, we, be, w1, w3, w2, *, tiles_per_batch):
    h, route = _route(xa, g, slab, wg, bg, we, be, tiles_per_batch=tiles_per_batch)
    experts = route[:, 0:MOE_TOPK].astype(jnp.int32)
    dest, slot_tok, block_e, n_used, n_blocks = _dispatch(experts)
    yb = _moe_experts(h, slot_tok, block_e, n_used, n_blocks, w1, w3, w2)
    return _moe_combine(yb, dest, xa, route, slab, tiles_per_batch=tiles_per_batch)


def _ssd_kernel(xs_ref, xsp_ref, xsn_ref, bm_ref, bmp_ref, bmn_ref, cm_ref, cmp_ref, cmn_ref,
                dt_ref, cwx_ref, cbx_ref, cwb_ref, cbb_ref, cwc_ref, cbc_ref, dtb_ref, a_ref, dsk_ref,
                o_ref, padx, padb, padc, xpair, ypair, cbs, bts, cs, acst, dtt, wtt, tott, state,
                *, rev, n_ctx_chunks, n_lat_chunks):
    k = pl.program_id(1)
    q = SSD_CHUNK
    nst = SSD_STATE
    in_ctx = k < n_ctx_chunks
    kk = k - n_ctx_chunks
    if rev:
        jc, jl = n_ctx_chunks - 1 - k, n_lat_chunks - 1 - kk
    else:
        jc, jl = k, kk
    first = jnp.where(in_ctx, jc == 0, jl == 0)
    last = jnp.where(in_ctx, jc == n_ctx_chunks - 1, jl == n_lat_chunks - 1)

    def conv_silu(pad_ref, main, prev, nxt, cw_ref, cb_ref):
        return _silu(_dwconv_tile(pad_ref, main[...], prev[...], nxt[...], first, last, cw_ref,
                                  cb_ref[...], q))

    xs = conv_silu(padx, xs_ref, xsp_ref, xsn_ref, cwx_ref, cbx_ref)
    bmat = conv_silu(padb, bm_ref, bmp_ref, bmn_ref, cwb_ref, cbb_ref)
    cmat = conv_silu(padc, cm_ref, cmp_ref, cmn_ref, cwc_ref, cbc_ref)
    npair = xs.shape[1] // V7X_LANES
    ngrp = bmat.shape[1] // nst
    pairs_per_group = npair // ngrp
    for p in range(npair):
        xpair[p] = xs[:, p * V7X_LANES:(p + 1) * V7X_LANES].astype(BF16)
    for g in range(ngrp):
        bg = bmat[:, g * nst:(g + 1) * nst]
        cg = cmat[:, g * nst:(g + 1) * nst].astype(BF16)
        cs[g] = cg
        cbs[g] = lax.dot_general(cg, bg.astype(BF16), (((1,), (1,)), ((), ())),
                                 preferred_element_type=F32)
        bts[g] = bg.T
    dtr = dt_ref[...] + dtb_ref[...]
    dt = jnp.maximum(dtr, 0.0) + jnp.log(1.0 + jnp.exp(-jnp.abs(dtr)))
    adt = dt * a_ref[...]
    row = lax.broadcasted_iota(jnp.int32, (q, q), 0)
    col = lax.broadcasted_iota(jnp.int32, (q, q), 1)
    causal = (row <= col) if rev else (row >= col)
    acs = jnp.dot(causal.astype(F32), adt, preferred_element_type=F32, precision=HIGHEST)
    acs_t = acs.T
    tot = acs_t[:, 0:1] if rev else acs_t[:, q - 1:q]
    acst[...] = acs_t
    dtt[...] = dt.T
    wtt[...] = jnp.exp(tot - acs_t) * dt.T
    tott[...] = jnp.broadcast_to(jnp.exp(tot), (tott.shape[0], q))

    @pl.when(k == 0)
    def _():
        state[...] = jnp.zeros_like(state)

    hoff = SSD_HEAD_DIM if rev else 0
    lane = lax.broadcasted_iota(jnp.int32, (q, V7X_LANES), 1)
    left = lane < SSD_HEAD_DIM

    def pair_body(p, c):
        g = p // pairs_per_group
        h0 = hoff + 2 * p
        x = xpair[p]
        cb = cbs[g]
        st = state[p]
        ys, news, scales, decs = [], [], [], []
        for m in range(2):
            arow = acst[pl.ds(h0 + m, 1), :]
            acol = jnp.broadcast_to(arow, (q, q)).T
            seg = jnp.where(causal, jnp.exp(acol - arow), 0.0) * dtt[pl.ds(h0 + m, 1), :]
            mm = (cb * seg).astype(BF16)
            ys.append(jnp.dot(mm, x, preferred_element_type=F32))
            scales.append(jnp.exp(acol))
            wb = (bts[g] * wtt[pl.ds(h0 + m, 1), :]).astype(BF16)
            news.append(jnp.dot(wb, x, preferred_element_type=F32))
            decs.append(tott[pl.ds(h0 + m, 1), :])
        inter = jnp.dot(cs[g], st.astype(BF16), preferred_element_type=F32)
        y = jnp.where(left, ys[0], ys[1]) + inter * jnp.where(left, scales[0], scales[1])
        ypair[p] = y
        state[p] = st * jnp.where(left, decs[0], decs[1]) + jnp.where(left, news[0], news[1])
        return c

    lax.fori_loop(0, npair, pair_body, 0)
    for p in range(npair):
        sl = slice(p * V7X_LANES, (p + 1) * V7X_LANES)
        if rev:
            o_ref[:, sl] = ypair[p]
        else:
            o_ref[:, sl] = ypair[p] + dsk_ref[:, sl] * xs[:, sl]


def _ssd_scan(hz, conv_w, conv_b, dt_bias, a_neg, d_skip_lanes, *, rev, nb, n_lat, n_ctx, inner):
    t = hz.shape[0]
    q = SSD_CHUNK
    bc = SSD_GROUPS * SSD_STATE
    ncc, ncl = n_ctx // q, n_lat // q
    hb = q // HALO
    n_halo_blocks = t // HALO
    nheads2 = dt_bias.size
    assert nheads2 == V7X_LANES and q == V7X_LANES and SSD_STATE == V7X_LANES
    npair = inner // V7X_LANES
    xs_col = inner // inner
    b_col = 2 * inner // bc
    c_col = b_col + 1
    dt_col = (2 * inner + 2 * bc) // nheads2

    def chunk(b, k):
        kk = k - ncc
        if rev:
            jc, jl = ncc - 1 - k, ncl - 1 - kk
        else:
            jc, jl = k, kk
        return jnp.where(k < ncc, nb * ncl + b * ncc + jc, b * ncl + jl)

    def main(width, colb):
        return pl.BlockSpec((q, width), lambda b, k: (chunk(b, k), colb))

    def prev(width, colb):
        return pl.BlockSpec((HALO, width), lambda b, k: (jnp.maximum(chunk(b, k) * hb - 1, 0), colb))

    def nxt(width, colb):
        return pl.BlockSpec((HALO, width),
                            lambda b, k: (jnp.minimum(chunk(b, k) * hb + hb, n_halo_blocks - 1), colb))

    const = lambda shape: pl.BlockSpec(shape, lambda b, k: (0, 0))
    cw = jnp.pad(conv_w, ((0, 8 - CONV_W), (0, 0)))
    cb = conv_b.reshape(1, -1)
    return pl.pallas_call(
        functools.partial(_ssd_kernel, rev=rev, n_ctx_chunks=ncc, n_lat_chunks=ncl),
        grid=(nb, ncc + ncl),
        in_specs=[main(inner, xs_col), prev(inner, xs_col), nxt(inner, xs_col),
                  main(bc, b_col), prev(bc, b_col), nxt(bc, b_col),
                  main(bc, c_col), prev(bc, c_col), nxt(bc, c_col),
                  main(nheads2, dt_col),
                  const((8, inner)), const((1, inner)),
                  const((8, bc)), const((1, bc)),
                  const((8, bc)), const((1, bc)),
                  const((1, nheads2)), const((1, nheads2)), const((1, inner))],
        out_specs=pl.BlockSpec((q, inner), lambda b, k: (chunk(b, k), 0)),
        out_shape=jax.ShapeDtypeStruct((t, inner), F32),
        scratch_shapes=[pltpu.VMEM((q + 2 * HALO, inner), F32),
                        pltpu.VMEM((q + 2 * HALO, bc), F32),
                        pltpu.VMEM((q + 2 * HALO, bc), F32),
                        pltpu.VMEM((npair, q, V7X_LANES), BF16),
                        pltpu.VMEM((npair, q, V7X_LANES), F32),
                        pltpu.VMEM((SSD_GROUPS, q, q), F32),
                        pltpu.VMEM((SSD_GROUPS, SSD_STATE, q), F32),
                        pltpu.VMEM((SSD_GROUPS, q, SSD_STATE), BF16),
                        pltpu.VMEM((nheads2, q), F32),
                        pltpu.VMEM((nheads2, q), F32),
                        pltpu.VMEM((nheads2, q), F32),
                        pltpu.VMEM((nheads2, q), F32),
                        pltpu.VMEM((npair, SSD_STATE, V7X_LANES), F32)],
        compiler_params=_params(("arbitrary", "arbitrary")),
        name="ssd_scan_bwd" if rev else "ssd_scan_fwd",
    )(hz, hz, hz, hz, hz, hz, hz, hz, hz, hz,
      cw[:, :inner], cb[:, :inner], cw[:, inner:inner + bc], cb[:, inner:inner + bc],
      cw[:, inner + bc:], cb[:, inner + bc:], dt_bias.reshape(1, nheads2), a_neg.reshape(1, nheads2),
      d_skip_lanes)


def _out1_kernel(yf_ref, yb_ref, z_ref, x_ref, m_ref, ng_ref, w_ref, o_ref, yn_ref):
    @pl.when(pl.program_id(1) == 0)
    def _():
        y = (yf_ref[...] + yb_ref[...]) * _silu(z_ref[...])
        ms = jnp.mean(y * y, axis=-1, keepdims=True)
        yn_ref[...] = (y * lax.rsqrt(ms + RMS_EPS) * ng_ref[...]).astype(BF16)

    o_ref[...] = x_ref[...] + m_ref[2:3, :] * jnp.dot(yn_ref[...], w_ref[...], preferred_element_type=F32)


def _out1(yf, yb, hz, xl, slab, norm_g, w_out, *, tiles_per_batch):
    t, d = xl.shape
    inner = yf.shape[1]
    tm = TOKEN_TILE // 2
    tn = d // 2
    tpb = tiles_per_batch * (TOKEN_TILE // tm)
    last_row = slab.shape[0] - 1
    return pl.pallas_call(
        _out1_kernel,
        grid=(t // tm, d // tn),
        in_specs=[pl.BlockSpec((tm, inner), lambda i, j: (i, 0)),
                  pl.BlockSpec((tm, inner), lambda i, j: (i, 0)),
                  pl.BlockSpec((tm, inner), lambda i, j: (i, 0)),
                  pl.BlockSpec((tm, tn), lambda i, j: (i, j)),
                  pl.BlockSpec((None, 8, tn), lambda i, j: (jnp.minimum(i // tpb, last_row), 0, j)),
                  pl.BlockSpec((1, inner), lambda i, j: (0, 0)),
                  pl.BlockSpec((inner, tn), lambda i, j: (0, j))],
        out_specs=pl.BlockSpec((tm, tn), lambda i, j: (i, j)),
        out_shape=jax.ShapeDtypeStruct((t, d), F32),
        scratch_shapes=[pltpu.VMEM((tm, inner), BF16)],
        compiler_params=_params(("arbitrary", "arbitrary")),
        name="ssd_out",
    )(yf, yb, hz, xl, slab, norm_g.reshape(1, inner), w_out.astype(BF16))


def _final_norm_kernel(x_ref, g_ref, o_ref):
    x = x_ref[...]
    ms = jnp.mean(x * x, axis=-1, keepdims=True)
    o_ref[...] = x * lax.rsqrt(ms + RMS_EPS) * g_ref[...]


def _final_norm(xl, g):
    t, d = xl.shape
    tm = TOKEN_TILE
    return pl.pallas_call(
        _final_norm_kernel,
        grid=(t // tm,),
        in_specs=[pl.BlockSpec((tm, d), lambda i: (i, 0)), pl.BlockSpec((1, d), lambda i: (0, 0))],
        out_specs=pl.BlockSpec((tm, d), lambda i: (i, 0)),
        out_shape=jax.ShapeDtypeStruct((t, d), F32),
        compiler_params=_params(("arbitrary",)),
        name="final_norm",
    )(xl, g.reshape(1, d))


def kernel(x, c, ctx, c_ctx, norm_mix_g, norm_ffn_g, mod_w, mod_b, hy_w_in, hy_w_out, s5_lam_re, s5_lam_im, s5_log_dt, s5_b_re, s5_b_im, s5_c_re, s5_c_im, s5_d, s5_glu_w, s5_glu_b, lru_conv_w, lru_conv_b, lru_wa, lru_ba, lru_wx, lru_bx, lru_lam, ssd_w_in, ssd_conv_w, ssd_conv_b, ssd_dt_bias, ssd_a_log, ssd_d, ssd_norm_g, ssd_w_out, moe_wg, moe_bg, moe_we, moe_be, moe_w1, moe_w3, moe_w2, final_norm_g):
    nb, n_lat, d = x.shape
    n_ctx = ctx.shape[1]
    depth = mod_w.shape[0]
    assert depth == 2 and nb + 1 <= 8
    assert n_lat % TOKEN_TILE == 0 and (nb * n_ctx) % TOKEN_TILE == 0
    t_lat = nb * n_lat
    rows = n_lat // GRID_W
    tiles_per_batch = n_lat // TOKEN_TILE
    n_cond = nb + 1

    xa = jnp.concatenate([x.reshape(t_lat, d), ctx.reshape(nb * n_ctx, d)], axis=0)
    cvec = jnp.zeros((8, d), F32).at[:nb].set(c).at[nb].set(c_ctx)
    mods = _mod_vectors(cvec, mod_w, mod_b)
    slab0 = _mod_slab(mods[0], n_cond, d)
    slab1 = _mod_slab(mods[1], n_cond, d)

    width = s5_d.shape[1]
    hz = _norm_matmul(xa, norm_mix_g[0], slab0, hy_w_in[0].astype(BF16), shift_row=0, scale_row=1,
                      tn=width, tiles_per_batch=tiles_per_batch)
    s5w = _s5_weights(s5_lam_re[0], s5_lam_im[0], s5_log_dt[0], s5_b_re[0], s5_b_im[0],
                      s5_c_re[0], s5_c_im[0])
    ys = _s5_mixer(hz[:, :width], s5w, nb=nb, n_lat=n_lat, n_ctx=n_ctx)
    sp = jax.nn.softplus(-lru_lam[0].astype(F32)).reshape(2, 1, width)
    hfb = _lru_scan(hz, 2, lru_conv_w[0], lru_conv_b[0], _block_diag_gates(lru_wa[0]),
                    _block_diag_gates(lru_wx[0]), lru_ba[0].reshape(2, 1, width),
                    lru_bx[0].reshape(2, 1, width), sp, nb=nb, n_lat=n_lat, n_ctx=n_ctx)
    xa = _out0(ys, hz, hfb, xa, slab0, s5_d[0], s5_glu_w[0], s5_glu_b[0], hy_w_out[0],
               tiles_per_batch=tiles_per_batch)
    xa = _moe_layer(xa, norm_ffn_g[0], slab0, moe_wg[0], moe_bg[0], moe_we[0], moe_be[0],
                    moe_w1[0], moe_w3[0], moe_w2[0], tiles_per_batch=tiles_per_batch)

    lat = xa[:t_lat].reshape(nb, rows, GRID_W, d).transpose(0, 2, 1, 3).reshape(t_lat, d)
    xc = jnp.concatenate([lat, xa[t_lat:]], axis=0)
    inner = ssd_norm_g.shape[1]
    hz1 = _norm_matmul(xc, norm_mix_g[1], slab1, ssd_w_in[0].astype(BF16), shift_row=0, scale_row=1,
                       tn=ssd_w_in.shape[2] // 9, tiles_per_batch=tiles_per_batch)
    a_neg = -jnp.exp(ssd_a_log[0].astype(F32))
    dsk = jnp.repeat(ssd_d[0], SSD_HEAD_DIM).reshape(1, inner)
    ssd_args = dict(nb=nb, n_lat=n_lat, n_ctx=n_ctx, inner=inner)
    yf = _ssd_scan(hz1, ssd_conv_w[0], ssd_conv_b[0], ssd_dt_bias[0], a_neg, dsk, rev=False, **ssd_args)
    yb = _ssd_scan(hz1, ssd_conv_w[0], ssd_conv_b[0], ssd_dt_bias[0], a_neg, dsk, rev=True, **ssd_args)
    xl = _out1(yf, yb, hz1, lat, slab1, ssd_norm_g[0], ssd_w_out[0], tiles_per_batch=tiles_per_batch)
    xl = _moe_layer(xl, norm_ffn_g[1], slab1, moe_wg[1], moe_bg[1], moe_we[1], moe_be[1],
                    moe_w1[1], moe_w3[1], moe_w2[1], tiles_per_batch=tiles_per_batch)
    out = _final_norm(xl, final_norm_g)
    return out.reshape(nb, GRID_W, rows, d).transpose(0, 2, 1, 3).reshape(nb, n_lat, d)
```

```python
import functools
import math

import jax
import jax.numpy as jnp
from jax import lax
from jax.experimental import pallas as pl
from jax.experimental.pallas import tpu as pltpu

F32 = jnp.float32
BF16 = jnp.bfloat16
HIGHEST = lax.Precision.HIGHEST

GRID_W = 64
N_MOD = 6
RMS_EPS = 1e-6
CONV_W = 4
CONV_PAD_LEFT = CONV_W // 2
S5_GROUP = 16
S5_STATE = 64
LRU_HEADS = 16
LRU_C = 8.0
SSD_HEAD_DIM = 64
SSD_GROUPS = 8
SSD_STATE = 128
SSD_CHUNK = 128
MOE_GROUPS = 4
MOE_PER_GROUP = 8
MOE_EXPERTS = MOE_GROUPS * MOE_PER_GROUP
MOE_TOPK = 2

V7X_LANES = 128
V7X_SUBLANES = 8
V7X_MXU_DIM = 256
V7X_VMEM_LIMIT_BYTES = 60000 * 1024

TOKEN_TILE = 512
S5_CHUNK = V7X_MXU_DIM // S5_GROUP
S5_PAIR_LANES = 2 * S5_STATE
S5_SCAN_LANES = 512
LRU_TILE = 256
LRU_GATE_BLOCK = V7X_MXU_DIM
HALO = V7X_SUBLANES
MOE_BLOCK = 256
ROUTE_LANES = V7X_LANES


def _params(sem):
    return pltpu.CompilerParams(dimension_semantics=sem, vmem_limit_bytes=V7X_VMEM_LIMIT_BYTES)


def _silu(v):
    return v * jax.nn.sigmoid(v)


def _gelu(v):
    return jax.nn.gelu(v, approximate=True)


def _mod_kernel(c_ref, w_ref, b_ref, o_ref):
    s = _silu(c_ref[...])
    o_ref[...] = jnp.dot(s, w_ref[...], preferred_element_type=F32, precision=HIGHEST) + b_ref[...]


def _mod_vectors(cvec, mod_w, mod_b):
    depth, d, n = mod_w.shape
    tn = n // 8
    return pl.pallas_call(
        _mod_kernel,
        grid=(depth, n // tn),
        in_specs=[pl.BlockSpec((8, d), lambda l, j: (0, 0)),
                  pl.BlockSpec((None, d, tn), lambda l, j: (l, 0, j)),
                  pl.BlockSpec((None, 1, tn), lambda l, j: (l, 0, j))],
        out_specs=pl.BlockSpec((None, 8, tn), lambda l, j: (l, 0, j)),
        out_shape=jax.ShapeDtypeStruct((depth, 8, n), F32),
        compiler_params=_params(("arbitrary", "arbitrary")),
        name="mod_vectors",
    )(cvec, mod_w, mod_b.reshape(depth, 1, n))


def _mod_slab(mods_layer, n_rows, d):
    m = mods_layer[:n_rows].reshape(n_rows, N_MOD, d)
    return jnp.pad(m, ((0, 0), (0, 8 - N_MOD), (0, 0)))


def _norm_mod(x, g, m_ref, shift_row, scale_row):
    ms = jnp.mean(x * x, axis=-1, keepdims=True)
    y = x * lax.rsqrt(ms + RMS_EPS) * g
    return y * (1.0 + m_ref[scale_row:scale_row + 1, :]) + m_ref[shift_row:shift_row + 1, :]


def _norm_mm_kernel(x_ref, g_ref, m_ref, w_ref, o_ref, xn_ref, *, shift_row, scale_row):
    @pl.when(pl.program_id(1) == 0)
    def _():
        xn_ref[...] = _norm_mod(x_ref[...], g_ref[...], m_ref, shift_row, scale_row).astype(BF16)

    o_ref[...] = jnp.dot(xn_ref[...], w_ref[...], preferred_element_type=F32)


def _norm_matmul(xa, g, slab, w, *, shift_row, scale_row, tn, tiles_per_batch):
    t, d = xa.shape
    n = w.shape[1]
    tm = TOKEN_TILE
    last_row = slab.shape[0] - 1
    return pl.pallas_call(
        functools.partial(_norm_mm_kernel, shift_row=shift_row, scale_row=scale_row),
        grid=(t // tm, n // tn),
        in_specs=[pl.BlockSpec((tm, d), lambda i, j: (i, 0)),
                  pl.BlockSpec((1, d), lambda i, j: (0, 0)),
                  pl.BlockSpec((None, 8, d),
                               lambda i, j: (jnp.minimum(i // tiles_per_batch, last_row), 0, 0)),
                  pl.BlockSpec((d, tn), lambda i, j: (0, j))],
        out_specs=pl.BlockSpec((tm, tn), lambda i, j: (i, j)),
        out_shape=jax.ShapeDtypeStruct((t, n), F32),
        scratch_shapes=[pltpu.VMEM((tm, d), BF16)],
        compiler_params=_params(("arbitrary", "arbitrary")),
        name="norm_matmul",
    )(xa, g.reshape(1, d), slab, w)


def _s5_weights(lam_re, lam_im, log_dt, b_re, b_im, c_re, c_im):
    q = S5_CHUNK
    ngrp, nst = lam_re.shape[1], lam_re.shape[2]
    nch = b_re.shape[-1]
    lr, li = lam_re.astype(F32), lam_im.astype(F32)
    dt = jnp.exp(log_dt.astype(F32))[..., None]
    mag = jnp.exp(lr * dt)
    a_re, a_im = mag * jnp.cos(li * dt), mag * jnp.sin(li * dt)
    den = lr * lr + li * li
    k_re = ((a_re - 1) * lr + a_im * li) / den
    k_im = (a_im * lr - (a_re - 1) * li) / den
    bb_re = k_re[..., None] * b_re - k_im[..., None] * b_im
    bb_im = k_re[..., None] * b_im + k_im[..., None] * b_re
    ks = jnp.arange(q + 1, dtype=F32)[:, None, None, None]
    pmag = jnp.exp(ks * (lr * dt))
    pw_re, pw_im = pmag * jnp.cos(ks * (li * dt)), pmag * jnp.sin(ks * (li * dt))
    ab_re = pw_re[..., None] * bb_re - pw_im[..., None] * bb_im
    ab_im = pw_re[..., None] * bb_im + pw_im[..., None] * bb_re
    kmat = (jnp.einsum('dgop,kdgpi->dgkio', c_re, ab_re, precision=HIGHEST)
            - jnp.einsum('dgop,kdgpi->dgkio', c_im, ab_im, precision=HIGHEST))
    s_idx = jnp.arange(q)[:, None]
    t_idx = jnp.arange(q)[None, :]
    lag_f = t_idx - s_idx
    lag_b = s_idx - t_idx
    kf = jnp.where((lag_f >= 0)[None, :, :, None, None], kmat[0][:, jnp.clip(lag_f, 0, q - 1)], 0.0)
    kb = jnp.where((lag_b >= 0)[None, :, :, None, None], kmat[1][:, jnp.clip(lag_b, 0, q - 1)], 0.0)
    toep = (kf + kb).transpose(0, 1, 3, 2, 4).reshape(ngrp, q * nch, q * nch)
    sf_re = ab_re[q - 1 - jnp.arange(q), 0]
    sf_im = ab_im[q - 1 - jnp.arange(q), 0]
    sb_re = ab_re[jnp.arange(q), 1]
    sb_im = ab_im[jnp.arange(q), 1]
    summ = jnp.stack([sf_re, sf_im, sb_re, sb_im], 0)
    summ = summ.transpose(2, 1, 4, 0, 3).reshape(ngrp, q * nch, 4, nst)
    npair = ngrp // 2
    summ = summ.reshape(npair, 2, q * nch, 4, nst)
    zero = jnp.zeros_like(summ[:, 0])
    wa = jnp.concatenate([jnp.concatenate([summ[:, 0], zero], -1),
                          jnp.concatenate([zero, summ[:, 1]], -1)], 1)
    wa = wa.reshape(npair, 2 * q * nch, 4 * 2 * nst)
    cp_re = c_re[None] * pw_re[:, :, :, None, :] - c_im[None] * pw_im[:, :, :, None, :]
    cp_im = c_re[None] * pw_im[:, :, :, None, :] + c_im[None] * pw_re[:, :, :, None, :]
    tf = jnp.arange(q) + 1
    tb = q - jnp.arange(q)
    cr = jnp.stack([cp_re[tf, 0], -cp_im[tf, 0], cp_re[tb, 1], -cp_im[tb, 1]], 0)
    cr = cr.transpose(2, 0, 4, 1, 3).reshape(ngrp, 4, nst, q * nch)
    cr = cr.reshape(npair, 2, 4, nst, q * nch)
    zc = jnp.zeros_like(cr[:, 0])
    wc0 = jnp.concatenate([cr[:, 0], zc], 2)
    wc1 = jnp.concatenate([zc, cr[:, 1]], 2)
    wc = jnp.stack([wc0, wc1], 1).reshape(ngrp, 4 * 2 * nst, q * nch)
    a16 = jnp.stack([pw_re[q, 0], pw_im[q, 0], pw_re[q, 1], pw_im[q, 1]], 0)
    a16 = jnp.pad(a16.reshape(4, ngrp * nst), ((0, 4), (0, 0)))
    return toep.astype(BF16), wa.astype(BF16), wc.astype(BF16), a16


def _s5_row_chunk(nc):
    for r in (128, 96, 64, 48, 32, 16):
        if nc % r == 0:
            return r
    raise ValueError(f"unsupported chunk count {nc}")


def _s5_summary_kernel(u_ref, w_ref, ug_ref, o_ref, *, nc):
    q, nch = S5_CHUNK, S5_GROUP
    ngl = u_ref.shape[1] // nch
    rc = _s5_row_chunk(nc)

    def regroup(k, c):
        r0 = pl.multiple_of(k * rc, rc)
        steps = [u_ref[pl.ds(r0 * q + s, rc, stride=q), :] for s in range(q)]
        for gl in range(ngl):
            ug_ref[gl, pl.ds(r0, rc), :] = jnp.concatenate(
                [p[:, gl * nch:(gl + 1) * nch] for p in steps], axis=1).astype(BF16)
        return c

    lax.fori_loop(0, nc // rc, regroup, 0)
    for pr in range(ngl // 2):
        x = jnp.concatenate([ug_ref[2 * pr], ug_ref[2 * pr + 1]], axis=1)
        r = jnp.dot(x, w_ref[pr], preferred_element_type=F32)
        for qd in range(4):
            o_ref[qd, :, pr * S5_PAIR_LANES:(pr + 1) * S5_PAIR_LANES] = (
                r[:, qd * S5_PAIR_LANES:(qd + 1) * S5_PAIR_LANES])


def _s5_scan_kernel(s_ref, a_ref, h_ref, *, nb, n_lat, n_ctx):
    afr, afi = a_ref[0:1, :], a_ref[1:2, :]
    abr, abi = a_ref[2:3, :], a_ref[3:4, :]
    zero = jnp.zeros_like(afr)

    def step(rf, rb, carry):
        hfr, hfi, hbr, hbi = carry
        h_ref[0, pl.ds(rf, 1), :] = hfr
        h_ref[1, pl.ds(rf, 1), :] = hfi
        h_ref[2, pl.ds(rb, 1), :] = hbr
        h_ref[3, pl.ds(rb, 1), :] = hbi
        sfr, sfi = s_ref[0, pl.ds(rf, 1), :], s_ref[1, pl.ds(rf, 1), :]
        sbr, sbi = s_ref[2, pl.ds(rb, 1), :], s_ref[3, pl.ds(rb, 1), :]
        return (afr * hfr - afi * hfi + sfr, afr * hfi + afi * hfr + sfi,
                abr * hbr - abi * hbi + sbr, abr * hbi + abi * hbr + sbi)

    for b in range(nb):
        lat0 = b * n_lat
        ctx0 = nb * n_lat + b * n_ctx
        carry = lax.fori_loop(
            0, n_ctx, lambda i, c: step(ctx0 + i, ctx0 + n_ctx - 1 - i, c), (zero, zero, zero, zero))
        lax.fori_loop(0, n_lat, lambda i, c: step(lat0 + i, lat0 + n_lat - 1 - i, c), carry)


def _s5_output_kernel(ug_ref, t_ref, h_ref, w_ref, o_ref, y_ref, *, nc):
    q, nch = S5_CHUNK, S5_GROUP
    ngl = ug_ref.shape[0]
    rc = _s5_row_chunk(nc)
    for gl in range(ngl):
        pr = gl // 2
        hcat = jnp.concatenate(
            [h_ref[qd, :, pr * S5_PAIR_LANES:(pr + 1) * S5_PAIR_LANES] for qd in range(4)],
            axis=1).astype(BF16)
        y_ref[gl] = (jnp.dot(ug_ref[gl], t_ref[gl], preferred_element_type=F32)
                     + jnp.dot(hcat, w_ref[gl], preferred_element_type=F32))

    def ungroup(k, c):
        r0 = pl.multiple_of(k * rc, rc)
        ys = [y_ref[gl, pl.ds(r0, rc), :] for gl in range(ngl)]
        for s in range(q):
            o_ref[pl.ds(r0 * q + s, rc, stride=q), :] = jnp.concatenate(
                [y[:, s * nch:(s + 1) * nch] for y in ys], axis=1)
        return c

    lax.fori_loop(0, nc // rc, ungroup, 0)


def _s5_mixer(hz, weights, *, width, nb, n_lat, n_ctx):
    toep, wa, wc, a16 = weights
    t = hz.shape[0]
    q, nch = S5_CHUNK, S5_GROUP
    ngrp = width // nch
    ngl = V7X_LANES // nch
    nstrip = ngrp // ngl
    nc = t // q
    cw = q * nch
    lanes = ngrp * S5_STATE
    slanes = ngl * S5_STATE
    ug, summ = pl.pallas_call(
        functools.partial(_s5_summary_kernel, nc=nc),
        grid=(nstrip,),
        in_specs=[pl.BlockSpec((t, V7X_LANES), lambda j: (0, j)),
                  pl.BlockSpec((ngl // 2, 2 * cw, 4 * S5_PAIR_LANES), lambda j: (j, 0, 0))],
        out_specs=[pl.BlockSpec((ngl, nc, cw), lambda j: (j, 0, 0)),
                   pl.BlockSpec((4, nc, slanes), lambda j: (0, 0, j))],
        out_shape=[jax.ShapeDtypeStruct((ngrp, nc, cw), BF16),
                   jax.ShapeDtypeStruct((4, nc, lanes), F32)],
        compiler_params=_params(("arbitrary",)),
        name="s5_summary",
    )(hz, wa)
    wl = S5_SCAN_LANES
    carry = pl.pallas_call(
        functools.partial(_s5_scan_kernel, nb=nb, n_lat=n_lat // q, n_ctx=n_ctx // q),
        grid=(lanes // wl,),
        in_specs=[pl.BlockSpec((4, nc, wl), lambda j: (0, 0, j)),
                  pl.BlockSpec((8, wl), lambda j: (0, j))],
        out_specs=pl.BlockSpec((4, nc, wl), lambda j: (0, 0, j)),
        out_shape=jax.ShapeDtypeStruct((4, nc, lanes), F32),
        compiler_params=_params(("arbitrary",)),
        name="s5_scan",
    )(summ, a16)
    return pl.pallas_call(
        functools.partial(_s5_output_kernel, nc=nc),
        grid=(nstrip,),
        in_specs=[pl.BlockSpec((ngl, nc, cw), lambda j: (j, 0, 0)),
                  pl.BlockSpec((ngl, cw, cw), lambda j: (j, 0, 0)),
                  pl.BlockSpec((4, nc, slanes), lambda j: (0, 0, j)),
                  pl.BlockSpec((ngl, 4 * S5_PAIR_LANES, cw), lambda j: (j, 0, 0))],
        out_specs=pl.BlockSpec((t, V7X_LANES), lambda j: (0, j)),
        out_shape=jax.ShapeDtypeStruct((t, width), F32),
        scratch_shapes=[pltpu.VMEM((ngl, nc, cw), F32)],
        compiler_params=_params(("arbitrary",)),
        name="s5_output",
    )(ug, toep, carry, wc)


def _dwconv_tile(pad_ref, main, prev, nxt, first, last, cw_ref, cb, rows):
    pad_ref[0:HALO, :] = jnp.where(first, 0.0, prev)
    pad_ref[HALO:HALO + rows, :] = main
    pad_ref[HALO + rows:2 * HALO + rows, :] = jnp.where(last, 0.0, nxt)
    acc = cb
    for k in range(CONV_W):
        off = HALO - CONV_PAD_LEFT + k
        acc = acc + cw_ref[k:k + 1, :] * pad_ref[off:off + rows, :]
    return acc


def _lru_kernel(v_ref, vp_ref, vn_ref, cw_ref, cb_ref, wa_ref, wx_ref, ba_ref, bx_ref, sp_ref,
                o_ref, pad_ref, a_ref, b_ref, h_ref, *, n_lat_tiles):
    d = pl.program_id(0)
    k = pl.program_id(2)
    tc = LRU_TILE
    j = jnp.where(d == 0, k - 1, n_lat_tiles - k)
    first = jnp.logical_or(k == 0, j == 0)
    last = jnp.logical_or(k == 0, j == n_lat_tiles - 1)
    vc = _dwconv_tile(pad_ref, v_ref[...], vp_ref[...], vn_ref[...], first, last, cw_ref,
                      cb_ref[...], tc)
    vcb = vc.astype(BF16)
    nblk = vc.shape[1] // LRU_GATE_BLOCK

    def gate(w_ref, bias):
        parts = [jnp.dot(vcb[:, m * LRU_GATE_BLOCK:(m + 1) * LRU_GATE_BLOCK], w_ref[m],
                         preferred_element_type=F32) for m in range(nblk)]
        return jax.nn.sigmoid(jnp.concatenate(parts, axis=1) + bias)

    r = gate(wa_ref, ba_ref[...])
    i = gate(wx_ref, bx_ref[...])
    a = jnp.exp(-LRU_C * r * sp_ref[...])
    a_ref[...] = a
    b_ref[...] = jnp.sqrt(1.0 - a * a) * (i * vc)

    @pl.when(k == 0)
    def _():
        h_ref[...] = jnp.zeros_like(h_ref)

    def body(t, h):
        tt = jnp.where(d == 0, t, tc - 1 - t)
        h = a_ref[pl.ds(tt, 1), :] * h + b_ref[pl.ds(tt, 1), :]
        o_ref[pl.ds(tt, 1), :] = h
        return h

    h_ref[0:1, :] = lax.fori_loop(0, tc, body, h_ref[0:1, :], unroll=8)


def _lru_scan(hz, col_block, conv_w, conv_b, wa, wx, ba, bx, sp, *, nb, n_lat, n_ctx):
    t = hz.shape[0]
    width = conv_w.shape[1]
    tc = LRU_TILE
    assert n_ctx == tc and n_lat % tc == 0
    nlt = n_lat // tc
    hb = tc // HALO
    n_halo_blocks = t // HALO

    def row_block(d, b, k):
        j = jnp.where(d == 0, k - 1, nlt - k)
        return jnp.where(k == 0, nb * nlt + b, b * nlt + j)

    cw = jnp.pad(conv_w, ((0, 8 - CONV_W), (0, 0)))
    nblk = width // LRU_GATE_BLOCK
    vec = lambda: pl.BlockSpec((None, 1, width), lambda d, b, k: (d, 0, 0))
    return pl.pallas_call(
        functools.partial(_lru_kernel, n_lat_tiles=nlt),
        grid=(2, nb, nlt + 1),
        in_specs=[pl.BlockSpec((tc, width), lambda d, b, k: (row_block(d, b, k), col_block)),
                  pl.BlockSpec((HALO, width),
                               lambda d, b, k: (jnp.maximum(row_block(d, b, k) * hb - 1, 0), col_block)),
                  pl.BlockSpec((HALO, width),
                               lambda d, b, k: (jnp.minimum(row_block(d, b, k) * hb + hb,
                                                            n_halo_blocks - 1), col_block)),
                  pl.BlockSpec((8, width), lambda d, b, k: (0, 0)),
                  pl.BlockSpec((1, width), lambda d, b, k: (0, 0)),
                  pl.BlockSpec((None, nblk, LRU_GATE_BLOCK, LRU_GATE_BLOCK), lambda d, b, k: (d, 0, 0, 0)),
                  pl.BlockSpec((None, nblk, LRU_GATE_BLOCK, LRU_GATE_BLOCK), lambda d, b, k: (d, 0, 0, 0)),
                  vec(), vec(), vec()],
        out_specs=pl.BlockSpec((None, tc, width), lambda d, b, k: (d, row_block(d, b, k), 0)),
        out_shape=jax.ShapeDtypeStruct((2, t, width), F32),
        scratch_shapes=[pltpu.VMEM((tc + 2 * HALO, width), F32),
                        pltpu.VMEM((tc, width), F32),
                        pltpu.VMEM((tc, width), F32),
                        pltpu.VMEM((8, width), F32)],
        compiler_params=_params(("arbitrary", "arbitrary", "arbitrary")),
        name="rglru_scan",
    )(hz, hz, hz, cw, conv_b.reshape(1, width), wa, wx, ba, bx, sp)


def _block_diag_gates(w):
    ndir, heads, hd, _ = w.shape
    per = LRU_GATE_BLOCK // hd
    w = w.reshape(ndir, heads // per, per, hd, hd)
    eye = jnp.eye(per, dtype=w.dtype)
    full = jnp.einsum('dmhij,hk->dmhikj', w, eye)
    return full.reshape(ndir, heads // per, per * hd, per * hd).astype(BF16)


def _out0_kernel(ys_ref, u_ref, g_ref, hf_ref, hb_ref, x_ref, m_ref, d_ref, gw_ref, gb_ref,
                 wt_ref, wb_ref, o_ref):
    y = ys_ref[...] + d_ref[...] * u_ref[...]
    z = _gelu(y)
    gate = jax.nn.sigmoid(jnp.dot(z.astype(BF16), gw_ref[...], preferred_element_type=F32) + gb_ref[...])
    a = z * gate
    r = _gelu(g_ref[...]) * (hf_ref[...] + hb_ref[...])
    dx = (jnp.dot(a.astype(BF16), wt_ref[...], preferred_element_type=F32)
          + jnp.dot(r.astype(BF16), wb_ref[...], preferred_element_type=F32))
    o_ref[...] = x_ref[...] + m_ref[2:3, :] * dx


def _out0(ys, hz, hfb, xa, slab, d_skip, glu_w, glu_b, w_out, *, tiles_per_batch):
    t, d = xa.shape
    width = ys.shape[1]
    tm = TOKEN_TILE // 2
    last_row = slab.shape[0] - 1
    tpb = tiles_per_batch * (TOKEN_TILE // tm)
    const = lambda shape: pl.BlockSpec(shape, lambda i: tuple(0 for _ in shape))
    return pl.pallas_call(
        _out0_kernel,
        grid=(t // tm,),
        in_specs=[pl.BlockSpec((tm, width), lambda i: (i, 0)),
                  pl.BlockSpec((tm, width), lambda i: (i, 0)),
                  pl.BlockSpec((tm, width), lambda i: (i, 1)),
                  pl.BlockSpec((None, tm, width), lambda i: (0, i, 0)),
                  pl.BlockSpec((None, tm, width), lambda i: (1, i, 0)),
                  pl.BlockSpec((tm, d), lambda i: (i, 0)),
                  pl.BlockSpec((None, 8, d), lambda i: (jnp.minimum(i // tpb, last_row), 0, 0)),
                  const((1, width)), const((width, width)), const((1, width)),
                  const((width, d)), const((width, d))],
        out_specs=pl.BlockSpec((tm, d), lambda i: (i, 0)),
        out_shape=jax.ShapeDtypeStruct((t, d), F32),
        compiler_params=_params(("arbitrary",)),
        name="hybrid_out",
    )(ys, hz, hz, hfb, hfb, xa, slab, d_skip.reshape(1, width), glu_w.astype(BF16),
      glu_b.reshape(1, width), w_out[:width].astype(BF16), w_out[width:].astype(BF16))


def _store_token_rows(dst_ref, val):
    rows, d = val.shape
    per = d // V7X_LANES
    for c in range(per):
        dst_ref[pl.ds(c, rows, stride=per), :] = val[:, c * V7X_LANES:(c + 1) * V7X_LANES]


def _load_token_rows(src_ref, start, rows, per):
    return [src_ref[pl.ds(start * per + c, rows, stride=per), :] for c in range(per)]


def _route_kernel(x_ref, g_ref, m_ref, w_ref, b_ref, h_ref, r_ref):
    h = _norm_mod(x_ref[...], g_ref[...], m_ref, 3, 4)
    _store_token_rows(h_ref, h)
    logits = jnp.dot(h, w_ref[...], preferred_element_type=F32, precision=HIGHEST) + b_ref[...]
    lane = lax.broadcasted_iota(jnp.int32, logits.shape, 1).astype(F32)
    ninf = -jnp.inf
    big = float(ROUTE_LANES)
    lg = jnp.where(lane < MOE_GROUPS, logits, ninf)
    mg = jnp.max(lg, axis=-1, keepdims=True)
    gidx = jnp.min(jnp.where(lg == mg, lane, big), axis=-1, keepdims=True)
    p_top = 1.0 / jnp.sum(jnp.exp(lg - mg), axis=-1, keepdims=True)
    lo = MOE_GROUPS + gidx * MOE_PER_GROUP
    le = jnp.where(jnp.logical_and(lane >= lo, lane < lo + MOE_PER_GROUP), logits, ninf)
    v1 = jnp.max(le, axis=-1, keepdims=True)
    i1 = jnp.min(jnp.where(le == v1, lane, big), axis=-1, keepdims=True)
    le2 = jnp.where(lane == i1, ninf, le)
    v2 = jnp.max(le2, axis=-1, keepdims=True)
    i2 = jnp.min(jnp.where(le2 == v2, lane, big), axis=-1, keepdims=True)
    tt = jnp.exp(v2 - v1)
    g1 = p_top / (1.0 + tt)
    g2 = p_top * tt / (1.0 + tt)
    out = jnp.where(lane == 0, i1 - MOE_GROUPS, 0.0)
    out = jnp.where(lane == 1, i2 - MOE_GROUPS, out)
    out = jnp.where(lane == 2, g1, out)
    out = jnp.where(lane == 3, g2, out)
    r_ref[...] = out


def _route(xa, g, slab, wg, bg, we, be, *, tiles_per_batch):
    t, d = xa.shape
    tm = TOKEN_TILE
    per = d // V7X_LANES
    nl = MOE_GROUPS + MOE_EXPERTS
    wr = jnp.pad(jnp.concatenate([wg, we], axis=1), ((0, 0), (0, ROUTE_LANES - nl)))
    br = jnp.pad(jnp.concatenate([bg, be], axis=0), (0, ROUTE_LANES - nl)).reshape(1, ROUTE_LANES)
    last_row = slab.shape[0] - 1
    return pl.pallas_call(
        _route_kernel,
        grid=(t // tm,),
        in_specs=[pl.BlockSpec((tm, d), lambda i: (i, 0)),
                  pl.BlockSpec((1, d), lambda i: (0, 0)),
                  pl.BlockSpec((None, 8, d), lambda i: (jnp.minimum(i // tiles_per_batch, last_row), 0, 0)),
                  pl.BlockSpec((d, ROUTE_LANES), lambda i: (0, 0)),
                  pl.BlockSpec((1, ROUTE_LANES), lambda i: (0, 0))],
        out_specs=[pl.BlockSpec((tm * per, V7X_LANES), lambda i: (i, 0)),
                   pl.BlockSpec((tm, ROUTE_LANES), lambda i: (i, 0))],
        out_shape=[jax.ShapeDtypeStruct((t * per, V7X_LANES), F32),
                   jax.ShapeDtypeStruct((t, ROUTE_LANES), F32)],
        compiler_params=_params(("arbitrary",)),
        name="moe_route",
    )(xa, g.reshape(1, d), slab, wr, br)


def _dispatch(experts):
    n_pairs = experts.size
    bm = MOE_BLOCK
    flat_e = experts.reshape(-1)
    onehot = (flat_e[:, None] == jnp.arange(MOE_EXPERTS, dtype=jnp.int32)[None, :]).astype(jnp.int32)
    csum = jnp.cumsum(onehot, axis=0)
    rank = jnp.take_along_axis(csum, flat_e[:, None], axis=1)[:, 0] - 1
    counts = csum[-1]
    padded = (counts + bm - 1) // bm * bm
    pad_end = jnp.cumsum(padded)
    dest = (pad_end - padded)[flat_e] + rank
    n_blocks = -(-n_pairs // bm) + MOE_EXPERTS
    slot_tok = jnp.zeros((n_blocks * bm,), jnp.int32).at[dest].set(
        jnp.arange(n_pairs, dtype=jnp.int32) // MOE_TOPK)
    block_e = jnp.minimum(
        jnp.searchsorted(pad_end, jnp.arange(n_blocks, dtype=jnp.int32) * bm, side='right'),
        MOE_EXPERTS - 1).astype(jnp.int32)
    n_used = (pad_end[-1] // bm).astype(jnp.int32).reshape(1)
    return dest.astype(jnp.int32), slot_tok, block_e, n_used, n_blocks


def _token_copy(src_hbm, tok, dst, slot, r, sem, per):
    return pltpu.make_async_copy(src_hbm.at[pl.ds(tok * per, per), :],
                                 dst.at[slot, pl.ds(r * per, per), :], sem.at[slot])


def _moe_kernel(be_ref, nu_ref, tok_ref, tokn_ref, h_hbm, w1_ref, w3_ref, w2_ref, o_ref,
                xbuf, sem, w1b, w3b, w2b, prev_ref):
    b = pl.program_id(0)
    bm = MOE_BLOCK
    per = w1_ref.shape[0] // V7X_LANES
    n_used = nu_ref[0]
    slot = lax.rem(b, 2)

    def start_rows(idx_ref, s):
        for r in range(bm):
            _token_copy(h_hbm, idx_ref[0, r], xbuf, s, r, sem, per).start()

    def wait_rows(s):
        for r in range(bm):
            _token_copy(h_hbm, 0, xbuf, s, r, sem, per).wait()

    @pl.when(b == 0)
    def _():
        prev_ref[0] = -1
        start_rows(tok_ref, 0)

    @pl.when(b < n_used)
    def _():
        e = be_ref[b]

        @pl.when(e != prev_ref[0])
        def _():
            w1b[...] = w1_ref[...].astype(BF16)
            w3b[...] = w3_ref[...].astype(BF16)
            w2b[...] = w2_ref[...].astype(BF16)
            prev_ref[0] = e

        start_rows(tokn_ref, 1 - slot)
        wait_rows(slot)
        x = jnp.concatenate(_load_token_rows(xbuf.at[slot], 0, bm, per), axis=1).astype(BF16)
        h1 = jnp.dot(x, w1b[...], preferred_element_type=F32)
        h3 = jnp.dot(x, w3b[...], preferred_element_type=F32)
        act = (_silu(h1) * h3).astype(BF16)
        _store_token_rows(o_ref, jnp.dot(act, w2b[...], preferred_element_type=F32))

    @pl.when(b == n_used)
    def _():
        wait_rows(slot)

    @pl.when(b >= n_used)
    def _():
        o_ref[...] = jnp.zeros_like(o_ref)


def _moe_experts(h_rows, slot_tok, block_e, n_used, n_blocks, w1, w3, w2):
    d, ff = w1.shape[1], w1.shape[2]
    per = d // V7X_LANES
    bm = MOE_BLOCK
    tok3 = slot_tok.reshape(n_blocks, 1, bm)
    smem_block = lambda fn: pl.BlockSpec((None, 1, bm), fn, memory_space=pltpu.SMEM)
    grid_spec = pltpu.PrefetchScalarGridSpec(
        num_scalar_prefetch=2,
        grid=(n_blocks,),
        in_specs=[smem_block(lambda b, be, nu: (b, 0, 0)),
                  smem_block(lambda b, be, nu: (jnp.minimum(b + 1, n_blocks - 1), 0, 0)),
                  pl.BlockSpec(memory_space=pl.ANY),
                  pl.BlockSpec((None, d, ff), lambda b, be, nu: (be[b], 0, 0)),
                  pl.BlockSpec((None, d, ff), lambda b, be, nu: (be[b], 0, 0)),
                  pl.BlockSpec((None, ff, d), lambda b, be, nu: (be[b], 0, 0))],
        out_specs=pl.BlockSpec((bm * per, V7X_LANES), lambda b, be, nu: (b, 0)),
        scratch_shapes=[pltpu.VMEM((2, bm * per, V7X_LANES), F32),
                        pltpu.SemaphoreType.DMA((2,)),
                        pltpu.VMEM((d, ff), BF16),
                        pltpu.VMEM((d, ff), BF16),
                        pltpu.VMEM((ff, d), BF16),
                        pltpu.SMEM((1,), jnp.int32)],
    )
    return pl.pallas_call(
        _moe_kernel,
        grid_spec=grid_spec,
        out_shape=jax.ShapeDtypeStruct((n_blocks * bm * per, V7X_LANES), F32),
        compiler_params=_params(("arbitrary",)),
        name="moe_experts",
    )(block_e, n_used, tok3, tok3, h_rows, w1, w3, w2)


def _combine_kernel(d_ref, dn_ref, yb_hbm, x_ref, r_ref, m_ref, o_ref, ybuf, sem, *, n_tiles):
    i = pl.program_id(0)
    tm, d = x_ref.shape
    per = d // V7X_LANES
    nrow = MOE_TOPK * tm
    slot = lax.rem(i, 2)

    def start_rows(idx_ref, s):
        for r in range(nrow):
            _token_copy(yb_hbm, idx_ref[0, r], ybuf, s, r, sem, per).start()

    @pl.when(i == 0)
    def _():
        start_rows(d_ref, 0)

    @pl.when(i + 1 < n_tiles)
    def _():
        start_rows(dn_ref, 1 - slot)

    for r in range(nrow):
        _token_copy(yb_hbm, 0, ybuf, slot, r, sem, per).wait()
    r = r_ref[...]
    g0, g1 = r[:, 2:3], r[:, 3:4]
    y0 = _load_token_rows(ybuf.at[slot], 0, tm, per)
    y1 = _load_token_rows(ybuf.at[slot], tm, tm, per)
    for c in range(per):
        sl = slice(c * V7X_LANES, (c + 1) * V7X_LANES)
        o_ref[:, sl] = x_ref[:, sl] + m_ref[5:6, sl] * (g0 * y0[c] + g1 * y1[c])


def _moe_combine(yb_rows, dest, xa, route, slab, *, tiles_per_batch):
    t, d = xa.shape
    per = d // V7X_LANES
    tm = TOKEN_TILE // 2
    n_tiles = t // tm
    tpb = tiles_per_batch * (TOKEN_TILE // tm)
    last_row = slab.shape[0] - 1
    d3 = dest.reshape(n_tiles, tm, MOE_TOPK).transpose(0, 2, 1).reshape(n_tiles, 1, MOE_TOPK * tm)
    smem_block = lambda fn: pl.BlockSpec((None, 1, MOE_TOPK * tm), fn, memory_space=pltpu.SMEM)
    return pl.pallas_call(
        functools.partial(_combine_kernel, n_tiles=n_tiles),
        grid=(n_tiles,),
        in_specs=[smem_block(lambda i: (i, 0, 0)),
                  smem_block(lambda i: (jnp.minimum(i + 1, n_tiles - 1), 0, 0)),
                  pl.BlockSpec(memory_space=pl.ANY),
                  pl.BlockSpec((tm, d), lambda i: (i, 0)),
                  pl.BlockSpec((tm, ROUTE_LANES), lambda i: (i, 0)),
                  pl.BlockSpec((None, 8, d), lambda i: (jnp.minimum(i // tpb, last_row), 0, 0))],
        out_specs=pl.BlockSpec((tm, d), lambda i: (i, 0)),
        out_shape=jax.ShapeDtypeStruct((t, d), F32),
        scratch_shapes=[pltpu.VMEM((2, MOE_TOPK * tm * per, V7X_LANES), F32),
                        pltpu.SemaphoreType.DMA((2,))],
        compiler_params=_params(("arbitrary",)),
        name="moe_combine",
    )(d3, d3, yb_rows, xa, route, slab)


def _moe_layer(xa, g, slab, wg, bg, we, be, w1, w3, w2, *, tiles_per_batch):
    h, route = _route(xa, g, slab, wg, bg, we, be, tiles_per_batch=tiles_per_batch)
    experts = route[:, 0:MOE_TOPK].astype(jnp.int32)
    dest, slot_tok, block_e, n_used, n_blocks = _dispatch(experts)
    yb = _moe_experts(h, slot_tok, block_e, n_used, n_blocks, w1, w3, w2)
    return _moe_combine(yb, dest, xa, route, slab, tiles_per_batch=tiles_per_batch)


def _ssd_kernel(xs_ref, xsp_ref, xsn_ref, bm_ref, bmp_ref, bmn_ref, cm_ref, cmp_ref, cmn_ref,
                dt_ref, cwx_ref, cbx_ref, cwb_ref, cbb_ref, cwc_ref, cbc_ref, dtb_ref, a_ref, dsk_ref,
                o_ref, padx, padb, padc, xpair, ypair, cbs, bts, cs, acst, dtt, wtt, tott, state,
                *, rev, n_ctx_chunks, n_lat_chunks):
    k = pl.program_id(1)
    q = SSD_CHUNK
    nst = SSD_STATE
    in_ctx = k < n_ctx_chunks
    kk = k - n_ctx_chunks
    if rev:
        jc, jl = n_ctx_chunks - 1 - k, n_lat_chunks - 1 - kk
    else:
        jc, jl = k, kk
    first = jnp.where(in_ctx, jc == 0, jl == 0)
    last = jnp.where(in_ctx, jc == n_ctx_chunks - 1, jl == n_lat_chunks - 1)

    def conv_silu(pad_ref, main, prev, nxt, cw_ref, cb_ref):
        return _silu(_dwconv_tile(pad_ref, main[...], prev[...], nxt[...], first, last, cw_ref,
                                  cb_ref[...], q))

    xs = conv_silu(padx, xs_ref, xsp_ref, xsn_ref, cwx_ref, cbx_ref)
    bmat = conv_silu(padb, bm_ref, bmp_ref, bmn_ref, cwb_ref, cbb_ref)
    cmat = conv_silu(padc, cm_ref, cmp_ref, cmn_ref, cwc_ref, cbc_ref)
    npair = xs.shape[1] // V7X_LANES
    ngrp = bmat.shape[1] // nst
    pairs_per_group = npair // ngrp
    for p in range(npair):
        xpair[p] = xs[:, p * V7X_LANES:(p + 1) * V7X_LANES].astype(BF16)
    for g in range(ngrp):
        bg = bmat[:, g * nst:(g + 1) * nst]
        cg = cmat[:, g * nst:(g + 1) * nst].astype(BF16)
        cs[g] = cg
        cbs[g] = lax.dot_general(cg, bg.astype(BF16), (((1,), (1,)), ((), ())),
                                 preferred_element_type=F32)
        bts[g] = bg.T
    dtr = dt_ref[...] + dtb_ref[...]
    dt = jnp.maximum(dtr, 0.0) + jnp.log(1.0 + jnp.exp(-jnp.abs(dtr)))
    adt = dt * a_ref[...]
    row = lax.broadcasted_iota(jnp.int32, (q, q), 0)
    col = lax.broadcasted_iota(jnp.int32, (q, q), 1)
    causal = (row <= col) if rev else (row >= col)
    acs = jnp.dot(causal.astype(F32), adt, preferred_element_type=F32, precision=HIGHEST)
    acs_t = acs.T
    tot = acs_t[:, 0:1] if rev else acs_t[:, q - 1:q]
    acst[...] = acs_t
    dtt[...] = dt.T
    wtt[...] = jnp.exp(tot - acs_t) * dt.T
    tott[...] = jnp.broadcast_to(jnp.exp(tot), (tott.shape[0], q))

    @pl.when(k == 0)
    def _():
        state[...] = jnp.zeros_like(state)

    hoff = SSD_HEAD_DIM if rev else 0
    lane = lax.broadcasted_iota(jnp.int32, (q, V7X_LANES), 1)
    left = lane < SSD_HEAD_DIM

    def pair_body(p, c):
        g = p // pairs_per_group
        h0 = hoff + 2 * p
        x = xpair[p]
        cb = cbs[g]
        st = state[p]
        ys, news, scales, decs = [], [], [], []
        for m in range(2):
            arow = acst[pl.ds(h0 + m, 1), :]
            acol = jnp.broadcast_to(arow, (q, q)).T
            seg = jnp.where(causal, jnp.exp(acol - arow), 0.0) * dtt[pl.ds(h0 + m, 1), :]
            mm = (cb * seg).astype(BF16)
            ys.append(jnp.dot(mm, x, preferred_element_type=F32))
            scales.append(jnp.exp(acol))
            wb = (bts[g] * wtt[pl.ds(h0 + m, 1), :]).astype(BF16)
            news.append(jnp.dot(wb, x, preferred_element_type=F32))
            decs.append(tott[pl.ds(h0 + m, 1), :])
        inter = jnp.dot(cs[g], st.astype(BF16), preferred_element_type=F32)
        y = jnp.where(left, ys[0], ys[1]) + inter * jnp.where(left, scales[0], scales[1])
        ypair[p] = y
        state[p] = st * jnp.where(left, decs[0], decs[1]) + jnp.where(left, news[0], news[1])
        return c

    lax.fori_loop(0, npair, pair_body, 0, unroll=2)
    for p in range(npair):
        sl = slice(p * V7X_LANES, (p + 1) * V7X_LANES)
        if rev:
            o_ref[:, sl] = ypair[p]
        else:
            o_ref[:, sl] = ypair[p] + dsk_ref[:, sl] * xs[:, sl]


def _ssd_scan(hz, conv_w, conv_b, dt_bias, a_neg, d_skip_lanes, *, rev, nb, n_lat, n_ctx, inner):
    t = hz.shape[0]
    q = SSD_CHUNK
    bc = SSD_GROUPS * SSD_STATE
    ncc, ncl = n_ctx // q, n_lat // q
    hb = q // HALO
    n_halo_blocks = t // HALO
    nheads2 = dt_bias.size
    assert nheads2 == V7X_LANES and q == V7X_LANES and SSD_STATE == V7X_LANES
    npair = inner // V7X_LANES
    xs_col = inner // inner
    b_col = 2 * inner // bc
    c_col = b_col + 1
    dt_col = (2 * inner + 2 * bc) // nheads2

    def chunk(b, k):
        kk = k - ncc
        if rev:
            jc, jl = ncc - 1 - k, ncl - 1 - kk
        else:
            jc, jl = k, kk
        return jnp.where(k < ncc, nb * ncl + b * ncc + jc, b * ncl + jl)

    def main(width, colb):
        return pl.BlockSpec((q, width), lambda b, k: (chunk(b, k), colb))

    def prev(width, colb):
        return pl.BlockSpec((HALO, width), lambda b, k: (jnp.maximum(chunk(b, k) * hb - 1, 0), colb))

    def nxt(width, colb):
        return pl.BlockSpec((HALO, width),
                            lambda b, k: (jnp.minimum(chunk(b, k) * hb + hb, n_halo_blocks - 1), colb))

    const = lambda shape: pl.BlockSpec(shape, lambda b, k: (0, 0))
    cw = jnp.pad(conv_w, ((0, 8 - CONV_W), (0, 0)))
    cb = conv_b.reshape(1, -1)
    return pl.pallas_call(
        functools.partial(_ssd_kernel, rev=rev, n_ctx_chunks=ncc, n_lat_chunks=ncl),
        grid=(nb, ncc + ncl),
        in_specs=[main(inner, xs_col), prev(inner, xs_col), nxt(inner, xs_col),
                  main(bc, b_col), prev(bc, b_col), nxt(bc, b_col),
                  main(bc, c_col), prev(bc, c_col), nxt(bc, c_col),
                  main(nheads2, dt_col),
                  const((8, inner)), const((1, inner)),
                  const((8, bc)), const((1, bc)),
                  const((8, bc)), const((1, bc)),
                  const((1, nheads2)), const((1, nheads2)), const((1, inner))],
        out_specs=pl.BlockSpec((q, inner), lambda b, k: (chunk(b, k), 0)),
        out_shape=jax.ShapeDtypeStruct((t, inner), F32),
        scratch_shapes=[pltpu.VMEM((q + 2 * HALO, inner), F32),
                        pltpu.VMEM((q + 2 * HALO, bc), F32),
                        pltpu.VMEM((q + 2 * HALO, bc), F32),
                        pltpu.VMEM((npair, q, V7X_LANES), BF16),
                        pltpu.VMEM((npair, q, V7X_LANES), F32),
                        pltpu.VMEM((SSD_GROUPS, q, q), F32),
                        pltpu.VMEM((SSD_GROUPS, SSD_STATE, q), F32),
                        pltpu.VMEM((SSD_GROUPS, q, SSD_STATE), BF16),
                        pltpu.VMEM((nheads2, q), F32),
                        pltpu.VMEM((nheads2, q), F32),
                        pltpu.VMEM((nheads2, q), F32),
                        pltpu.VMEM((nheads2, q), F32),
                        pltpu.VMEM((npair, SSD_STATE, V7X_LANES), F32)],
        compiler_params=_params(("arbitrary", "arbitrary")),
        name="ssd_scan_bwd" if rev else "ssd_scan_fwd",
    )(hz, hz, hz, hz, hz, hz, hz, hz, hz, hz,
      cw[:, :inner], cb[:, :inner], cw[:, inner:inner + bc], cb[:, inner:inner + bc],
      cw[:, inner + bc:], cb[:, inner + bc:], dt_bias.reshape(1, nheads2), a_neg.reshape(1, nheads2),
      d_skip_lanes)


def _out1_kernel(yf_ref, yb_ref, z_ref, x_ref, m_ref, ng_ref, w_ref, o_ref, yn_ref):
    @pl.when(pl.program_id(1) == 0)
    def _():
        y = (yf_ref[...] + yb_ref[...]) * _silu(z_ref[...])
        ms = jnp.mean(y * y, axis=-1, keepdims=True)
        yn_ref[...] = (y * lax.rsqrt(ms + RMS_EPS) * ng_ref[...]).astype(BF16)

    o_ref[...] = x_ref[...] + m_ref[2:3, :] * jnp.dot(yn_ref[...], w_ref[...], preferred_element_type=F32)


def _out1(yf, yb, hz, xl, slab, norm_g, w_out, *, tiles_per_batch):
    t, d = xl.shape
    inner = yf.shape[1]
    tm = TOKEN_TILE // 2
    tn = d // 2
    tpb = tiles_per_batch * (TOKEN_TILE // tm)
    last_row = slab.shape[0] - 1
    return pl.pallas_call(
        _out1_kernel,
        grid=(t // tm, d // tn),
        in_specs=[pl.BlockSpec((tm, inner), lambda i, j: (i, 0)),
                  pl.BlockSpec((tm, inner), lambda i, j: (i, 0)),
                  pl.BlockSpec((tm, inner), lambda i, j: (i, 0)),
                  pl.BlockSpec((tm, tn), lambda i, j: (i, j)),
                  pl.BlockSpec((None, 8, tn), lambda i, j: (jnp.minimum(i // tpb, last_row), 0, j)),
                  pl.BlockSpec((1, inner), lambda i, j: (0, 0)),
                  pl.BlockSpec((inner, tn), lambda i, j: (0, j))],
        out_specs=pl.BlockSpec((tm, tn), lambda i, j: (i, j)),
        out_shape=jax.ShapeDtypeStruct((t, d), F32),
        scratch_shapes=[pltpu.VMEM((tm, inner), BF16)],
        compiler_params=_params(("arbitrary", "arbitrary")),
        name="ssd_out",
    )(yf, yb, hz, xl, slab, norm_g.reshape(1, inner), w_out.astype(BF16))


def _final_norm_kernel(x_ref, g_ref, o_ref):
    x = x_ref[...]
    ms = jnp.mean(x * x, axis=-1, keepdims=True)
    o_ref[...] = x * lax.rsqrt(ms + RMS_EPS) * g_ref[...]


def _final_norm(xl, g):
    t, d = xl.shape
    tm = TOKEN_TILE
    return pl.pallas_call(
        _final_norm_kernel,
        grid=(t // tm,),
        in_specs=[pl.BlockSpec((tm, d), lambda i: (i, 0)), pl.BlockSpec((1, d), lambda i: (0, 0))],
        out_specs=pl.BlockSpec((tm, d), lambda i: (i, 0)),
        out_shape=jax.ShapeDtypeStruct((t, d), F32),
        compiler_params=_params(("arbitrary",)),
        name="final_norm",
    )(xl, g.reshape(1, d))


def kernel(x, c, ctx, c_ctx, norm_mix_g, norm_ffn_g, mod_w, mod_b, hy_w_in, hy_w_out, s5_lam_re, s5_lam_im, s5_log_dt, s5_b_re, s5_b_im, s5_c_re, s5_c_im, s5_d, s5_glu_w, s5_glu_b, lru_conv_w, lru_conv_b, lru_wa, lru_ba, lru_wx, lru_bx, lru_lam, ssd_w_in, ssd_conv_w, ssd_conv_b, ssd_dt_bias, ssd_a_log, ssd_d, ssd_norm_g, ssd_w_out, moe_wg, moe_bg, moe_we, moe_be, moe_w1, moe_w3, moe_w2, final_norm_g):
    nb, n_lat, d = x.shape
    n_ctx = ctx.shape[1]
    depth = mod_w.shape[0]
    assert depth == 2 and nb + 1 <= 8
    assert n_lat % TOKEN_TILE == 0 and (nb * n_ctx) % TOKEN_TILE == 0
    t_lat = nb * n_lat
    rows = n_lat // GRID_W
    tiles_per_batch = n_lat // TOKEN_TILE
    n_cond = nb + 1

    xa = jnp.concatenate([x.reshape(t_lat, d), ctx.reshape(nb * n_ctx, d)], axis=0)
    cvec = jnp.zeros((8, d), F32).at[:nb].set(c).at[nb].set(c_ctx)
    mods = _mod_vectors(cvec, mod_w, mod_b)
    slab0 = _mod_slab(mods[0], n_cond, d)
    slab1 = _mod_slab(mods[1], n_cond, d)

    width = s5_d.shape[1]
    hz = _norm_matmul(xa, norm_mix_g[0], slab0, hy_w_in[0].astype(BF16), shift_row=0, scale_row=1,
                      tn=width, tiles_per_batch=tiles_per_batch)
    s5w = _s5_weights(s5_lam_re[0], s5_lam_im[0], s5_log_dt[0], s5_b_re[0], s5_b_im[0],
                      s5_c_re[0], s5_c_im[0])
    ys = _s5_mixer(hz, s5w, width=width, nb=nb, n_lat=n_lat, n_ctx=n_ctx)
    sp = jax.nn.softplus(-lru_lam[0].astype(F32)).reshape(2, 1, width)
    hfb = _lru_scan(hz, 2, lru_conv_w[0], lru_conv_b[0], _block_diag_gates(lru_wa[0]),
                    _block_diag_gates(lru_wx[0]), lru_ba[0].reshape(2, 1, width),
                    lru_bx[0].reshape(2, 1, width), sp, nb=nb, n_lat=n_lat, n_ctx=n_ctx)
    xa = _out0(ys, hz, hfb, xa, slab0, s5_d[0], s5_glu_w[0], s5_glu_b[0], hy_w_out[0],
               tiles_per_batch=tiles_per_batch)
    xa = _moe_layer(xa, norm_ffn_g[0], slab0, moe_wg[0], moe_bg[0], moe_we[0], moe_be[0],
                    moe_w1[0], moe_w3[0], moe_w2[0], tiles_per_batch=tiles_per_batch)

    lat = xa[:t_lat].reshape(nb, rows, GRID_W, d).transpose(0, 2, 1, 3).reshape(t_lat, d)
    xc = jnp.concatenate([lat, xa[t_lat:]], axis=0)
    inner = ssd_norm_g.shape[1]
    hz1 = _norm_matmul(xc, norm_mix_g[1], slab1, ssd_w_in[0].astype(BF16), shift_row=0, scale_row=1,
                       tn=ssd_w_in.shape[2] // 9, tiles_per_batch=tiles_per_batch)
    a_neg = -jnp.exp(ssd_a_log[0].astype(F32))
    dsk = jnp.repeat(ssd_d[0], SSD_HEAD_DIM).reshape(1, inner)
    ssd_args = dict(nb=nb, n_lat=n_lat, n_ctx=n_ctx, inner=inner)
    yf = _ssd_scan(hz1, ssd_conv_w[0], ssd_conv_b[0], ssd_dt_bias[0], a_neg, dsk, rev=False, **ssd_args)
    yb = _ssd_scan(hz1, ssd_conv_w[0], ssd_conv_b[0], ssd_dt_bias[0], a_neg, dsk, rev=True, **ssd_args)
    xl = _out1(yf, yb, hz1, lat, slab1, ssd_norm_g[0], ssd_w_out[0], tiles_per_batch=tiles_per_batch)
    xl = _moe_layer(xl, norm_ffn_g[1], slab1, moe_wg[1], moe_bg[1], moe_we[1], moe_be[1],
                    moe_w1[1], moe_w3[1], moe_w2[1], tiles_per_batch=tiles_per_batch)
    out = _final_norm(xl, final_norm_g)
    return out.reshape(nb, GRID_W, rows, d).transpose(0, 2, 1, 3).reshape(nb, n_lat, d)
```

```python
import functools
import math

import jax
import jax.numpy as jnp
from jax import lax
from jax.experimental import pallas as pl
from jax.experimental.pallas import tpu as pltpu

F32 = jnp.float32
BF16 = jnp.bfloat16
HIGHEST = lax.Precision.HIGHEST

GRID_W = 64
N_MOD = 6
RMS_EPS = 1e-6
CONV_W = 4
CONV_PAD_LEFT = CONV_W // 2
S5_GROUP = 16
S5_STATE = 64
LRU_HEADS = 16
LRU_C = 8.0
SSD_HEAD_DIM = 64
SSD_GROUPS = 8
SSD_STATE = 128
SSD_CHUNK = 128
MOE_GROUPS = 4
MOE_PER_GROUP = 8
MOE_EXPERTS = MOE_GROUPS * MOE_PER_GROUP
MOE_TOPK = 2

V7X_LANES = 128
V7X_SUBLANES = 8
V7X_MXU_DIM = 256
V7X_VMEM_LIMIT_BYTES = 60000 * 1024

TOKEN_TILE = 512
S5_CHUNK = V7X_MXU_DIM // S5_GROUP
S5_PAIR_LANES = 2 * S5_STATE
S5_SCAN_LANES = 512
LRU_TILE = 256
LRU_GATE_BLOCK = V7X_MXU_DIM
HALO = V7X_SUBLANES
MOE_BLOCK = 256
SSD_PAIR_UNROLL = 4
ROUTE_LANES = V7X_LANES


def _params(sem):
    return pltpu.CompilerParams(dimension_semantics=sem, vmem_limit_bytes=V7X_VMEM_LIMIT_BYTES)


def _silu(v):
    return v * jax.nn.sigmoid(v)


def _gelu(v):
    return jax.nn.gelu(v, approximate=True)


def _mod_kernel(c_ref, w_ref, b_ref, o_ref):
    s = _silu(c_ref[...])
    o_ref[...] = jnp.dot(s, w_ref[...], preferred_element_type=F32, precision=HIGHEST) + b_ref[...]


def _mod_vectors(cvec, mod_w, mod_b):
    depth, d, n = mod_w.shape
    tn = n // 8
    return pl.pallas_call(
        _mod_kernel,
        grid=(depth, n // tn),
        in_specs=[pl.BlockSpec((8, d), lambda l, j: (0, 0)),
                  pl.BlockSpec((None, d, tn), lambda l, j: (l, 0, j)),
                  pl.BlockSpec((None, 1, tn), lambda l, j: (l, 0, j))],
        out_specs=pl.BlockSpec((None, 8, tn), lambda l, j: (l, 0, j)),
        out_shape=jax.ShapeDtypeStruct((depth, 8, n), F32),
        compiler_params=_params(("arbitrary", "arbitrary")),
        name="mod_vectors",
    )(cvec, mod_w, mod_b.reshape(depth, 1, n))


def _mod_slab(mods_layer, n_rows, d):
    m = mods_layer[:n_rows].reshape(n_rows, N_MOD, d)
    return jnp.pad(m, ((0, 0), (0, 8 - N_MOD), (0, 0)))


def _norm_mod(x, g, m_ref, shift_row, scale_row):
    ms = jnp.mean(x * x, axis=-1, keepdims=True)
    y = x * lax.rsqrt(ms + RMS_EPS) * g
    return y * (1.0 + m_ref[scale_row:scale_row + 1, :]) + m_ref[shift_row:shift_row + 1, :]


def _norm_mm_kernel(x_ref, g_ref, m_ref, w_ref, o_ref, xn_ref, *, shift_row, scale_row):
    @pl.when(pl.program_id(1) == 0)
    def _():
        xn_ref[...] = _norm_mod(x_ref[...], g_ref[...], m_ref, shift_row, scale_row).astype(BF16)

    o_ref[...] = jnp.dot(xn_ref[...], w_ref[...], preferred_element_type=F32)


def _norm_matmul(xa, g, slab, w, *, shift_row, scale_row, tn, tiles_per_batch):
    t, d = xa.shape
    n = w.shape[1]
    tm = TOKEN_TILE
    last_row = slab.shape[0] - 1
    return pl.pallas_call(
        functools.partial(_norm_mm_kernel, shift_row=shift_row, scale_row=scale_row),
        grid=(t // tm, n // tn),
        in_specs=[pl.BlockSpec((tm, d), lambda i, j: (i, 0)),
                  pl.BlockSpec((1, d), lambda i, j: (0, 0)),
                  pl.BlockSpec((None, 8, d),
                               lambda i, j: (jnp.minimum(i // tiles_per_batch, last_row), 0, 0)),
                  pl.BlockSpec((d, tn), lambda i, j: (0, j))],
        out_specs=pl.BlockSpec((tm, tn), lambda i, j: (i, j)),
        out_shape=jax.ShapeDtypeStruct((t, n), F32),
        scratch_shapes=[pltpu.VMEM((tm, d), BF16)],
        compiler_params=_params(("arbitrary", "arbitrary")),
        name="norm_matmul",
    )(xa, g.reshape(1, d), slab, w)


def _s5_weights(lam_re, lam_im, log_dt, b_re, b_im, c_re, c_im):
    q = S5_CHUNK
    ngrp, nst = lam_re.shape[1], lam_re.shape[2]
    nch = b_re.shape[-1]
    lr, li = lam_re.astype(F32), lam_im.astype(F32)
    dt = jnp.exp(log_dt.astype(F32))[..., None]
    mag = jnp.exp(lr * dt)
    a_re, a_im = mag * jnp.cos(li * dt), mag * jnp.sin(li * dt)
    den = lr * lr + li * li
    k_re = ((a_re - 1) * lr + a_im * li) / den
    k_im = (a_im * lr - (a_re - 1) * li) / den
    bb_re = k_re[..., None] * b_re - k_im[..., None] * b_im
    bb_im = k_re[..., None] * b_im + k_im[..., None] * b_re
    ks = jnp.arange(q + 1, dtype=F32)[:, None, None, None]
    pmag = jnp.exp(ks * (lr * dt))
    pw_re, pw_im = pmag * jnp.cos(ks * (li * dt)), pmag * jnp.sin(ks * (li * dt))
    ab_re = pw_re[..., None] * bb_re - pw_im[..., None] * bb_im
    ab_im = pw_re[..., None] * bb_im + pw_im[..., None] * bb_re
    kmat = (jnp.einsum('dgop,kdgpi->dgkio', c_re, ab_re, precision=HIGHEST)
            - jnp.einsum('dgop,kdgpi->dgkio', c_im, ab_im, precision=HIGHEST))
    s_idx = jnp.arange(q)[:, None]
    t_idx = jnp.arange(q)[None, :]
    lag_f = t_idx - s_idx
    lag_b = s_idx - t_idx
    kf = jnp.where((lag_f >= 0)[None, :, :, None, None], kmat[0][:, jnp.clip(lag_f, 0, q - 1)], 0.0)
    kb = jnp.where((lag_b >= 0)[None, :, :, None, None], kmat[1][:, jnp.clip(lag_b, 0, q - 1)], 0.0)
    toep = (kf + kb).transpose(0, 1, 3, 2, 4).reshape(ngrp, q * nch, q * nch)
    sf_re = ab_re[q - 1 - jnp.arange(q), 0]
    sf_im = ab_im[q - 1 - jnp.arange(q), 0]
    sb_re = ab_re[jnp.arange(q), 1]
    sb_im = ab_im[jnp.arange(q), 1]
    summ = jnp.stack([sf_re, sf_im, sb_re, sb_im], 0)
    summ = summ.transpose(2, 1, 4, 0, 3).reshape(ngrp, q * nch, 4, nst)
    npair = ngrp // 2
    summ = summ.reshape(npair, 2, q * nch, 4, nst)
    zero = jnp.zeros_like(summ[:, 0])
    wa = jnp.concatenate([jnp.concatenate([summ[:, 0], zero], -1),
                          jnp.concatenate([zero, summ[:, 1]], -1)], 1)
    wa = wa.reshape(npair, 2 * q * nch, 4 * 2 * nst)
    cp_re = c_re[None] * pw_re[:, :, :, None, :] - c_im[None] * pw_im[:, :, :, None, :]
    cp_im = c_re[None] * pw_im[:, :, :, None, :] + c_im[None] * pw_re[:, :, :, None, :]
    tf = jnp.arange(q) + 1
    tb = q - jnp.arange(q)
    cr = jnp.stack([cp_re[tf, 0], -cp_im[tf, 0], cp_re[tb, 1], -cp_im[tb, 1]], 0)
    cr = cr.transpose(2, 0, 4, 1, 3).reshape(ngrp, 4, nst, q * nch)
    cr = cr.reshape(npair, 2, 4, nst, q * nch)
    zc = jnp.zeros_like(cr[:, 0])
    wc0 = jnp.concatenate([cr[:, 0], zc], 2)
    wc1 = jnp.concatenate([zc, cr[:, 1]], 2)
    wc = jnp.stack([wc0, wc1], 1).reshape(ngrp, 4 * 2 * nst, q * nch)
    a16 = jnp.stack([pw_re[q, 0], pw_im[q, 0], pw_re[q, 1], pw_im[q, 1]], 0)
    a16 = jnp.pad(a16.reshape(4, ngrp * nst), ((0, 4), (0, 0)))
    return toep.astype(BF16), wa.astype(BF16), wc.astype(BF16), a16


def _s5_row_chunk(nc):
    for r in (128, 96, 64, 48, 32, 16):
        if nc % r == 0:
            return r
    raise ValueError(f"unsupported chunk count {nc}")


def _s5_summary_kernel(u_ref, w_ref, ug_ref, o_ref, *, nc):
    q, nch = S5_CHUNK, S5_GROUP
    ngl = u_ref.shape[1] // nch
    rc = _s5_row_chunk(nc)

    def regroup(k, c):
        r0 = pl.multiple_of(k * rc, rc)
        steps = [u_ref[pl.ds(r0 * q + s, rc, stride=q), :] for s in range(q)]
        for gl in range(ngl):
            ug_ref[gl, pl.ds(r0, rc), :] = jnp.concatenate(
                [p[:, gl * nch:(gl + 1) * nch] for p in steps], axis=1).astype(BF16)
        return c

    lax.fori_loop(0, nc // rc, regroup, 0)
    for pr in range(ngl // 2):
        x = jnp.concatenate([ug_ref[2 * pr], ug_ref[2 * pr + 1]], axis=1)
        r = jnp.dot(x, w_ref[pr], preferred_element_type=F32)
        for qd in range(4):
            o_ref[qd, :, pr * S5_PAIR_LANES:(pr + 1) * S5_PAIR_LANES] = (
                r[:, qd * S5_PAIR_LANES:(qd + 1) * S5_PAIR_LANES])


def _s5_scan_kernel(s_ref, a_ref, h_ref, *, nb, n_lat, n_ctx):
    afr, afi = a_ref[0:1, :], a_ref[1:2, :]
    abr, abi = a_ref[2:3, :], a_ref[3:4, :]
    zero = jnp.zeros_like(afr)

    def step(rf, rb, carry):
        hfr, hfi, hbr, hbi = carry
        h_ref[0, pl.ds(rf, 1), :] = hfr
        h_ref[1, pl.ds(rf, 1), :] = hfi
        h_ref[2, pl.ds(rb, 1), :] = hbr
        h_ref[3, pl.ds(rb, 1), :] = hbi
        sfr, sfi = s_ref[0, pl.ds(rf, 1), :], s_ref[1, pl.ds(rf, 1), :]
        sbr, sbi = s_ref[2, pl.ds(rb, 1), :], s_ref[3, pl.ds(rb, 1), :]
        return (afr * hfr - afi * hfi + sfr, afr * hfi + afi * hfr + sfi,
                abr * hbr - abi * hbi + sbr, abr * hbi + abi * hbr + sbi)

    for b in range(nb):
        lat0 = b * n_lat
        ctx0 = nb * n_lat + b * n_ctx
        carry = lax.fori_loop(
            0, n_ctx, lambda i, c: step(ctx0 + i, ctx0 + n_ctx - 1 - i, c), (zero, zero, zero, zero))
        lax.fori_loop(0, n_lat, lambda i, c: step(lat0 + i, lat0 + n_lat - 1 - i, c), carry)


def _s5_output_kernel(ug_ref, t_ref, h_ref, w_ref, o_ref, y_ref, *, nc):
    q, nch = S5_CHUNK, S5_GROUP
    ngl = ug_ref.shape[0]
    rc = _s5_row_chunk(nc)
    for gl in range(ngl):
        pr = gl // 2
        hcat = jnp.concatenate(
            [h_ref[qd, :, pr * S5_PAIR_LANES:(pr + 1) * S5_PAIR_LANES] for qd in range(4)],
            axis=1).astype(BF16)
        y_ref[gl] = (jnp.dot(ug_ref[gl], t_ref[gl], preferred_element_type=F32)
                     + jnp.dot(hcat, w_ref[gl], preferred_element_type=F32))

    def ungroup(k, c):
        r0 = pl.multiple_of(k * rc, rc)
        ys = [y_ref[gl, pl.ds(r0, rc), :] for gl in range(ngl)]
        for s in range(q):
            o_ref[pl.ds(r0 * q + s, rc, stride=q), :] = jnp.concatenate(
                [y[:, s * nch:(s + 1) * nch] for y in ys], axis=1)
        return c

    lax.fori_loop(0, nc // rc, ungroup, 0)


def _s5_mixer(hz, weights, *, width, nb, n_lat, n_ctx):
    toep, wa, wc, a16 = weights
    t = hz.shape[0]
    q, nch = S5_CHUNK, S5_GROUP
    ngrp = width // nch
    ngl = V7X_LANES // nch
    nstrip = ngrp // ngl
    nc = t // q
    cw = q * nch
    lanes = ngrp * S5_STATE
    slanes = ngl * S5_STATE
    ug, summ = pl.pallas_call(
        functools.partial(_s5_summary_kernel, nc=nc),
        grid=(nstrip,),
        in_specs=[pl.BlockSpec((t, V7X_LANES), lambda j: (0, j)),
                  pl.BlockSpec((ngl // 2, 2 * cw, 4 * S5_PAIR_LANES), lambda j: (j, 0, 0))],
        out_specs=[pl.BlockSpec((ngl, nc, cw), lambda j: (j, 0, 0)),
                   pl.BlockSpec((4, nc, slanes), lambda j: (0, 0, j))],
        out_shape=[jax.ShapeDtypeStruct((ngrp, nc, cw), BF16),
                   jax.ShapeDtypeStruct((4, nc, lanes), F32)],
        compiler_params=_params(("arbitrary",)),
        name="s5_summary",
    )(hz, wa)
    wl = S5_SCAN_LANES
    carry = pl.pallas_call(
        functools.partial(_s5_scan_kernel, nb=nb, n_lat=n_lat // q, n_ctx=n_ctx // q),
        grid=(lanes // wl,),
        in_specs=[pl.BlockSpec((4, nc, wl), lambda j: (0, 0, j)),
                  pl.BlockSpec((8, wl), lambda j: (0, j))],
        out_specs=pl.BlockSpec((4, nc, wl), lambda j: (0, 0, j)),
        out_shape=jax.ShapeDtypeStruct((4, nc, lanes), F32),
        compiler_params=_params(("arbitrary",)),
        name="s5_scan",
    )(summ, a16)
    return pl.pallas_call(
        functools.partial(_s5_output_kernel, nc=nc),
        grid=(nstrip,),
        in_specs=[pl.BlockSpec((ngl, nc, cw), lambda j: (j, 0, 0)),
                  pl.BlockSpec((ngl, cw, cw), lambda j: (j, 0, 0)),
                  pl.BlockSpec((4, nc, slanes), lambda j: (0, 0, j)),
                  pl.BlockSpec((ngl, 4 * S5_PAIR_LANES, cw), lambda j: (j, 0, 0))],
        out_specs=pl.BlockSpec((t, V7X_LANES), lambda j: (0, j)),
        out_shape=jax.ShapeDtypeStruct((t, width), F32),
        scratch_shapes=[pltpu.VMEM((ngl, nc, cw), F32)],
        compiler_params=_params(("arbitrary",)),
        name="s5_output",
    )(ug, toep, carry, wc)


def _dwconv_tile(pad_ref, main, prev, nxt, first, last, cw_ref, cb, rows):
    pad_ref[0:HALO, :] = jnp.where(first, 0.0, prev)
    pad_ref[HALO:HALO + rows, :] = main
    pad_ref[HALO + rows:2 * HALO + rows, :] = jnp.where(last, 0.0, nxt)
    acc = cb
    for k in range(CONV_W):
        off = HALO - CONV_PAD_LEFT + k
        acc = acc + cw_ref[k:k + 1, :] * pad_ref[off:off + rows, :]
    return acc


def _lru_kernel(v_ref, vp_ref, vn_ref, cw_ref, cb_ref, wa_ref, wx_ref, ba_ref, bx_ref, sp_ref,
                o_ref, pad_ref, a_ref, b_ref, h_ref, *, n_lat_tiles):
    d = pl.program_id(0)
    k = pl.program_id(2)
    tc = LRU_TILE
    j = jnp.where(d == 0, k - 1, n_lat_tiles - k)
    first = jnp.logical_or(k == 0, j == 0)
    last = jnp.logical_or(k == 0, j == n_lat_tiles - 1)
    vc = _dwconv_tile(pad_ref, v_ref[...], vp_ref[...], vn_ref[...], first, last, cw_ref,
                      cb_ref[...], tc)
    vcb = vc.astype(BF16)
    nblk = vc.shape[1] // LRU_GATE_BLOCK

    def gate(w_ref, bias):
        parts = [jnp.dot(vcb[:, m * LRU_GATE_BLOCK:(m + 1) * LRU_GATE_BLOCK], w_ref[m],
                         preferred_element_type=F32) for m in range(nblk)]
        return jax.nn.sigmoid(jnp.concatenate(parts, axis=1) + bias)

    r = gate(wa_ref, ba_ref[...])
    i = gate(wx_ref, bx_ref[...])
    a = jnp.exp(-LRU_C * r * sp_ref[...])
    a_ref[...] = a
    b_ref[...] = jnp.sqrt(1.0 - a * a) * (i * vc)

    @pl.when(k == 0)
    def _():
        h_ref[...] = jnp.zeros_like(h_ref)

    def body(t, h):
        tt = jnp.where(d == 0, t, tc - 1 - t)
        h = a_ref[pl.ds(tt, 1), :] * h + b_ref[pl.ds(tt, 1), :]
        o_ref[pl.ds(tt, 1), :] = h
        return h

    h_ref[0:1, :] = lax.fori_loop(0, tc, body, h_ref[0:1, :], unroll=8)


def _lru_scan(hz, col_block, conv_w, conv_b, wa, wx, ba, bx, sp, *, nb, n_lat, n_ctx):
    t = hz.shape[0]
    width = conv_w.shape[1]
    tc = LRU_TILE
    assert n_ctx == tc and n_lat % tc == 0
    nlt = n_lat // tc
    hb = tc // HALO
    n_halo_blocks = t // HALO

    def row_block(d, b, k):
        j = jnp.where(d == 0, k - 1, nlt - k)
        return jnp.where(k == 0, nb * nlt + b, b * nlt + j)

    cw = jnp.pad(conv_w, ((0, 8 - CONV_W), (0, 0)))
    nblk = width // LRU_GATE_BLOCK
    vec = lambda: pl.BlockSpec((None, 1, width), lambda d, b, k: (d, 0, 0))
    return pl.pallas_call(
        functools.partial(_lru_kernel, n_lat_tiles=nlt),
        grid=(2, nb, nlt + 1),
        in_specs=[pl.BlockSpec((tc, width), lambda d, b, k: (row_block(d, b, k), col_block)),
                  pl.BlockSpec((HALO, width),
                               lambda d, b, k: (jnp.maximum(row_block(d, b, k) * hb - 1, 0), col_block)),
                  pl.BlockSpec((HALO, width),
                               lambda d, b, k: (jnp.minimum(row_block(d, b, k) * hb + hb,
                                                            n_halo_blocks - 1), col_block)),
                  pl.BlockSpec((8, width), lambda d, b, k: (0, 0)),
                  pl.BlockSpec((1, width), lambda d, b, k: (0, 0)),
                  pl.BlockSpec((None, nblk, LRU_GATE_BLOCK, LRU_GATE_BLOCK), lambda d, b, k: (d, 0, 0, 0)),
                  pl.BlockSpec((None, nblk, LRU_GATE_BLOCK, LRU_GATE_BLOCK), lambda d, b, k: (d, 0, 0, 0)),
                  vec(), vec(), vec()],
        out_specs=pl.BlockSpec((None, tc, width), lambda d, b, k: (d, row_block(d, b, k), 0)),
        out_shape=jax.ShapeDtypeStruct((2, t, width), F32),
        scratch_shapes=[pltpu.VMEM((tc + 2 * HALO, width), F32),
                        pltpu.VMEM((tc, width), F32),
                        pltpu.VMEM((tc, width), F32),
                        pltpu.VMEM((8, width), F32)],
        compiler_params=_params(("arbitrary", "arbitrary", "arbitrary")),
        name="rglru_scan",
    )(hz, hz, hz, cw, conv_b.reshape(1, width), wa, wx, ba, bx, sp)


def _block_diag_gates(w):
    ndir, heads, hd, _ = w.shape
    per = LRU_GATE_BLOCK // hd
    w = w.reshape(ndir, heads // per, per, hd, hd)
    eye = jnp.eye(per, dtype=w.dtype)
    full = jnp.einsum('dmhij,hk->dmhikj', w, eye)
    return full.reshape(ndir, heads // per, per * hd, per * hd).astype(BF16)


def _out0_kernel(ys_ref, u_ref, g_ref, hf_ref, hb_ref, x_ref, m_ref, d_ref, gw_ref, gb_ref,
                 wt_ref, wb_ref, o_ref):
    y = ys_ref[...] + d_ref[...] * u_ref[...]
    z = _gelu(y)
    gate = jax.nn.sigmoid(jnp.dot(z.astype(BF16), gw_ref[...], preferred_element_type=F32) + gb_ref[...])
    a = z * gate
    r = _gelu(g_ref[...]) * (hf_ref[...] + hb_ref[...])
    dx = (jnp.dot(a.astype(BF16), wt_ref[...], preferred_element_type=F32)
          + jnp.dot(r.astype(BF16), wb_ref[...], preferred_element_type=F32))
    o_ref[...] = x_ref[...] + m_ref[2:3, :] * dx


def _out0(ys, hz, hfb, xa, slab, d_skip, glu_w, glu_b, w_out, *, tiles_per_batch):
    t, d = xa.shape
    width = ys.shape[1]
    tm = TOKEN_TILE // 2
    last_row = slab.shape[0] - 1
    tpb = tiles_per_batch * (TOKEN_TILE // tm)
    const = lambda shape: pl.BlockSpec(shape, lambda i: tuple(0 for _ in shape))
    return pl.pallas_call(
        _out0_kernel,
        grid=(t // tm,),
        in_specs=[pl.BlockSpec((tm, width), lambda i: (i, 0)),
                  pl.BlockSpec((tm, width), lambda i: (i, 0)),
                  pl.BlockSpec((tm, width), lambda i: (i, 1)),
                  pl.BlockSpec((None, tm, width), lambda i: (0, i, 0)),
                  pl.BlockSpec((None, tm, width), lambda i: (1, i, 0)),
                  pl.BlockSpec((tm, d), lambda i: (i, 0)),
                  pl.BlockSpec((None, 8, d), lambda i: (jnp.minimum(i // tpb, last_row), 0, 0)),
                  const((1, width)), const((width, width)), const((1, width)),
                  const((width, d)), const((width, d))],
        out_specs=pl.BlockSpec((tm, d), lambda i: (i, 0)),
        out_shape=jax.ShapeDtypeStruct((t, d), F32),
        compiler_params=_params(("arbitrary",)),
        name="hybrid_out",
    )(ys, hz, hz, hfb, hfb, xa, slab, d_skip.reshape(1, width), glu_w.astype(BF16),
      glu_b.reshape(1, width), w_out[:width].astype(BF16), w_out[width:].astype(BF16))


def _store_token_rows(dst_ref, val):
    rows, d = val.shape
    per = d // V7X_LANES
    for c in range(per):
        dst_ref[pl.ds(c, rows, stride=per), :] = val[:, c * V7X_LANES:(c + 1) * V7X_LANES]


def _load_token_rows(src_ref, start, rows, per):
    return [src_ref[pl.ds(start * per + c, rows, stride=per), :] for c in range(per)]


def _route_kernel(x_ref, g_ref, m_ref, w_ref, b_ref, h_ref, r_ref, cnt_ref, carry_ref):
    @pl.when(pl.program_id(0) == 0)
    def _():
        carry_ref[...] = jnp.zeros_like(carry_ref)

    h = _norm_mod(x_ref[...], g_ref[...], m_ref, 3, 4)
    _store_token_rows(h_ref, h)
    logits = jnp.dot(h, w_ref[...], preferred_element_type=F32, precision=HIGHEST) + b_ref[...]
    lane = lax.broadcasted_iota(jnp.int32, logits.shape, 1).astype(F32)
    ninf = -jnp.inf
    big = float(ROUTE_LANES)
    lg = jnp.where(lane < MOE_GROUPS, logits, ninf)
    mg = jnp.max(lg, axis=-1, keepdims=True)
    gidx = jnp.min(jnp.where(lg == mg, lane, big), axis=-1, keepdims=True)
    p_top = 1.0 / jnp.sum(jnp.exp(lg - mg), axis=-1, keepdims=True)
    lo = MOE_GROUPS + gidx * MOE_PER_GROUP
    le = jnp.where(jnp.logical_and(lane >= lo, lane < lo + MOE_PER_GROUP), logits, ninf)
    v1 = jnp.max(le, axis=-1, keepdims=True)
    i1 = jnp.min(jnp.where(le == v1, lane, big), axis=-1, keepdims=True)
    le2 = jnp.where(lane == i1, ninf, le)
    v2 = jnp.max(le2, axis=-1, keepdims=True)
    i2 = jnp.min(jnp.where(le2 == v2, lane, big), axis=-1, keepdims=True)
    tt = jnp.exp(v2 - v1)
    g1 = p_top / (1.0 + tt)
    g2 = p_top * tt / (1.0 + tt)
    e1, e2 = i1 - MOE_GROUPS, i2 - MOE_GROUPS
    tm = h.shape[0]
    tri = (lax.broadcasted_iota(jnp.int32, (tm, tm), 1)
           < lax.broadcasted_iota(jnp.int32, (tm, tm), 0)).astype(BF16)
    hot1, hot2 = lane == e1, lane == e2
    before1 = jnp.dot(tri, hot1.astype(BF16), preferred_element_type=F32)
    before2 = jnp.dot(tri, hot2.astype(BF16), preferred_element_type=F32)
    tot1 = jnp.sum(hot1.astype(F32), axis=0, keepdims=True)
    tot2 = jnp.sum(hot2.astype(F32), axis=0, keepdims=True)
    carry = carry_ref[0:1, :]
    rank1 = jnp.sum(jnp.where(hot1, carry + before1, 0.0), axis=-1, keepdims=True)
    rank2 = jnp.sum(jnp.where(hot2, carry + tot1 + before2, 0.0), axis=-1, keepdims=True)
    carry = carry + tot1 + tot2
    carry_ref[0:1, :] = carry
    cnt_ref[...] = jnp.broadcast_to(carry, cnt_ref.shape)
    out = jnp.where(lane == 0, e1, 0.0)
    out = jnp.where(lane == 1, e2, out)
    out = jnp.where(lane == 2, g1, out)
    out = jnp.where(lane == 3, g2, out)
    out = jnp.where(lane == 4, rank1, out)
    out = jnp.where(lane == 5, rank2, out)
    r_ref[...] = out


def _route(xa, g, slab, wg, bg, we, be, *, tiles_per_batch):
    t, d = xa.shape
    tm = TOKEN_TILE
    per = d // V7X_LANES
    nl = MOE_GROUPS + MOE_EXPERTS
    wr = jnp.pad(jnp.concatenate([wg, we], axis=1), ((0, 0), (0, ROUTE_LANES - nl)))
    br = jnp.pad(jnp.concatenate([bg, be], axis=0), (0, ROUTE_LANES - nl)).reshape(1, ROUTE_LANES)
    last_row = slab.shape[0] - 1
    return pl.pallas_call(
        _route_kernel,
        grid=(t // tm,),
        in_specs=[pl.BlockSpec((tm, d), lambda i: (i, 0)),
                  pl.BlockSpec((1, d), lambda i: (0, 0)),
                  pl.BlockSpec((None, 8, d), lambda i: (jnp.minimum(i // tiles_per_batch, last_row), 0, 0)),
                  pl.BlockSpec((d, ROUTE_LANES), lambda i: (0, 0)),
                  pl.BlockSpec((1, ROUTE_LANES), lambda i: (0, 0))],
        out_specs=[pl.BlockSpec((tm * per, V7X_LANES), lambda i: (i, 0)),
                   pl.BlockSpec((tm, ROUTE_LANES), lambda i: (i, 0)),
                   pl.BlockSpec((8, ROUTE_LANES), lambda i: (0, 0))],
        out_shape=[jax.ShapeDtypeStruct((t * per, V7X_LANES), F32),
                   jax.ShapeDtypeStruct((t, ROUTE_LANES), F32),
                   jax.ShapeDtypeStruct((8, ROUTE_LANES), F32)],
        scratch_shapes=[pltpu.VMEM((8, ROUTE_LANES), F32)],
        compiler_params=_params(("arbitrary",)),
        name="moe_route",
    )(xa, g.reshape(1, d), slab, wr, br)


def _dispatch(route, counts):
    bm = MOE_BLOCK
    n_pairs = route.shape[0] * MOE_TOPK
    experts = route[:, 0:MOE_TOPK].astype(jnp.int32)
    rank = route[:, 4:4 + MOE_TOPK].astype(jnp.int32)
    cnt = counts[0, :MOE_EXPERTS].astype(jnp.int32)
    padded = (cnt + bm - 1) // bm * bm
    pad_end = jnp.cumsum(padded)
    dest = (pad_end - padded)[experts] + rank
    n_blocks = -(-n_pairs // bm) + MOE_EXPERTS
    block_e = jnp.minimum(
        jnp.searchsorted(pad_end, jnp.arange(n_blocks, dtype=jnp.int32) * bm, side='right'),
        MOE_EXPERTS - 1).astype(jnp.int32)
    n_used = (pad_end[-1] // bm).astype(jnp.int32).reshape(1)
    tail = jnp.maximum(pad_end - bm, 0).astype(jnp.int32)
    return dest, block_e, n_used, tail, n_blocks


def _scatter_kernel(tail_ref, nu_ref, d_ref, h_hbm, zero_hbm, xs_hbm, sem, *, n_steps, n_blocks, per):
    i = pl.program_id(0)
    tm = d_ref.shape[1] // MOE_TOPK
    zrows = zero_hbm.shape[0]

    def fill_at(slot):
        return pltpu.make_async_copy(zero_hbm, xs_hbm.at[pl.ds(slot * per, zrows), :], sem.at[2])

    def fill(e):
        return fill_at(tail_ref[e])

    def fill_unused(wait):
        def body(blk, c):
            cp = fill_at(blk * MOE_BLOCK)
            cp.wait() if wait else cp.start()
            return c
        lax.fori_loop(nu_ref[0], n_blocks, body, 0)

    def copy(step, r, k):
        tok = step * tm + r
        return pltpu.make_async_copy(h_hbm.at[pl.ds(tok * per, per), :],
                                     xs_hbm.at[pl.ds(d_ref[0, MOE_TOPK * r + k] * per, per), :],
                                     sem.at[lax.rem(step, 2)])

    def wait_step(step):
        for r in range(tm):
            for k in range(MOE_TOPK):
                copy(step, r, k).wait()

    @pl.when(i == 0)
    def _():
        for e in range(MOE_EXPERTS):
            fill(e).start()
        fill_unused(False)
        for e in range(MOE_EXPERTS):
            fill(e).wait()
        fill_unused(True)

    for r in range(tm):
        for k in range(MOE_TOPK):
            copy(i, r, k).start()

    @pl.when(i > 0)
    def _():
        wait_step(i - 1)

    @pl.when(i == n_steps - 1)
    def _():
        wait_step(i)


def _moe_scatter(h_rows, dest, tail, n_used, n_blocks, per):
    t = dest.shape[0]
    tm = TOKEN_TILE // 2
    n_steps = t // tm
    bm = MOE_BLOCK
    d3 = dest.reshape(n_steps, 1, MOE_TOPK * tm)
    zeros = jnp.zeros((bm * per, V7X_LANES), F32)
    grid_spec = pltpu.PrefetchScalarGridSpec(
        num_scalar_prefetch=2,
        grid=(n_steps,),
        in_specs=[pl.BlockSpec((None, 1, MOE_TOPK * tm), lambda i, tl, nu: (i, 0, 0),
                               memory_space=pltpu.SMEM),
                  pl.BlockSpec(memory_space=pl.ANY),
                  pl.BlockSpec(memory_space=pl.ANY)],
        out_specs=pl.BlockSpec(memory_space=pl.ANY),
        scratch_shapes=[pltpu.SemaphoreType.DMA((3,))],
    )
    return pl.pallas_call(
        functools.partial(_scatter_kernel, n_steps=n_steps, n_blocks=n_blocks, per=per),
        grid_spec=grid_spec,
        out_shape=jax.ShapeDtypeStruct((n_blocks * bm * per, V7X_LANES), F32),
        compiler_params=_params(("arbitrary",)),
        name="moe_scatter",
    )(tail, n_used, d3, h_rows, zeros)


def _token_copy(src_hbm, tok, dst, slot, r, sem, per):
    return pltpu.make_async_copy(src_hbm.at[pl.ds(tok * per, per), :],
                                 dst.at[slot, pl.ds(r * per, per), :], sem.at[slot])


def _moe_kernel(be_ref, nu_ref, xs_ref, w1_ref, w3_ref, w2_ref, o_ref, w1b, w3b, w2b, prev_ref):
    b = pl.program_id(0)
    bm = MOE_BLOCK
    per = w1_ref.shape[0] // V7X_LANES

    @pl.when(b == 0)
    def _():
        prev_ref[0] = -1

    @pl.when(b < nu_ref[0])
    def _():
        e = be_ref[b]

        @pl.when(e != prev_ref[0])
        def _():
            w1b[...] = w1_ref[...].astype(BF16)
            w3b[...] = w3_ref[...].astype(BF16)
            w2b[...] = w2_ref[...].astype(BF16)
            prev_ref[0] = e

        x = jnp.concatenate(_load_token_rows(xs_ref, 0, bm, per), axis=1).astype(BF16)
        h1 = jnp.dot(x, w1b[...], preferred_element_type=F32)
        h3 = jnp.dot(x, w3b[...], preferred_element_type=F32)
        act = (_silu(h1) * h3).astype(BF16)
        _store_token_rows(o_ref, jnp.dot(act, w2b[...], preferred_element_type=F32))

    @pl.when(b >= nu_ref[0])
    def _():
        o_ref[...] = jnp.zeros_like(o_ref)


def _moe_experts(xs_rows, block_e, n_used, n_blocks, w1, w3, w2):
    d, ff = w1.shape[1], w1.shape[2]
    per = d // V7X_LANES
    bm = MOE_BLOCK
    blk = lambda b, be, nu: jnp.minimum(b, nu[0] - 1)
    grid_spec = pltpu.PrefetchScalarGridSpec(
        num_scalar_prefetch=2,
        grid=(n_blocks,),
        in_specs=[pl.BlockSpec((bm * per, V7X_LANES), lambda b, be, nu: (blk(b, be, nu), 0)),
                  pl.BlockSpec((None, d, ff), lambda b, be, nu: (be[blk(b, be, nu)], 0, 0)),
                  pl.BlockSpec((None, d, ff), lambda b, be, nu: (be[blk(b, be, nu)], 0, 0)),
                  pl.BlockSpec((None, ff, d), lambda b, be, nu: (be[blk(b, be, nu)], 0, 0))],
        out_specs=pl.BlockSpec((bm * per, V7X_LANES), lambda b, be, nu: (b, 0)),
        scratch_shapes=[pltpu.VMEM((d, ff), BF16),
                        pltpu.VMEM((d, ff), BF16),
                        pltpu.VMEM((ff, d), BF16),
                        pltpu.SMEM((1,), jnp.int32)],
    )
    return pl.pallas_call(
        _moe_kernel,
        grid_spec=grid_spec,
        out_shape=jax.ShapeDtypeStruct((n_blocks * bm * per, V7X_LANES), F32),
        compiler_params=_params(("arbitrary",)),
        name="moe_experts",
    )(block_e, n_used, xs_rows, w1, w3, w2)


def _combine_kernel(d_ref, dn_ref, yb_hbm, x_ref, r_ref, m_ref, o_ref, ybuf, sem, *, n_tiles):
    i = pl.program_id(0)
    tm, d = x_ref.shape
    per = d // V7X_LANES
    nrow = MOE_TOPK * tm
    slot = lax.rem(i, 2)

    def start_rows(idx_ref, s):
        for r in range(nrow):
            _token_copy(yb_hbm, idx_ref[0, r], ybuf, s, r, sem, per).start()

    @pl.when(i == 0)
    def _():
        start_rows(d_ref, 0)

    @pl.when(i + 1 < n_tiles)
    def _():
        start_rows(dn_ref, 1 - slot)

    for r in range(nrow):
        _token_copy(yb_hbm, 0, ybuf, slot, r, sem, per).wait()
    r = r_ref[...]
    g0, g1 = r[:, 2:3], r[:, 3:4]
    y0 = _load_token_rows(ybuf.at[slot], 0, tm, per)
    y1 = _load_token_rows(ybuf.at[slot], tm, tm, per)
    for c in range(per):
        sl = slice(c * V7X_LANES, (c + 1) * V7X_LANES)
        o_ref[:, sl] = x_ref[:, sl] + m_ref[5:6, sl] * (g0 * y0[c] + g1 * y1[c])


def _moe_combine(yb_rows, dest, xa, route, slab, *, tiles_per_batch):
    t, d = xa.shape
    per = d // V7X_LANES
    tm = TOKEN_TILE // 2
    n_tiles = t // tm
    tpb = tiles_per_batch * (TOKEN_TILE // tm)
    last_row = slab.shape[0] - 1
    d3 = dest.reshape(n_tiles, tm, MOE_TOPK).transpose(0, 2, 1).reshape(n_tiles, 1, MOE_TOPK * tm)
    smem_block = lambda fn: pl.BlockSpec((None, 1, MOE_TOPK * tm), fn, memory_space=pltpu.SMEM)
    return pl.pallas_call(
        functools.partial(_combine_kernel, n_tiles=n_tiles),
        grid=(n_tiles,),
        in_specs=[smem_block(lambda i: (i, 0, 0)),
                  smem_block(lambda i: (jnp.minimum(i + 1, n_tiles - 1), 0, 0)),
                  pl.BlockSpec(memory_space=pl.ANY),
                  pl.BlockSpec((tm, d), lambda i: (i, 0)),
                  pl.BlockSpec((tm, ROUTE_LANES), lambda i: (i, 0)),
                  pl.BlockSpec((None, 8, d), lambda i: (jnp.minimum(i // tpb, last_row), 0, 0))],
        out_specs=pl.BlockSpec((tm, d), lambda i: (i, 0)),
        out_shape=jax.ShapeDtypeStruct((t, d), F32),
        scratch_shapes=[pltpu.VMEM((2, MOE_TOPK * tm * per, V7X_LANES), F32),
                        pltpu.SemaphoreType.DMA((2,))],
        compiler_params=_params(("arbitrary",)),
        name="moe_combine",
    )(d3, d3, yb_rows, xa, route, slab)


def _moe_layer(xa, g, slab, wg, bg, we, be, w1, w3, w2, *, tiles_per_batch):
    per = xa.shape[1] // V7X_LANES
    h_rows, route, counts = _route(xa, g, slab, wg, bg, we, be, tiles_per_batch=tiles_per_batch)
    dest, block_e, n_used, tail, n_blocks = _dispatch(route, counts)
    xs_rows = _moe_scatter(h_rows, dest, tail, n_used, n_blocks, per)
    yb_rows = _moe_experts(xs_rows, block_e, n_used, n_blocks, w1, w3, w2)
    return _moe_combine(yb_rows, dest, xa, route, slab, tiles_per_batch=tiles_per_batch)


def _ssd_kernel(xs_ref, xsp_ref, xsn_ref, bm_ref, bmp_ref, bmn_ref, cm_ref, cmp_ref, cmn_ref,
                dt_ref, cwx_ref, cbx_ref, cwb_ref, cbb_ref, cwc_ref, cbc_ref, dtb_ref, a_ref, dsk_ref,
                *rest, rev, n_ctx_chunks, n_lat_chunks):
    if rev:
        yf_ref, z_ref, ng_ref = rest[:3]
        rest = rest[3:]
    o_ref, padx, padb, padc, xpair, ypair, cbs, bts, cs, acst, dtt, wtt, tott, state = rest
    k = pl.program_id(1)
    q = SSD_CHUNK
    nst = SSD_STATE
    in_ctx = k < n_ctx_chunks
    kk = k - n_ctx_chunks
    if rev:
        jc, jl = n_ctx_chunks - 1 - k, n_lat_chunks - 1 - kk
    else:
        jc, jl = k, kk
    first = jnp.where(in_ctx, jc == 0, jl == 0)
    last = jnp.where(in_ctx, jc == n_ctx_chunks - 1, jl == n_lat_chunks - 1)

    def conv_silu(pad_ref, main, prev, nxt, cw_ref, cb_ref):
        return _silu(_dwconv_tile(pad_ref, main[...], prev[...], nxt[...], first, last, cw_ref,
                                  cb_ref[...], q))

    xs = conv_silu(padx, xs_ref, xsp_ref, xsn_ref, cwx_ref, cbx_ref)
    bmat = conv_silu(padb, bm_ref, bmp_ref, bmn_ref, cwb_ref, cbb_ref)
    cmat = conv_silu(padc, cm_ref, cmp_ref, cmn_ref, cwc_ref, cbc_ref)
    npair = xs.shape[1] // V7X_LANES
    ngrp = bmat.shape[1] // nst
    pairs_per_group = npair // ngrp
    for p in range(npair):
        xpair[p] = xs[:, p * V7X_LANES:(p + 1) * V7X_LANES].astype(BF16)
    for g in range(ngrp):
        bg = bmat[:, g * nst:(g + 1) * nst]
        cg = cmat[:, g * nst:(g + 1) * nst].astype(BF16)
        cs[g] = cg
        cbs[g] = lax.dot_general(cg, bg.astype(BF16), (((1,), (1,)), ((), ())),
                                 preferred_element_type=F32)
        bts[g] = bg.T
    dtr = dt_ref[...] + dtb_ref[...]
    dt = jnp.maximum(dtr, 0.0) + jnp.log(1.0 + jnp.exp(-jnp.abs(dtr)))
    adt = dt * a_ref[...]
    row = lax.broadcasted_iota(jnp.int32, (q, q), 0)
    col = lax.broadcasted_iota(jnp.int32, (q, q), 1)
    causal = (row <= col) if rev else (row >= col)
    acs = jnp.dot(causal.astype(F32), adt, preferred_element_type=F32, precision=HIGHEST)
    acs_t = acs.T
    tot = acs_t[:, 0:1] if rev else acs_t[:, q - 1:q]
    acst[...] = acs_t
    dtt[...] = dt.T
    wtt[...] = jnp.exp(tot - acs_t) * dt.T
    tott[...] = jnp.broadcast_to(jnp.exp(tot), (tott.shape[0], q))

    @pl.when(k == 0)
    def _():
        state[...] = jnp.zeros_like(state)

    hoff = SSD_HEAD_DIM if rev else 0
    lane = lax.broadcasted_iota(jnp.int32, (q, V7X_LANES), 1)
    left = lane < SSD_HEAD_DIM

    def pair_body(p, c):
        g = p // pairs_per_group
        h0 = hoff + 2 * p
        x = xpair[p]
        cb = cbs[g]
        st = state[p]
        ys, news, scales, decs = [], [], [], []
        for m in range(2):
            arow = acst[pl.ds(h0 + m, 1), :]
            acol = jnp.broadcast_to(arow, (q, q)).T
            seg = jnp.where(causal, jnp.exp(acol - arow), 0.0) * dtt[pl.ds(h0 + m, 1), :]
            mm = (cb * seg).astype(BF16)
            ys.append(jnp.dot(mm, x, preferred_element_type=F32))
            scales.append(jnp.exp(acol))
            wb = (bts[g] * wtt[pl.ds(h0 + m, 1), :]).astype(BF16)
            news.append(jnp.dot(wb, x, preferred_element_type=F32))
            decs.append(tott[pl.ds(h0 + m, 1), :])
        inter = jnp.dot(cs[g], st.astype(BF16), preferred_element_type=F32)
        y = jnp.where(left, ys[0], ys[1]) + inter * jnp.where(left, scales[0], scales[1])
        ypair[p] = y
        state[p] = st * jnp.where(left, decs[0], decs[1]) + jnp.where(left, news[0], news[1])
        return c

    lax.fori_loop(0, npair, pair_body, 0, unroll=SSD_PAIR_UNROLL)
    if not rev:
        for p in range(npair):
            sl = slice(p * V7X_LANES, (p + 1) * V7X_LANES)
            o_ref[:, sl] = ypair[p] + dsk_ref[:, sl] * xs[:, sl]
        return
    sq = jnp.zeros((q, V7X_LANES), F32)
    for p in range(npair):
        sl = slice(p * V7X_LANES, (p + 1) * V7X_LANES)
        gated = (ypair[p] + yf_ref[:, sl]) * _silu(z_ref[:, sl])
        ypair[p] = gated
        sq = sq + gated * gated
    scale = lax.rsqrt(jnp.sum(sq, axis=-1, keepdims=True) / (npair * V7X_LANES) + RMS_EPS)
    for p in range(npair):
        sl = slice(p * V7X_LANES, (p + 1) * V7X_LANES)
        o_ref[:, sl] = (ypair[p] * scale * ng_ref[:, sl]).astype(BF16)


def _ssd_scan(hz, conv_w, conv_b, dt_bias, a_neg, d_skip_lanes, *, rev, nb, n_lat, n_ctx, inner,
              y_fwd=None, norm_g=None):
    t = hz.shape[0]
    q = SSD_CHUNK
    bc = SSD_GROUPS * SSD_STATE
    ncc, ncl = n_ctx // q, n_lat // q
    hb = q // HALO
    n_halo_blocks = t // HALO
    nheads2 = dt_bias.size
    assert nheads2 == V7X_LANES and q == V7X_LANES and SSD_STATE == V7X_LANES
    npair = inner // V7X_LANES
    xs_col = inner // inner
    b_col = 2 * inner // bc
    c_col = b_col + 1
    dt_col = (2 * inner + 2 * bc) // nheads2

    def chunk(b, k):
        kk = k - ncc
        if rev:
            jc, jl = ncc - 1 - k, ncl - 1 - kk
        else:
            jc, jl = k, kk
        return jnp.where(k < ncc, nb * ncl + b * ncc + jc, b * ncl + jl)

    def main(width, colb):
        return pl.BlockSpec((q, width), lambda b, k: (chunk(b, k), colb))

    def prev(width, colb):
        return pl.BlockSpec((HALO, width), lambda b, k: (jnp.maximum(chunk(b, k) * hb - 1, 0), colb))

    def nxt(width, colb):
        return pl.BlockSpec((HALO, width),
                            lambda b, k: (jnp.minimum(chunk(b, k) * hb + hb, n_halo_blocks - 1), colb))

    const = lambda shape: pl.BlockSpec(shape, lambda b, k: (0, 0))
    cw = jnp.pad(conv_w, ((0, 8 - CONV_W), (0, 0)))
    cb = conv_b.reshape(1, -1)
    in_specs = [main(inner, xs_col), prev(inner, xs_col), nxt(inner, xs_col),
                main(bc, b_col), prev(bc, b_col), nxt(bc, b_col),
                main(bc, c_col), prev(bc, c_col), nxt(bc, c_col),
                main(nheads2, dt_col),
                const((8, inner)), const((1, inner)),
                const((8, bc)), const((1, bc)),
                const((8, bc)), const((1, bc)),
                const((1, nheads2)), const((1, nheads2)), const((1, inner))]
    args = [hz, hz, hz, hz, hz, hz, hz, hz, hz, hz,
            cw[:, :inner], cb[:, :inner], cw[:, inner:inner + bc], cb[:, inner:inner + bc],
            cw[:, inner + bc:], cb[:, inner + bc:], dt_bias.reshape(1, nheads2),
            a_neg.reshape(1, nheads2), d_skip_lanes]
    if rev:
        in_specs += [main(inner, 0), main(inner, 0), const((1, inner))]
        args += [y_fwd, hz, norm_g.reshape(1, inner)]
    return pl.pallas_call(
        functools.partial(_ssd_kernel, rev=rev, n_ctx_chunks=ncc, n_lat_chunks=ncl),
        grid=(nb, ncc + ncl),
        in_specs=in_specs,
        out_specs=pl.BlockSpec((q, inner), lambda b, k: (chunk(b, k), 0)),
        out_shape=jax.ShapeDtypeStruct((t, inner), BF16 if rev else F32),
        scratch_shapes=[pltpu.VMEM((q + 2 * HALO, inner), F32),
                        pltpu.VMEM((q + 2 * HALO, bc), F32),
                        pltpu.VMEM((q + 2 * HALO, bc), F32),
                        pltpu.VMEM((npair, q, V7X_LANES), BF16),
                        pltpu.VMEM((npair, q, V7X_LANES), F32),
                        pltpu.VMEM((SSD_GROUPS, q, q), F32),
                        pltpu.VMEM((SSD_GROUPS, SSD_STATE, q), F32),
                        pltpu.VMEM((SSD_GROUPS, q, SSD_STATE), BF16),
                        pltpu.VMEM((nheads2, q), F32),
                        pltpu.VMEM((nheads2, q), F32),
                        pltpu.VMEM((nheads2, q), F32),
                        pltpu.VMEM((nheads2, q), F32),
                        pltpu.VMEM((npair, SSD_STATE, V7X_LANES), F32)],
        compiler_params=_params(("arbitrary", "arbitrary")),
        name="ssd_scan_bwd" if rev else "ssd_scan_fwd",
    )(*args)


def _out1_kernel(yn_ref, x_ref, m_ref, w_ref, o_ref):
    o_ref[...] = x_ref[...] + m_ref[2:3, :] * jnp.dot(yn_ref[...], w_ref[...], preferred_element_type=F32)


def _out1(yn, xl, slab, w_out, *, tiles_per_batch):
    t, d = xl.shape
    inner = yn.shape[1]
    tm = TOKEN_TILE // 2
    tpb = tiles_per_batch * (TOKEN_TILE // tm)
    last_row = slab.shape[0] - 1
    return pl.pallas_call(
        _out1_kernel,
        grid=(t // tm,),
        in_specs=[pl.BlockSpec((tm, inner), lambda i: (i, 0)),
                  pl.BlockSpec((tm, d), lambda i: (i, 0)),
                  pl.BlockSpec((None, 8, d), lambda i: (jnp.minimum(i // tpb, last_row), 0, 0)),
                  pl.BlockSpec((inner, d), lambda i: (0, 0))],
        out_specs=pl.BlockSpec((tm, d), lambda i: (i, 0)),
        out_shape=jax.ShapeDtypeStruct((t, d), F32),
        compiler_params=_params(("arbitrary",)),
        name="ssd_out",
    )(yn, xl, slab, w_out.astype(BF16))


def _final_norm_kernel(x_ref, g_ref, o_ref):
    x = x_ref[...]
    ms = jnp.mean(x * x, axis=-1, keepdims=True)
    o_ref[...] = x * lax.rsqrt(ms + RMS_EPS) * g_ref[...]


def _final_norm(xl, g):
    t, d = xl.shape
    tm = TOKEN_TILE
    return pl.pallas_call(
        _final_norm_kernel,
        grid=(t // tm,),
        in_specs=[pl.BlockSpec((tm, d), lambda i: (i, 0)), pl.BlockSpec((1, d), lambda i: (0, 0))],
        out_specs=pl.BlockSpec((tm, d), lambda i: (i, 0)),
        out_shape=jax.ShapeDtypeStruct((t, d), F32),
        compiler_params=_params(("arbitrary",)),
        name="final_norm",
    )(xl, g.reshape(1, d))


def kernel(x, c, ctx, c_ctx, norm_mix_g, norm_ffn_g, mod_w, mod_b, hy_w_in, hy_w_out, s5_lam_re, s5_lam_im, s5_log_dt, s5_b_re, s5_b_im, s5_c_re, s5_c_im, s5_d, s5_glu_w, s5_glu_b, lru_conv_w, lru_conv_b, lru_wa, lru_ba, lru_wx, lru_bx, lru_lam, ssd_w_in, ssd_conv_w, ssd_conv_b, ssd_dt_bias, ssd_a_log, ssd_d, ssd_norm_g, ssd_w_out, moe_wg, moe_bg, moe_we, moe_be, moe_w1, moe_w3, moe_w2, final_norm_g):
    nb, n_lat, d = x.shape
    n_ctx = ctx.shape[1]
    depth = mod_w.shape[0]
    assert depth == 2 and nb + 1 <= 8
    assert n_lat % TOKEN_TILE == 0 and (nb * n_ctx) % TOKEN_TILE == 0
    t_lat = nb * n_lat
    rows = n_lat // GRID_W
    tiles_per_batch = n_lat // TOKEN_TILE
    n_cond = nb + 1

    xa = jnp.concatenate([x.reshape(t_lat, d), ctx.reshape(nb * n_ctx, d)], axis=0)
    cvec = jnp.zeros((8, d), F32).at[:nb].set(c).at[nb].set(c_ctx)
    mods = _mod_vectors(cvec, mod_w, mod_b)
    slab0 = _mod_slab(mods[0], n_cond, d)
    slab1 = _mod_slab(mods[1], n_cond, d)

    width = s5_d.shape[1]
    hz = _norm_matmul(xa, norm_mix_g[0], slab0, hy_w_in[0].astype(BF16), shift_row=0, scale_row=1,
                      tn=width, tiles_per_batch=tiles_per_batch)
    s5w = _s5_weights(s5_lam_re[0], s5_lam_im[0], s5_log_dt[0], s5_b_re[0], s5_b_im[0],
                      s5_c_re[0], s5_c_im[0])
    ys = _s5_mixer(hz, s5w, width=width, nb=nb, n_lat=n_lat, n_ctx=n_ctx)
    sp = jax.nn.softplus(-lru_lam[0].astype(F32)).reshape(2, 1, width)
    hfb = _lru_scan(hz, 2, lru_conv_w[0], lru_conv_b[0], _block_diag_gates(lru_wa[0]),
                    _block_diag_gates(lru_wx[0]), lru_ba[0].reshape(2, 1, width),
                    lru_bx[0].reshape(2, 1, width), sp, nb=nb, n_lat=n_lat, n_ctx=n_ctx)
    xa = _out0(ys, hz, hfb, xa, slab0, s5_d[0], s5_glu_w[0], s5_glu_b[0], hy_w_out[0],
               tiles_per_batch=tiles_per_batch)
    xa = _moe_layer(xa, norm_ffn_g[0], slab0, moe_wg[0], moe_bg[0], moe_we[0], moe_be[0],
                    moe_w1[0], moe_w3[0], moe_w2[0], tiles_per_batch=tiles_per_batch)

    lat = xa[:t_lat].reshape(nb, rows, GRID_W, d).transpose(0, 2, 1, 3).reshape(t_lat, d)
    xc = jnp.concatenate([lat, xa[t_lat:]], axis=0)
    inner = ssd_norm_g.shape[1]
    hz1 = _norm_matmul(xc, norm_mix_g[1], slab1, ssd_w_in[0].astype(BF16), shift_row=0, scale_row=1,
                       tn=ssd_w_in.shape[2] // 9, tiles_per_batch=tiles_per_batch)
    a_neg = -jnp.exp(ssd_a_log[0].astype(F32))
    dsk = jnp.repeat(ssd_d[0], SSD_HEAD_DIM).reshape(1, inner)
    ssd_args = dict(nb=nb, n_lat=n_lat, n_ctx=n_ctx, inner=inner)
    yf = _ssd_scan(hz1, ssd_conv_w[0], ssd_conv_b[0], ssd_dt_bias[0], a_neg, dsk, rev=False, **ssd_args)
    yn = _ssd_scan(hz1, ssd_conv_w[0], ssd_conv_b[0], ssd_dt_bias[0], a_neg, dsk, rev=True,
                   y_fwd=yf, norm_g=ssd_norm_g[0], **ssd_args)
    xl = _out1(yn, lat, slab1, ssd_w_out[0], tiles_per_batch=tiles_per_batch)
    xl = _moe_layer(xl, norm_ffn_g[1], slab1, moe_wg[1], moe_bg[1], moe_we[1], moe_be[1],
                    moe_w1[1], moe_w3[1], moe_w2[1], tiles_per_batch=tiles_per_batch)
    out = _final_norm(xl, final_norm_g)
    return out.reshape(nb, GRID_W, rows, d).transpose(0, 2, 1, 3).reshape(nb, n_lat, d)
```

```python
import functools
import math

import jax
import jax.numpy as jnp
from jax import lax
from jax.experimental import pallas as pl
from jax.experimental.pallas import tpu as pltpu

F32 = jnp.float32
BF16 = jnp.bfloat16
HIGHEST = lax.Precision.HIGHEST

GRID_W = 64
N_MOD = 6
RMS_EPS = 1e-6
CONV_W = 4
CONV_PAD_LEFT = CONV_W // 2
S5_GROUP = 16
S5_STATE = 64
LRU_HEADS = 16
LRU_C = 8.0
SSD_HEAD_DIM = 64
SSD_GROUPS = 8
SSD_STATE = 128
SSD_CHUNK = 128
MOE_GROUPS = 4
MOE_PER_GROUP = 8
MOE_EXPERTS = MOE_GROUPS * MOE_PER_GROUP
MOE_TOPK = 2

V7X_LANES = 128
V7X_SUBLANES = 8
V7X_MXU_DIM = 256
V7X_VMEM_LIMIT_BYTES = 60000 * 1024

TOKEN_TILE = 512
S5_CHUNK = V7X_MXU_DIM // S5_GROUP
S5_PAIR_LANES = 2 * S5_STATE
S5_SCAN_LANES = 512
LRU_TILE = 256
LRU_GATE_BLOCK = V7X_MXU_DIM
HALO = V7X_SUBLANES
MOE_BLOCK = 256
SSD_PAIR_UNROLL = 4
ROUTE_LANES = V7X_LANES


def _params(sem):
    return pltpu.CompilerParams(dimension_semantics=sem, vmem_limit_bytes=V7X_VMEM_LIMIT_BYTES)


def _silu(v):
    return v * jax.nn.sigmoid(v)


def _gelu(v):
    return jax.nn.gelu(v, approximate=True)


def _mod_kernel(c_ref, w_ref, b_ref, o_ref):
    s = _silu(c_ref[...])
    o_ref[...] = jnp.dot(s, w_ref[...], preferred_element_type=F32, precision=HIGHEST) + b_ref[...]


def _mod_vectors(cvec, mod_w, mod_b):
    depth, d, n = mod_w.shape
    tn = n // 8
    return pl.pallas_call(
        _mod_kernel,
        grid=(depth, n // tn),
        in_specs=[pl.BlockSpec((8, d), lambda l, j: (0, 0)),
                  pl.BlockSpec((None, d, tn), lambda l, j: (l, 0, j)),
                  pl.BlockSpec((None, 1, tn), lambda l, j: (l, 0, j))],
        out_specs=pl.BlockSpec((None, 8, tn), lambda l, j: (l, 0, j)),
        out_shape=jax.ShapeDtypeStruct((depth, 8, n), F32),
        compiler_params=_params(("arbitrary", "arbitrary")),
        name="mod_vectors",
    )(cvec, mod_w, mod_b.reshape(depth, 1, n))


def _mod_slab(mods_layer, n_rows, d):
    m = mods_layer[:n_rows].reshape(n_rows, N_MOD, d)
    return jnp.pad(m, ((0, 0), (0, 8 - N_MOD), (0, 0)))


def _norm_mod(x, g, m_ref, shift_row, scale_row):
    ms = jnp.mean(x * x, axis=-1, keepdims=True)
    y = x * lax.rsqrt(ms + RMS_EPS) * g
    return y * (1.0 + m_ref[scale_row:scale_row + 1, :]) + m_ref[shift_row:shift_row + 1, :]


def _norm_mm_kernel(x_ref, g_ref, m_ref, w_ref, o_ref, xn_ref, *, shift_row, scale_row):
    @pl.when(pl.program_id(1) == 0)
    def _():
        xn_ref[...] = _norm_mod(x_ref[...], g_ref[...], m_ref, shift_row, scale_row).astype(BF16)

    o_ref[...] = jnp.dot(xn_ref[...], w_ref[...], preferred_element_type=F32)


def _norm_matmul(xa, g, slab, w, *, shift_row, scale_row, tn, tiles_per_batch):
    t, d = xa.shape
    n = w.shape[1]
    tm = TOKEN_TILE
    last_row = slab.shape[0] - 1
    return pl.pallas_call(
        functools.partial(_norm_mm_kernel, shift_row=shift_row, scale_row=scale_row),
        grid=(t // tm, n // tn),
        in_specs=[pl.BlockSpec((tm, d), lambda i, j: (i, 0)),
                  pl.BlockSpec((1, d), lambda i, j: (0, 0)),
                  pl.BlockSpec((None, 8, d),
                               lambda i, j: (jnp.minimum(i // tiles_per_batch, last_row), 0, 0)),
                  pl.BlockSpec((d, tn), lambda i, j: (0, j))],
        out_specs=pl.BlockSpec((tm, tn), lambda i, j: (i, j)),
        out_shape=jax.ShapeDtypeStruct((t, n), F32),
        scratch_shapes=[pltpu.VMEM((tm, d), BF16)],
        compiler_params=_params(("arbitrary", "arbitrary")),
        name="norm_matmul",
    )(xa, g.reshape(1, d), slab, w)


def _s5_weights(lam_re, lam_im, log_dt, b_re, b_im, c_re, c_im):
    q = S5_CHUNK
    ngrp, nst = lam_re.shape[1], lam_re.shape[2]
    nch = b_re.shape[-1]
    lr, li = lam_re.astype(F32), lam_im.astype(F32)
    dt = jnp.exp(log_dt.astype(F32))[..., None]
    mag = jnp.exp(lr * dt)
    a_re, a_im = mag * jnp.cos(li * dt), mag * jnp.sin(li * dt)
    den = lr * lr + li * li
    k_re = ((a_re - 1) * lr + a_im * li) / den
    k_im = (a_im * lr - (a_re - 1) * li) / den
    bb_re = k_re[..., None] * b_re - k_im[..., None] * b_im
    bb_im = k_re[..., None] * b_im + k_im[..., None] * b_re
    ks = jnp.arange(q + 1, dtype=F32)[:, None, None, None]
    pmag = jnp.exp(ks * (lr * dt))
    pw_re, pw_im = pmag * jnp.cos(ks * (li * dt)), pmag * jnp.sin(ks * (li * dt))
    ab_re = pw_re[..., None] * bb_re - pw_im[..., None] * bb_im
    ab_im = pw_re[..., None] * bb_im + pw_im[..., None] * bb_re
    contract = lambda c, ab: jnp.sum(ab.transpose(1, 2, 0, 4, 3)[:, :, :, :, None, :]
                                     * c[:, :, None, None, :, :], axis=-1)
    kmat = contract(c_re, ab_re) - contract(c_im, ab_im)
    s_idx = jnp.arange(q)[:, None]
    t_idx = jnp.arange(q)[None, :]
    lag_f = t_idx - s_idx
    lag_b = s_idx - t_idx
    kf = jnp.where((lag_f >= 0)[None, :, :, None, None], kmat[0][:, jnp.clip(lag_f, 0, q - 1)], 0.0)
    kb = jnp.where((lag_b >= 0)[None, :, :, None, None], kmat[1][:, jnp.clip(lag_b, 0, q - 1)], 0.0)
    toep = (kf + kb).transpose(0, 1, 3, 2, 4).reshape(ngrp, q * nch, q * nch)
    sf_re = ab_re[q - 1 - jnp.arange(q), 0]
    sf_im = ab_im[q - 1 - jnp.arange(q), 0]
    sb_re = ab_re[jnp.arange(q), 1]
    sb_im = ab_im[jnp.arange(q), 1]
    summ = jnp.stack([sf_re, sf_im, sb_re, sb_im], 0)
    summ = summ.transpose(2, 1, 4, 0, 3).reshape(ngrp, q * nch, 4, nst)
    npair = ngrp // 2
    summ = summ.reshape(npair, 2, q * nch, 4, nst)
    zero = jnp.zeros_like(summ[:, 0])
    wa = jnp.concatenate([jnp.concatenate([summ[:, 0], zero], -1),
                          jnp.concatenate([zero, summ[:, 1]], -1)], 1)
    wa = wa.reshape(npair, 2 * q * nch, 4 * 2 * nst)
    cp_re = c_re[None] * pw_re[:, :, :, None, :] - c_im[None] * pw_im[:, :, :, None, :]
    cp_im = c_re[None] * pw_im[:, :, :, None, :] + c_im[None] * pw_re[:, :, :, None, :]
    tf = jnp.arange(q) + 1
    tb = q - jnp.arange(q)
    cr = jnp.stack([cp_re[tf, 0], -cp_im[tf, 0], cp_re[tb, 1], -cp_im[tb, 1]], 0)
    cr = cr.transpose(2, 0, 4, 1, 3).reshape(ngrp, 4, nst, q * nch)
    cr = cr.reshape(npair, 2, 4, nst, q * nch)
    zc = jnp.zeros_like(cr[:, 0])
    wc0 = jnp.concatenate([cr[:, 0], zc], 2)
    wc1 = jnp.concatenate([zc, cr[:, 1]], 2)
    wc = jnp.stack([wc0, wc1], 1).reshape(ngrp, 4 * 2 * nst, q * nch)
    a16 = jnp.stack([pw_re[q, 0], pw_im[q, 0], pw_re[q, 1], pw_im[q, 1]], 0)
    a16 = jnp.pad(a16.reshape(4, ngrp * nst), ((0, 4), (0, 0)))
    return toep.astype(BF16), wa.astype(BF16), wc.astype(BF16), a16


def _s5_row_chunk(nc):
    for r in (128, 96, 64, 48, 32, 16):
        if nc % r == 0:
            return r
    raise ValueError(f"unsupported chunk count {nc}")


def _s5_summary_kernel(u_ref, w_ref, ug_ref, o_ref, *, nc):
    q, nch = S5_CHUNK, S5_GROUP
    ngl = u_ref.shape[1] // nch
    rc = _s5_row_chunk(nc)

    def regroup(k, c):
        r0 = pl.multiple_of(k * rc, rc)
        steps = [u_ref[pl.ds(r0 * q + s, rc, stride=q), :] for s in range(q)]
        for gl in range(ngl):
            ug_ref[gl, pl.ds(r0, rc), :] = jnp.concatenate(
                [p[:, gl * nch:(gl + 1) * nch] for p in steps], axis=1).astype(BF16)
        return c

    lax.fori_loop(0, nc // rc, regroup, 0)
    for pr in range(ngl // 2):
        x = jnp.concatenate([ug_ref[2 * pr], ug_ref[2 * pr + 1]], axis=1)
        r = jnp.dot(x, w_ref[pr], preferred_element_type=F32)
        for qd in range(4):
            o_ref[qd, :, pr * S5_PAIR_LANES:(pr + 1) * S5_PAIR_LANES] = (
                r[:, qd * S5_PAIR_LANES:(qd + 1) * S5_PAIR_LANES])


def _s5_scan_kernel(s_ref, a_ref, h_ref, *, nb, n_lat, n_ctx):
    afr, afi = a_ref[0:1, :], a_ref[1:2, :]
    abr, abi = a_ref[2:3, :], a_ref[3:4, :]
    zero = jnp.zeros_like(afr)

    def step(rf, rb, carry):
        hfr, hfi, hbr, hbi = carry
        h_ref[0, pl.ds(rf, 1), :] = hfr
        h_ref[1, pl.ds(rf, 1), :] = hfi
        h_ref[2, pl.ds(rb, 1), :] = hbr
        h_ref[3, pl.ds(rb, 1), :] = hbi
        sfr, sfi = s_ref[0, pl.ds(rf, 1), :], s_ref[1, pl.ds(rf, 1), :]
        sbr, sbi = s_ref[2, pl.ds(rb, 1), :], s_ref[3, pl.ds(rb, 1), :]
        return (afr * hfr - afi * hfi + sfr, afr * hfi + afi * hfr + sfi,
                abr * hbr - abi * hbi + sbr, abr * hbi + abi * hbr + sbi)

    for b in range(nb):
        lat0 = b * n_lat
        ctx0 = nb * n_lat + b * n_ctx
        carry = lax.fori_loop(
            0, n_ctx, lambda i, c: step(ctx0 + i, ctx0 + n_ctx - 1 - i, c), (zero, zero, zero, zero))
        lax.fori_loop(0, n_lat, lambda i, c: step(lat0 + i, lat0 + n_lat - 1 - i, c), carry)


def _s5_output_kernel(ug_ref, t_ref, h_ref, w_ref, o_ref, y_ref, *, nc):
    q, nch = S5_CHUNK, S5_GROUP
    ngl = ug_ref.shape[0]
    rc = _s5_row_chunk(nc)
    for gl in range(ngl):
        pr = gl // 2
        hcat = jnp.concatenate(
            [h_ref[qd, :, pr * S5_PAIR_LANES:(pr + 1) * S5_PAIR_LANES] for qd in range(4)],
            axis=1).astype(BF16)
        y_ref[gl] = (jnp.dot(ug_ref[gl], t_ref[gl], preferred_element_type=F32)
                     + jnp.dot(hcat, w_ref[gl], preferred_element_type=F32))

    def ungroup(k, c):
        r0 = pl.multiple_of(k * rc, rc)
        ys = [y_ref[gl, pl.ds(r0, rc), :] for gl in range(ngl)]
        for s in range(q):
            o_ref[pl.ds(r0 * q + s, rc, stride=q), :] = jnp.concatenate(
                [y[:, s * nch:(s + 1) * nch] for y in ys], axis=1)
        return c

    lax.fori_loop(0, nc // rc, ungroup, 0)


def _s5_mixer(hz, weights, *, width, nb, n_lat, n_ctx):
    toep, wa, wc, a16 = weights
    t = hz.shape[0]
    q, nch = S5_CHUNK, S5_GROUP
    ngrp = width // nch
    ngl = V7X_LANES // nch
    nstrip = ngrp // ngl
    nc = t // q
    cw = q * nch
    lanes = ngrp * S5_STATE
    slanes = ngl * S5_STATE
    ug, summ = pl.pallas_call(
        functools.partial(_s5_summary_kernel, nc=nc),
        grid=(nstrip,),
        in_specs=[pl.BlockSpec((t, V7X_LANES), lambda j: (0, j)),
                  pl.BlockSpec((ngl // 2, 2 * cw, 4 * S5_PAIR_LANES), lambda j: (j, 0, 0))],
        out_specs=[pl.BlockSpec((ngl, nc, cw), lambda j: (j, 0, 0)),
                   pl.BlockSpec((4, nc, slanes), lambda j: (0, 0, j))],
        out_shape=[jax.ShapeDtypeStruct((ngrp, nc, cw), BF16),
                   jax.ShapeDtypeStruct((4, nc, lanes), F32)],
        compiler_params=_params(("arbitrary",)),
        name="s5_summary",
    )(hz, wa)
    wl = S5_SCAN_LANES
    carry = pl.pallas_call(
        functools.partial(_s5_scan_kernel, nb=nb, n_lat=n_lat // q, n_ctx=n_ctx // q),
        grid=(lanes // wl,),
        in_specs=[pl.BlockSpec((4, nc, wl), lambda j: (0, 0, j)),
                  pl.BlockSpec((8, wl), lambda j: (0, j))],
        out_specs=pl.BlockSpec((4, nc, wl), lambda j: (0, 0, j)),
        out_shape=jax.ShapeDtypeStruct((4, nc, lanes), F32),
        compiler_params=_params(("arbitrary",)),
        name="s5_scan",
    )(summ, a16)
    return pl.pallas_call(
        functools.partial(_s5_output_kernel, nc=nc),
        grid=(nstrip,),
        in_specs=[pl.BlockSpec((ngl, nc, cw), lambda j: (j, 0, 0)),
                  pl.BlockSpec((ngl, cw, cw), lambda j: (j, 0, 0)),
                  pl.BlockSpec((4, nc, slanes), lambda j: (0, 0, j)),
                  pl.BlockSpec((ngl, 4 * S5_PAIR_LANES, cw), lambda j: (j, 0, 0))],
        out_specs=pl.BlockSpec((t, V7X_LANES), lambda j: (0, j)),
        out_shape=jax.ShapeDtypeStruct((t, width), F32),
        scratch_shapes=[pltpu.VMEM((ngl, nc, cw), F32)],
        compiler_params=_params(("arbitrary",)),
        name="s5_output",
    )(ug, toep, carry, wc)


def _dwconv_tile(pad_ref, main, prev, nxt, first, last, cw_ref, cb, rows):
    pad_ref[0:HALO, :] = jnp.where(first, 0.0, prev)
    pad_ref[HALO:HALO + rows, :] = main
    pad_ref[HALO + rows:2 * HALO + rows, :] = jnp.where(last, 0.0, nxt)
    acc = cb
    for k in range(CONV_W):
        off = HALO - CONV_PAD_LEFT + k
        acc = acc + cw_ref[k:k + 1, :] * pad_ref[off:off + rows, :]
    return acc


def _lru_kernel(v_ref, vp_ref, vn_ref, cw_ref, cb_ref, wa_ref, wx_ref, ba_ref, bx_ref, sp_ref,
                o_ref, pad_ref, a_ref, b_ref, h_ref, *, n_lat_tiles):
    d = pl.program_id(0)
    k = pl.program_id(2)
    tc = LRU_TILE
    j = jnp.where(d == 0, k - 1, n_lat_tiles - k)
    first = jnp.logical_or(k == 0, j == 0)
    last = jnp.logical_or(k == 0, j == n_lat_tiles - 1)
    vc = _dwconv_tile(pad_ref, v_ref[...], vp_ref[...], vn_ref[...], first, last, cw_ref,
                      cb_ref[...], tc)
    vcb = vc.astype(BF16)
    nblk = vc.shape[1] // LRU_GATE_BLOCK

    def gate(w_ref, bias):
        parts = [jnp.dot(vcb[:, m * LRU_GATE_BLOCK:(m + 1) * LRU_GATE_BLOCK], w_ref[m],
                         preferred_element_type=F32) for m in range(nblk)]
        return jax.nn.sigmoid(jnp.concatenate(parts, axis=1) + bias)

    r = gate(wa_ref, ba_ref[...])
    i = gate(wx_ref, bx_ref[...])
    a = jnp.exp(-LRU_C * r * sp_ref[...])
    a_ref[...] = a
    b_ref[...] = jnp.sqrt(1.0 - a * a) * (i * vc)

    @pl.when(k == 0)
    def _():
        h_ref[...] = jnp.zeros_like(h_ref)

    def body(t, h):
        tt = jnp.where(d == 0, t, tc - 1 - t)
        h = a_ref[pl.ds(tt, 1), :] * h + b_ref[pl.ds(tt, 1), :]
        o_ref[pl.ds(tt, 1), :] = h
        return h

    h_ref[0:1, :] = lax.fori_loop(0, tc, body, h_ref[0:1, :], unroll=8)


def _lru_scan(hz, col_block, conv_w, conv_b, wa, wx, ba, bx, sp, *, nb, n_lat, n_ctx):
    t = hz.shape[0]
    width = conv_w.shape[1]
    tc = LRU_TILE
    assert n_ctx == tc and n_lat % tc == 0
    nlt = n_lat // tc
    hb = tc // HALO
    n_halo_blocks = t // HALO

    def row_block(d, b, k):
        j = jnp.where(d == 0, k - 1, nlt - k)
        return jnp.where(k == 0, nb * nlt + b, b * nlt + j)

    cw = jnp.pad(conv_w, ((0, 8 - CONV_W), (0, 0)))
    nblk = width // LRU_GATE_BLOCK
    vec = lambda: pl.BlockSpec((None, 1, width), lambda d, b, k: (d, 0, 0))
    return pl.pallas_call(
        functools.partial(_lru_kernel, n_lat_tiles=nlt),
        grid=(2, nb, nlt + 1),
        in_specs=[pl.BlockSpec((tc, width), lambda d, b, k: (row_block(d, b, k), col_block)),
                  pl.BlockSpec((HALO, width),
                               lambda d, b, k: (jnp.maximum(row_block(d, b, k) * hb - 1, 0), col_block)),
                  pl.BlockSpec((HALO, width),
                               lambda d, b, k: (jnp.minimum(row_block(d, b, k) * hb + hb,
                                                            n_halo_blocks - 1), col_block)),
                  pl.BlockSpec((8, width), lambda d, b, k: (0, 0)),
                  pl.BlockSpec((1, width), lambda d, b, k: (0, 0)),
                  pl.BlockSpec((None, nblk, LRU_GATE_BLOCK, LRU_GATE_BLOCK), lambda d, b, k: (d, 0, 0, 0)),
                  pl.BlockSpec((None, nblk, LRU_GATE_BLOCK, LRU_GATE_BLOCK), lambda d, b, k: (d, 0, 0, 0)),
                  vec(), vec(), vec()],
        out_specs=pl.BlockSpec((None, tc, width), lambda d, b, k: (d, row_block(d, b, k), 0)),
        out_shape=jax.ShapeDtypeStruct((2, t, width), F32),
        scratch_shapes=[pltpu.VMEM((tc + 2 * HALO, width), F32),
                        pltpu.VMEM((tc, width), F32),
                        pltpu.VMEM((tc, width), F32),
                        pltpu.VMEM((8, width), F32)],
        compiler_params=_params(("arbitrary", "arbitrary", "arbitrary")),
        name="rglru_scan",
    )(hz, hz, hz, cw, conv_b.reshape(1, width), wa, wx, ba, bx, sp)


def _block_diag_gates(w):
    ndir, heads, hd, _ = w.shape
    per = LRU_GATE_BLOCK // hd
    w = w.reshape(ndir, heads // per, per, hd, hd)
    eye = jnp.eye(per, dtype=w.dtype)
    full = jnp.einsum('dmhij,hk->dmhikj', w, eye)
    return full.reshape(ndir, heads // per, per * hd, per * hd).astype(BF16)


def _out0_kernel(ys_ref, u_ref, g_ref, hf_ref, hb_ref, x_ref, m_ref, d_ref, gw_ref, gb_ref,
                 wt_ref, wb_ref, o_ref):
    y = ys_ref[...] + d_ref[...] * u_ref[...]
    z = _gelu(y)
    gate = jax.nn.sigmoid(jnp.dot(z.astype(BF16), gw_ref[...], preferred_element_type=F32) + gb_ref[...])
    a = z * gate
    r = _gelu(g_ref[...]) * (hf_ref[...] + hb_ref[...])
    dx = (jnp.dot(a.astype(BF16), wt_ref[...], preferred_element_type=F32)
          + jnp.dot(r.astype(BF16), wb_ref[...], preferred_element_type=F32))
    o_ref[...] = x_ref[...] + m_ref[2:3, :] * dx


def _out0(ys, hz, hfb, xa, slab, d_skip, glu_w, glu_b, w_out, *, tiles_per_batch):
    t, d = xa.shape
    width = ys.shape[1]
    tm = TOKEN_TILE // 2
    last_row = slab.shape[0] - 1
    tpb = tiles_per_batch * (TOKEN_TILE // tm)
    const = lambda shape: pl.BlockSpec(shape, lambda i: tuple(0 for _ in shape))
    return pl.pallas_call(
        _out0_kernel,
        grid=(t // tm,),
        in_specs=[pl.BlockSpec((tm, width), lambda i: (i, 0)),
                  pl.BlockSpec((tm, width), lambda i: (i, 0)),
                  pl.BlockSpec((tm, width), lambda i: (i, 1)),
                  pl.BlockSpec((None, tm, width), lambda i: (0, i, 0)),
                  pl.BlockSpec((None, tm, width), lambda i: (1, i, 0)),
                  pl.BlockSpec((tm, d), lambda i: (i, 0)),
                  pl.BlockSpec((None, 8, d), lambda i: (jnp.minimum(i // tpb, last_row), 0, 0)),
                  const((1, width)), const((width, width)), const((1, width)),
                  const((width, d)), const((width, d))],
        out_specs=pl.BlockSpec((tm, d), lambda i: (i, 0)),
        out_shape=jax.ShapeDtypeStruct((t, d), F32),
        compiler_params=_params(("arbitrary",)),
        name="hybrid_out",
    )(ys, hz, hz, hfb, hfb, xa, slab, d_skip.reshape(1, width), glu_w.astype(BF16),
      glu_b.reshape(1, width), w_out[:width].astype(BF16), w_out[width:].astype(BF16))


def _store_token_rows(dst_ref, val):
    rows, d = val.shape
    per = d // V7X_LANES
    for c in range(per):
        dst_ref[pl.ds(c, rows, stride=per), :] = val[:, c * V7X_LANES:(c + 1) * V7X_LANES]


def _load_token_rows(src_ref, start, rows, per):
    return [src_ref[pl.ds(start * per + c, rows, stride=per), :] for c in range(per)]


def _route_kernel(x_ref, g_ref, m_ref, w_ref, b_ref, h_ref, r_ref, cnt_ref, carry_ref):
    @pl.when(pl.program_id(0) == 0)
    def _():
        carry_ref[...] = jnp.zeros_like(carry_ref)

    h = _norm_mod(x_ref[...], g_ref[...], m_ref, 3, 4)
    _store_token_rows(h_ref, h)
    logits = jnp.dot(h, w_ref[...], preferred_element_type=F32, precision=HIGHEST) + b_ref[...]
    lane = lax.broadcasted_iota(jnp.int32, logits.shape, 1).astype(F32)
    ninf = -jnp.inf
    big = float(ROUTE_LANES)
    lg = jnp.where(lane < MOE_GROUPS, logits, ninf)
    mg = jnp.max(lg, axis=-1, keepdims=True)
    gidx = jnp.min(jnp.where(lg == mg, lane, big), axis=-1, keepdims=True)
    p_top = 1.0 / jnp.sum(jnp.exp(lg - mg), axis=-1, keepdims=True)
    lo = MOE_GROUPS + gidx * MOE_PER_GROUP
    le = jnp.where(jnp.logical_and(lane >= lo, lane < lo + MOE_PER_GROUP), logits, ninf)
    v1 = jnp.max(le, axis=-1, keepdims=True)
    i1 = jnp.min(jnp.where(le == v1, lane, big), axis=-1, keepdims=True)
    le2 = jnp.where(lane == i1, ninf, le)
    v2 = jnp.max(le2, axis=-1, keepdims=True)
    i2 = jnp.min(jnp.where(le2 == v2, lane, big), axis=-1, keepdims=True)
    tt = jnp.exp(v2 - v1)
    g1 = p_top / (1.0 + tt)
    g2 = p_top * tt / (1.0 + tt)
    e1, e2 = i1 - MOE_GROUPS, i2 - MOE_GROUPS
    tm = h.shape[0]
    tri = (lax.broadcasted_iota(jnp.int32, (tm, tm), 1)
           < lax.broadcasted_iota(jnp.int32, (tm, tm), 0)).astype(BF16)
    hot1, hot2 = lane == e1, lane == e2
    before1 = jnp.dot(tri, hot1.astype(BF16), preferred_element_type=F32)
    before2 = jnp.dot(tri, hot2.astype(BF16), preferred_element_type=F32)
    tot1 = jnp.sum(hot1.astype(F32), axis=0, keepdims=True)
    tot2 = jnp.sum(hot2.astype(F32), axis=0, keepdims=True)
    carry = carry_ref[0:1, :]
    rank1 = jnp.sum(jnp.where(hot1, carry + before1, 0.0), axis=-1, keepdims=True)
    rank2 = jnp.sum(jnp.where(hot2, carry + tot1 + before2, 0.0), axis=-1, keepdims=True)
    carry = carry + tot1 + tot2
    carry_ref[0:1, :] = carry
    cnt_ref[...] = jnp.broadcast_to(carry, cnt_ref.shape)
    out = jnp.where(lane == 0, e1, 0.0)
    out = jnp.where(lane == 1, e2, out)
    out = jnp.where(lane == 2, g1, out)
    out = jnp.where(lane == 3, g2, out)
    out = jnp.where(lane == 4, rank1, out)
    out = jnp.where(lane == 5, rank2, out)
    r_ref[...] = out


def _route(xa, g, slab, wg, bg, we, be, *, tiles_per_batch):
    t, d = xa.shape
    tm = TOKEN_TILE
    per = d // V7X_LANES
    nl = MOE_GROUPS + MOE_EXPERTS
    wr = jnp.pad(jnp.concatenate([wg, we], axis=1), ((0, 0), (0, ROUTE_LANES - nl)))
    br = jnp.pad(jnp.concatenate([bg, be], axis=0), (0, ROUTE_LANES - nl)).reshape(1, ROUTE_LANES)
    last_row = slab.shape[0] - 1
    return pl.pallas_call(
        _route_kernel,
        grid=(t // tm,),
        in_specs=[pl.BlockSpec((tm, d), lambda i: (i, 0)),
                  pl.BlockSpec((1, d), lambda i: (0, 0)),
                  pl.BlockSpec((None, 8, d), lambda i: (jnp.minimum(i // tiles_per_batch, last_row), 0, 0)),
                  pl.BlockSpec((d, ROUTE_LANES), lambda i: (0, 0)),
                  pl.BlockSpec((1, ROUTE_LANES), lambda i: (0, 0))],
        out_specs=[pl.BlockSpec((tm * per, V7X_LANES), lambda i: (i, 0)),
                   pl.BlockSpec((tm, ROUTE_LANES), lambda i: (i, 0)),
                   pl.BlockSpec((8, ROUTE_LANES), lambda i: (0, 0))],
        out_shape=[jax.ShapeDtypeStruct((t * per, V7X_LANES), F32),
                   jax.ShapeDtypeStruct((t, ROUTE_LANES), F32),
                   jax.ShapeDtypeStruct((8, ROUTE_LANES), F32)],
        scratch_shapes=[pltpu.VMEM((8, ROUTE_LANES), F32)],
        compiler_params=_params(("arbitrary",)),
        name="moe_route",
    )(xa, g.reshape(1, d), slab, wr, br)


def _dispatch(route, counts):
    bm = MOE_BLOCK
    n_pairs = route.shape[0] * MOE_TOPK
    experts = route[:, 0:MOE_TOPK].astype(jnp.int32)
    rank = route[:, 4:4 + MOE_TOPK].astype(jnp.int32)
    cnt = counts[0, :MOE_EXPERTS].astype(jnp.int32)
    padded = (cnt + bm - 1) // bm * bm
    pad_end = jnp.cumsum(padded)
    dest = (pad_end - padded)[experts] + rank
    n_blocks = -(-n_pairs // bm) + MOE_EXPERTS
    first_slot = jnp.arange(n_blocks, dtype=jnp.int32) * bm
    block_e = jnp.minimum(jnp.sum((pad_end[None, :] <= first_slot[:, None]).astype(jnp.int32), axis=1),
                          MOE_EXPERTS - 1)
    n_used = (pad_end[-1] // bm).astype(jnp.int32).reshape(1)
    tail = jnp.maximum(pad_end - bm, 0).astype(jnp.int32)
    return dest, block_e, n_used, tail, n_blocks


def _scatter_kernel(tail_ref, nu_ref, d_ref, h_ref, xs_hbm, stage, zsrc, sem, *, n_steps, n_blocks, per):
    i = pl.program_id(0)
    tm = d_ref.shape[1] // MOE_TOPK
    zrows = zsrc.shape[0]

    def fill_at(slot):
        return pltpu.make_async_copy(zsrc, xs_hbm.at[pl.ds(slot * per, zrows), :], sem.at[2])

    def fill(e):
        return fill_at(tail_ref[e])

    def fill_unused(wait):
        def body(blk, c):
            cp = fill_at(blk * MOE_BLOCK)
            cp.wait() if wait else cp.start()
            return c
        lax.fori_loop(nu_ref[0], n_blocks, body, 0)

    def copy(step, r, k):
        return pltpu.make_async_copy(stage.at[lax.rem(step, 2), pl.ds(r * per, per), :],
                                     xs_hbm.at[pl.ds(d_ref[0, MOE_TOPK * r + k] * per, per), :],
                                     sem.at[lax.rem(step, 2)])

    def wait_step(step):
        for r in range(tm):
            for k in range(MOE_TOPK):
                copy(step, r, k).wait()

    @pl.when(i == 0)
    def _():
        zsrc[...] = jnp.zeros_like(zsrc)
        for e in range(MOE_EXPERTS):
            fill(e).start()
        fill_unused(False)
        for e in range(MOE_EXPERTS):
            fill(e).wait()
        fill_unused(True)

    stage[lax.rem(i, 2)] = h_ref[...]
    for r in range(tm):
        for k in range(MOE_TOPK):
            copy(i, r, k).start()

    @pl.when(i > 0)
    def _():
        wait_step(i - 1)

    @pl.when(i == n_steps - 1)
    def _():
        wait_step(i)


def _moe_scatter(h_rows, dest, tail, n_used, n_blocks, per):
    t = dest.shape[0]
    tm = TOKEN_TILE // 2
    n_steps = t // tm
    bm = MOE_BLOCK
    d3 = dest.reshape(n_steps, 1, MOE_TOPK * tm)
    grid_spec = pltpu.PrefetchScalarGridSpec(
        num_scalar_prefetch=2,
        grid=(n_steps,),
        in_specs=[pl.BlockSpec((None, 1, MOE_TOPK * tm), lambda i, tl, nu: (i, 0, 0),
                               memory_space=pltpu.SMEM),
                  pl.BlockSpec((tm * per, V7X_LANES), lambda i, tl, nu: (i, 0))],
        out_specs=pl.BlockSpec(memory_space=pl.ANY),
        scratch_shapes=[pltpu.VMEM((2, tm * per, V7X_LANES), F32),
                        pltpu.VMEM((bm * per, V7X_LANES), F32),
                        pltpu.SemaphoreType.DMA((3,))],
    )
    return pl.pallas_call(
        functools.partial(_scatter_kernel, n_steps=n_steps, n_blocks=n_blocks, per=per),
        grid_spec=grid_spec,
        out_shape=jax.ShapeDtypeStruct((n_blocks * bm * per, V7X_LANES), F32),
        compiler_params=_params(("arbitrary",)),
        name="moe_scatter",
    )(tail, n_used, d3, h_rows)


def _token_copy(src_hbm, tok, dst, slot, r, sem, per):
    return pltpu.make_async_copy(src_hbm.at[pl.ds(tok * per, per), :],
                                 dst.at[slot, pl.ds(r * per, per), :], sem.at[slot])


def _moe_kernel(be_ref, nu_ref, xs_ref, w1_ref, w3_ref, w2_ref, o_ref, w1b, w3b, w2b, prev_ref):
    b = pl.program_id(0)
    bm = MOE_BLOCK
    per = w1_ref.shape[0] // V7X_LANES

    @pl.when(b == 0)
    def _():
        prev_ref[0] = -1

    @pl.when(b < nu_ref[0])
    def _():
        e = be_ref[b]

        @pl.when(e != prev_ref[0])
        def _():
            w1b[...] = w1_ref[...].astype(BF16)
            w3b[...] = w3_ref[...].astype(BF16)
            w2b[...] = w2_ref[...].astype(BF16)
            prev_ref[0] = e

        x = jnp.concatenate(_load_token_rows(xs_ref, 0, bm, per), axis=1).astype(BF16)
        h1 = jnp.dot(x, w1b[...], preferred_element_type=F32)
        h3 = jnp.dot(x, w3b[...], preferred_element_type=F32)
        act = (_silu(h1) * h3).astype(BF16)
        _store_token_rows(o_ref, jnp.dot(act, w2b[...], preferred_element_type=F32))

    @pl.when(b >= nu_ref[0])
    def _():
        o_ref[...] = jnp.zeros_like(o_ref)


def _moe_experts(xs_rows, block_e, n_used, n_blocks, w1, w3, w2):
    d, ff = w1.shape[1], w1.shape[2]
    per = d // V7X_LANES
    bm = MOE_BLOCK
    blk = lambda b, be, nu: jnp.minimum(b, jnp.maximum(nu[0] - 1, 0))
    grid_spec = pltpu.PrefetchScalarGridSpec(
        num_scalar_prefetch=2,
        grid=(n_blocks,),
        in_specs=[pl.BlockSpec((bm * per, V7X_LANES), lambda b, be, nu: (blk(b, be, nu), 0)),
                  pl.BlockSpec((None, d, ff), lambda b, be, nu: (be[blk(b, be, nu)], 0, 0)),
                  pl.BlockSpec((None, d, ff), lambda b, be, nu: (be[blk(b, be, nu)], 0, 0)),
                  pl.BlockSpec((None, ff, d), lambda b, be, nu: (be[blk(b, be, nu)], 0, 0))],
        out_specs=pl.BlockSpec((bm * per, V7X_LANES), lambda b, be, nu: (b, 0)),
        scratch_shapes=[pltpu.VMEM((d, ff), BF16),
                        pltpu.VMEM((d, ff), BF16),
                        pltpu.VMEM((ff, d), BF16),
                        pltpu.SMEM((1,), jnp.int32)],
    )
    return pl.pallas_call(
        _moe_kernel,
        grid_spec=grid_spec,
        out_shape=jax.ShapeDtypeStruct((n_blocks * bm * per, V7X_LANES), F32),
        compiler_params=_params(("arbitrary",)),
        name="moe_experts",
    )(block_e, n_used, xs_rows, w1, w3, w2)


def _combine_kernel(d_ref, dn_ref, yb_hbm, x_ref, r_ref, m_ref, o_ref, ybuf, sem, *, n_tiles):
    i = pl.program_id(0)
    tm, d = x_ref.shape
    per = d // V7X_LANES
    nrow = MOE_TOPK * tm
    slot = lax.rem(i, 2)

    def start_rows(idx_ref, s):
        for r in range(nrow):
            _token_copy(yb_hbm, idx_ref[0, r], ybuf, s, r, sem, per).start()

    @pl.when(i == 0)
    def _():
        start_rows(d_ref, 0)

    @pl.when(i + 1 < n_tiles)
    def _():
        start_rows(dn_ref, 1 - slot)

    for r in range(nrow):
        _token_copy(yb_hbm, 0, ybuf, slot, r, sem, per).wait()
    r = r_ref[...]
    g0, g1 = r[:, 2:3], r[:, 3:4]
    y0 = _load_token_rows(ybuf.at[slot], 0, tm, per)
    y1 = _load_token_rows(ybuf.at[slot], tm, tm, per)
    for c in range(per):
        sl = slice(c * V7X_LANES, (c + 1) * V7X_LANES)
        o_ref[:, sl] = x_ref[:, sl] + m_ref[5:6, sl] * (g0 * y0[c] + g1 * y1[c])


def _moe_combine(yb_rows, dest, xa, route, slab, *, tiles_per_batch):
    t, d = xa.shape
    per = d // V7X_LANES
    tm = TOKEN_TILE // 2
    n_tiles = t // tm
    tpb = tiles_per_batch * (TOKEN_TILE // tm)
    last_row = slab.shape[0] - 1
    d3 = dest.reshape(n_tiles, tm, MOE_TOPK).transpose(0, 2, 1).reshape(n_tiles, 1, MOE_TOPK * tm)
    smem_block = lambda fn: pl.BlockSpec((None, 1, MOE_TOPK * tm), fn, memory_space=pltpu.SMEM)
    return pl.pallas_call(
        functools.partial(_combine_kernel, n_tiles=n_tiles),
        grid=(n_tiles,),
        in_specs=[smem_block(lambda i: (i, 0, 0)),
                  smem_block(lambda i: (jnp.minimum(i + 1, n_tiles - 1), 0, 0)),
                  pl.BlockSpec(memory_space=pl.ANY),
                  pl.BlockSpec((tm, d), lambda i: (i, 0)),
                  pl.BlockSpec((tm, ROUTE_LANES), lambda i: (i, 0)),
                  pl.BlockSpec((None, 8, d), lambda i: (jnp.minimum(i // tpb, last_row), 0, 0))],
        out_specs=pl.BlockSpec((tm, d), lambda i: (i, 0)),
        out_shape=jax.ShapeDtypeStruct((t, d), F32),
        scratch_shapes=[pltpu.VMEM((2, MOE_TOPK * tm * per, V7X_LANES), F32),
                        pltpu.SemaphoreType.DMA((2,))],
        compiler_params=_params(("arbitrary",)),
        name="moe_combine",
    )(d3, d3, yb_rows, xa, route, slab)


def _moe_layer(xa, g, slab, wg, bg, we, be, w1, w3, w2, *, tiles_per_batch):
    per = xa.shape[1] // V7X_LANES
    h_rows, route, counts = _route(xa, g, slab, wg, bg, we, be, tiles_per_batch=tiles_per_batch)
    dest, block_e, n_used, tail, n_blocks = _dispatch(route, counts)
    xs_rows = _moe_scatter(h_rows, dest, tail, n_used, n_blocks, per)
    yb_rows = _moe_experts(xs_rows, block_e, n_used, n_blocks, w1, w3, w2)
    return _moe_combine(yb_rows, dest, xa, route, slab, tiles_per_batch=tiles_per_batch)


def _ssd_kernel(xs_ref, xsp_ref, xsn_ref, bm_ref, bmp_ref, bmn_ref, cm_ref, cmp_ref, cmn_ref,
                dt_ref, cwx_ref, cbx_ref, cwb_ref, cbb_ref, cwc_ref, cbc_ref, dtb_ref, a_ref, dsk_ref,
                *rest, rev, n_ctx_chunks, n_lat_chunks):
    if rev:
        yf_ref, z_ref, ng_ref = rest[:3]
        rest = rest[3:]
    o_ref, padx, padb, padc, xpair, ypair, cbs, bts, cs, acst, dtt, wtt, tott, state = rest
    k = pl.program_id(1)
    q = SSD_CHUNK
    nst = SSD_STATE
    in_ctx = k < n_ctx_chunks
    kk = k - n_ctx_chunks
    if rev:
        jc, jl = n_ctx_chunks - 1 - k, n_lat_chunks - 1 - kk
    else:
        jc, jl = k, kk
    first = jnp.where(in_ctx, jc == 0, jl == 0)
    last = jnp.where(in_ctx, jc == n_ctx_chunks - 1, jl == n_lat_chunks - 1)

    def conv_silu(pad_ref, main, prev, nxt, cw_ref, cb_ref):
        return _silu(_dwconv_tile(pad_ref, main[...], prev[...], nxt[...], first, last, cw_ref,
                                  cb_ref[...], q))

    xs = conv_silu(padx, xs_ref, xsp_ref, xsn_ref, cwx_ref, cbx_ref)
    bmat = conv_silu(padb, bm_ref, bmp_ref, bmn_ref, cwb_ref, cbb_ref)
    cmat = conv_silu(padc, cm_ref, cmp_ref, cmn_ref, cwc_ref, cbc_ref)
    npair = xs.shape[1] // V7X_LANES
    ngrp = bmat.shape[1] // nst
    pairs_per_group = npair // ngrp
    for p in range(npair):
        xpair[p] = xs[:, p * V7X_LANES:(p + 1) * V7X_LANES].astype(BF16)
    for g in range(ngrp):
        bg = bmat[:, g * nst:(g + 1) * nst]
        cg = cmat[:, g * nst:(g + 1) * nst].astype(BF16)
        cs[g] = cg
        cbs[g] = lax.dot_general(cg, bg.astype(BF16), (((1,), (1,)), ((), ())),
                                 preferred_element_type=F32)
        bts[g] = bg.T
    dtr = dt_ref[...] + dtb_ref[...]
    dt = jnp.maximum(dtr, 0.0) + jnp.log(1.0 + jnp.exp(-jnp.abs(dtr)))
    adt = dt * a_ref[...]
    row = lax.broadcasted_iota(jnp.int32, (q, q), 0)
    col = lax.broadcasted_iota(jnp.int32, (q, q), 1)
    causal = (row <= col) if rev else (row >= col)
    acs = jnp.dot(causal.astype(F32), adt, preferred_element_type=F32, precision=HIGHEST)
    acs_t = acs.T
    tot = acs_t[:, 0:1] if rev else acs_t[:, q - 1:q]
    acst[...] = acs_t
    dtt[...] = dt.T
    wtt[...] = jnp.exp(tot - acs_t) * dt.T
    tott[...] = jnp.broadcast_to(jnp.exp(tot), (tott.shape[0], q))

    @pl.when(k == 0)
    def _():
        state[...] = jnp.zeros_like(state)

    hoff = SSD_HEAD_DIM if rev else 0
    lane = lax.broadcasted_iota(jnp.int32, (q, V7X_LANES), 1)
    left = lane < SSD_HEAD_DIM

    def pair_body(p, c):
        g = p // pairs_per_group
        h0 = hoff + 2 * p
        x = xpair[p]
        cb = cbs[g]
        st = state[p]
        ys, news, scales, decs = [], [], [], []
        for m in range(2):
            arow = acst[pl.ds(h0 + m, 1), :]
            acol = jnp.broadcast_to(arow, (q, q)).T
            seg = jnp.where(causal, jnp.exp(acol - arow), 0.0) * dtt[pl.ds(h0 + m, 1), :]
            mm = (cb * seg).astype(BF16)
            ys.append(jnp.dot(mm, x, preferred_element_type=F32))
            scales.append(jnp.exp(acol))
            wb = (bts[g] * wtt[pl.ds(h0 + m, 1), :]).astype(BF16)
            news.append(jnp.dot(wb, x, preferred_element_type=F32))
            decs.append(tott[pl.ds(h0 + m, 1), :])
        inter = jnp.dot(cs[g], st.astype(BF16), preferred_element_type=F32)
        y = jnp.where(left, ys[0], ys[1]) + inter * jnp.where(left, scales[0], scales[1])
        ypair[p] = y
        state[p] = st * jnp.where(left, decs[0], decs[1]) + jnp.where(left, news[0], news[1])
        return c

    lax.fori_loop(0, npair, pair_body, 0, unroll=SSD_PAIR_UNROLL)
    if not rev:
        for p in range(npair):
            sl = slice(p * V7X_LANES, (p + 1) * V7X_LANES)
            o_ref[:, sl] = ypair[p] + dsk_ref[:, sl] * xs[:, sl]
        return
    sq = jnp.zeros((q, V7X_LANES), F32)
    for p in range(npair):
        sl = slice(p * V7X_LANES, (p + 1) * V7X_LANES)
        gated = (ypair[p] + yf_ref[:, sl]) * _silu(z_ref[:, sl])
        ypair[p] = gated
        sq = sq + gated * gated
    scale = lax.rsqrt(jnp.sum(sq, axis=-1, keepdims=True) / (npair * V7X_LANES) + RMS_EPS)
    for p in range(npair):
        sl = slice(p * V7X_LANES, (p + 1) * V7X_LANES)
        o_ref[:, sl] = (ypair[p] * scale * ng_ref[:, sl]).astype(BF16)


def _ssd_scan(hz, conv_w, conv_b, dt_bias, a_neg, d_skip_lanes, *, rev, nb, n_lat, n_ctx, inner,
              y_fwd=None, norm_g=None):
    t = hz.shape[0]
    q = SSD_CHUNK
    bc = SSD_GROUPS * SSD_STATE
    ncc, ncl = n_ctx // q, n_lat // q
    hb = q // HALO
    n_halo_blocks = t // HALO
    nheads2 = dt_bias.size
    assert nheads2 == V7X_LANES and q == V7X_LANES and SSD_STATE == V7X_LANES
    npair = inner // V7X_LANES
    xs_col = inner // inner
    b_col = 2 * inner // bc
    c_col = b_col + 1
    dt_col = (2 * inner + 2 * bc) // nheads2

    def chunk(b, k):
        kk = k - ncc
        if rev:
            jc, jl = ncc - 1 - k, ncl - 1 - kk
        else:
            jc, jl = k, kk
        return jnp.where(k < ncc, nb * ncl + b * ncc + jc, b * ncl + jl)

    def main(width, colb):
        return pl.BlockSpec((q, width), lambda b, k: (chunk(b, k), colb))

    def prev(width, colb):
        return pl.BlockSpec((HALO, width), lambda b, k: (jnp.maximum(chunk(b, k) * hb - 1, 0), colb))

    def nxt(width, colb):
        return pl.BlockSpec((HALO, width),
                            lambda b, k: (jnp.minimum(chunk(b, k) * hb + hb, n_halo_blocks - 1), colb))

    const = lambda shape: pl.BlockSpec(shape, lambda b, k: (0, 0))
    cw = jnp.pad(conv_w, ((0, 8 - CONV_W), (0, 0)))
    cb = conv_b.reshape(1, -1)
    in_specs = [main(inner, xs_col), prev(inner, xs_col), nxt(inner, xs_col),
                main(bc, b_col), prev(bc, b_col), nxt(bc, b_col),
                main(bc, c_col), prev(bc, c_col), nxt(bc, c_col),
                main(nheads2, dt_col),
                const((8, inner)), const((1, inner)),
                const((8, bc)), const((1, bc)),
                const((8, bc)), const((1, bc)),
                const((1, nheads2)), const((1, nheads2)), const((1, inner))]
    args = [hz, hz, hz, hz, hz, hz, hz, hz, hz, hz,
            cw[:, :inner], cb[:, :inner], cw[:, inner:inner + bc], cb[:, inner:inner + bc],
            cw[:, inner + bc:], cb[:, inner + bc:], dt_bias.reshape(1, nheads2),
            a_neg.reshape(1, nheads2), d_skip_lanes]
    if rev:
        in_specs += [main(inner, 0), main(inner, 0), const((1, inner))]
        args += [y_fwd, hz, norm_g.reshape(1, inner)]
    return pl.pallas_call(
        functools.partial(_ssd_kernel, rev=rev, n_ctx_chunks=ncc, n_lat_chunks=ncl),
        grid=(nb, ncc + ncl),
        in_specs=in_specs,
        out_specs=pl.BlockSpec((q, inner), lambda b, k: (chunk(b, k), 0)),
        out_shape=jax.ShapeDtypeStruct((t, inner), BF16 if rev else F32),
        scratch_shapes=[pltpu.VMEM((q + 2 * HALO, inner), F32),
                        pltpu.VMEM((q + 2 * HALO, bc), F32),
                        pltpu.VMEM((q + 2 * HALO, bc), F32),
                        pltpu.VMEM((npair, q, V7X_LANES), BF16),
                        pltpu.VMEM((npair, q, V7X_LANES), F32),
                        pltpu.VMEM((SSD_GROUPS, q, q), F32),
                        pltpu.VMEM((SSD_GROUPS, SSD_STATE, q), F32),
                        pltpu.VMEM((SSD_GROUPS, q, SSD_STATE), BF16),
                        pltpu.VMEM((nheads2, q), F32),
                        pltpu.VMEM((nheads2, q), F32),
                        pltpu.VMEM((nheads2, q), F32),
                        pltpu.VMEM((nheads2, q), F32),
                        pltpu.VMEM((npair, SSD_STATE, V7X_LANES), F32)],
        compiler_params=_params(("arbitrary", "arbitrary")),
        name="ssd_scan_bwd" if rev else "ssd_scan_fwd",
    )(*args)


def _out1_kernel(yn_ref, x_ref, m_ref, w_ref, o_ref):
    o_ref[...] = x_ref[...] + m_ref[2:3, :] * jnp.dot(yn_ref[...], w_ref[...], preferred_element_type=F32)


def _out1(yn, xl, slab, w_out, *, tiles_per_batch):
    t, d = xl.shape
    inner = yn.shape[1]
    tm = TOKEN_TILE // 2
    tpb = tiles_per_batch * (TOKEN_TILE // tm)
    last_row = slab.shape[0] - 1
    return pl.pallas_call(
        _out1_kernel,
        grid=(t // tm,),
        in_specs=[pl.BlockSpec((tm, inner), lambda i: (i, 0)),
                  pl.BlockSpec((tm, d), lambda i: (i, 0)),
                  pl.BlockSpec((None, 8, d), lambda i: (jnp.minimum(i // tpb, last_row), 0, 0)),
                  pl.BlockSpec((inner, d), lambda i: (0, 0))],
        out_specs=pl.BlockSpec((tm, d), lambda i: (i, 0)),
        out_shape=jax.ShapeDtypeStruct((t, d), F32),
        compiler_params=_params(("arbitrary",)),
        name="ssd_out",
    )(yn, xl, slab, w_out.astype(BF16))


def _final_norm_kernel(x_ref, g_ref, o_ref):
    x = x_ref[...]
    ms = jnp.mean(x * x, axis=-1, keepdims=True)
    o_ref[...] = x * lax.rsqrt(ms + RMS_EPS) * g_ref[...]


def _final_norm(xl, g):
    t, d = xl.shape
    tm = TOKEN_TILE
    return pl.pallas_call(
        _final_norm_kernel,
        grid=(t // tm,),
        in_specs=[pl.BlockSpec((tm, d), lambda i: (i, 0)), pl.BlockSpec((1, d), lambda i: (0, 0))],
        out_specs=pl.BlockSpec((tm, d), lambda i: (i, 0)),
        out_shape=jax.ShapeDtypeStruct((t, d), F32),
        compiler_params=_params(("arbitrary",)),
        name="final_norm",
    )(xl, g.reshape(1, d))


def kernel(x, c, ctx, c_ctx, norm_mix_g, norm_ffn_g, mod_w, mod_b, hy_w_in, hy_w_out, s5_lam_re, s5_lam_im, s5_log_dt, s5_b_re, s5_b_im, s5_c_re, s5_c_im, s5_d, s5_glu_w, s5_glu_b, lru_conv_w, lru_conv_b, lru_wa, lru_ba, lru_wx, lru_bx, lru_lam, ssd_w_in, ssd_conv_w, ssd_conv_b, ssd_dt_bias, ssd_a_log, ssd_d, ssd_norm_g, ssd_w_out, moe_wg, moe_bg, moe_we, moe_be, moe_w1, moe_w3, moe_w2, final_norm_g):
    nb, n_lat, d = x.shape
    n_ctx = ctx.shape[1]
    depth = mod_w.shape[0]
    assert depth == 2 and nb + 1 <= 8
    assert n_lat % TOKEN_TILE == 0 and (nb * n_ctx) % TOKEN_TILE == 0
    t_lat = nb * n_lat
    rows = n_lat // GRID_W
    tiles_per_batch = n_lat // TOKEN_TILE
    n_cond = nb + 1

    xa = jnp.concatenate([x.reshape(t_lat, d), ctx.reshape(nb * n_ctx, d)], axis=0)
    cvec = jnp.zeros((8, d), F32).at[:nb].set(c).at[nb].set(c_ctx)
    mods = _mod_vectors(cvec, mod_w, mod_b)
    slab0 = _mod_slab(mods[0], n_cond, d)
    slab1 = _mod_slab(mods[1], n_cond, d)

    width = s5_d.shape[1]
    hz = _norm_matmul(xa, norm_mix_g[0], slab0, hy_w_in[0].astype(BF16), shift_row=0, scale_row=1,
                      tn=width, tiles_per_batch=tiles_per_batch)
    s5w = _s5_weights(s5_lam_re[0], s5_lam_im[0], s5_log_dt[0], s5_b_re[0], s5_b_im[0],
                      s5_c_re[0], s5_c_im[0])
    ys = _s5_mixer(hz, s5w, width=width, nb=nb, n_lat=n_lat, n_ctx=n_ctx)
    sp = jax.nn.softplus(-lru_lam[0].astype(F32)).reshape(2, 1, width)
    hfb = _lru_scan(hz, 2, lru_conv_w[0], lru_conv_b[0], _block_diag_gates(lru_wa[0]),
                    _block_diag_gates(lru_wx[0]), lru_ba[0].reshape(2, 1, width),
                    lru_bx[0].reshape(2, 1, width), sp, nb=nb, n_lat=n_lat, n_ctx=n_ctx)
    xa = _out0(ys, hz, hfb, xa, slab0, s5_d[0], s5_glu_w[0], s5_glu_b[0], hy_w_out[0],
               tiles_per_batch=tiles_per_batch)
    xa = _moe_layer(xa, norm_ffn_g[0], slab0, moe_wg[0], moe_bg[0], moe_we[0], moe_be[0],
                    moe_w1[0], moe_w3[0], moe_w2[0], tiles_per_batch=tiles_per_batch)

    lat = xa[:t_lat].reshape(nb, rows, GRID_W, d).transpose(0, 2, 1, 3).reshape(t_lat, d)
    xc = jnp.concatenate([lat, xa[t_lat:]], axis=0)
    inner = ssd_norm_g.shape[1]
    hz1 = _norm_matmul(xc, norm_mix_g[1], slab1, ssd_w_in[0].astype(BF16), shift_row=0, scale_row=1,
                       tn=ssd_w_in.shape[2] // 9, tiles_per_batch=tiles_per_batch)
    a_neg = -jnp.exp(ssd_a_log[0].astype(F32))
    dsk = jnp.repeat(ssd_d[0], SSD_HEAD_DIM).reshape(1, inner)
    ssd_args = dict(nb=nb, n_lat=n_lat, n_ctx=n_ctx, inner=inner)
    yf = _ssd_scan(hz1, ssd_conv_w[0], ssd_conv_b[0], ssd_dt_bias[0], a_neg, dsk, rev=False, **ssd_args)
    yn = _ssd_scan(hz1, ssd_conv_w[0], ssd_conv_b[0], ssd_dt_bias[0], a_neg, dsk, rev=True,
                   y_fwd=yf, norm_g=ssd_norm_g[0], **ssd_args)
    xl = _out1(yn, lat, slab1, ssd_w_out[0], tiles_per_batch=tiles_per_batch)
    xl = _moe_layer(xl, norm_ffn_g[1], slab1, moe_wg[1], moe_bg[1], moe_we[1], moe_be[1],
                    moe_w1[1], moe_w3[1], moe_w2[1], tiles_per_batch=tiles_per_batch)
    out = _final_norm(xl, final_norm_g)
    return out.reshape(nb, GRID_W, rows, d).transpose(0, 2, 1, 3).reshape(nb, n_lat, d)
```

```python
import functools
import math

import jax
import jax.numpy as jnp
from jax import lax
from jax.experimental import pallas as pl
from jax.experimental.pallas import tpu as pltpu

F32 = jnp.float32
BF16 = jnp.bfloat16
HIGHEST = lax.Precision.HIGHEST

GRID_W = 64
N_MOD = 6
RMS_EPS = 1e-6
CONV_W = 4
CONV_PAD_LEFT = CONV_W // 2
S5_GROUP = 16
S5_STATE = 64
LRU_HEADS = 16
LRU_C = 8.0
SSD_HEAD_DIM = 64
SSD_GROUPS = 8
SSD_STATE = 128
SSD_CHUNK = 128
MOE_GROUPS = 4
MOE_PER_GROUP = 8
MOE_EXPERTS = MOE_GROUPS * MOE_PER_GROUP
MOE_TOPK = 2

V7X_LANES = 128
V7X_SUBLANES = 8
V7X_MXU_DIM = 256
V7X_VMEM_LIMIT_BYTES = 60000 * 1024

TOKEN_TILE = 512
S5_CHUNK = V7X_MXU_DIM // S5_GROUP
S5_PAIR_LANES = 2 * S5_STATE
S5_SCAN_LANES = 512
LRU_TILE = 256
LRU_GATE_BLOCK = V7X_MXU_DIM
HALO = V7X_SUBLANES
MOE_BLOCK = 256
SSD_PAIR_UNROLL = 4
ROUTE_LANES = V7X_LANES


def _params(sem):
    return pltpu.CompilerParams(dimension_semantics=sem, vmem_limit_bytes=V7X_VMEM_LIMIT_BYTES)


def _silu(v):
    return v * jax.nn.sigmoid(v)


def _gelu(v):
    return jax.nn.gelu(v, approximate=True)


def _mod_kernel(c_ref, w_ref, b_ref, o_ref):
    s = _silu(c_ref[...])
    o_ref[...] = jnp.dot(s, w_ref[...], preferred_element_type=F32, precision=HIGHEST) + b_ref[...]


def _mod_vectors(cvec, mod_w, mod_b):
    depth, d, n = mod_w.shape
    tn = n // 8
    return pl.pallas_call(
        _mod_kernel,
        grid=(depth, n // tn),
        in_specs=[pl.BlockSpec((8, d), lambda l, j: (0, 0)),
                  pl.BlockSpec((None, d, tn), lambda l, j: (l, 0, j)),
                  pl.BlockSpec((None, 1, tn), lambda l, j: (l, 0, j))],
        out_specs=pl.BlockSpec((None, 8, tn), lambda l, j: (l, 0, j)),
        out_shape=jax.ShapeDtypeStruct((depth, 8, n), F32),
        compiler_params=_params(("arbitrary", "arbitrary")),
        name="mod_vectors",
    )(cvec, mod_w, mod_b.reshape(depth, 1, n))


def _mod_slab(mods_layer, n_rows, d):
    m = mods_layer[:n_rows].reshape(n_rows, N_MOD, d)
    return jnp.pad(m, ((0, 0), (0, 8 - N_MOD), (0, 0)))


def _norm_mod(x, g, m_ref, shift_row, scale_row):
    ms = jnp.mean(x * x, axis=-1, keepdims=True)
    y = x * lax.rsqrt(ms + RMS_EPS) * g
    return y * (1.0 + m_ref[scale_row:scale_row + 1, :]) + m_ref[shift_row:shift_row + 1, :]


def _norm_mm_kernel(x_ref, g_ref, m_ref, w_ref, o_ref, xn_ref, *, shift_row, scale_row):
    @pl.when(pl.program_id(1) == 0)
    def _():
        xn_ref[...] = _norm_mod(x_ref[...], g_ref[...], m_ref, shift_row, scale_row).astype(BF16)

    o_ref[...] = jnp.dot(xn_ref[...], w_ref[...], preferred_element_type=F32)


def _norm_matmul(xa, g, slab, w, *, shift_row, scale_row, tn, tiles_per_batch):
    t, d = xa.shape
    n = w.shape[1]
    tm = TOKEN_TILE
    last_row = slab.shape[0] - 1
    return pl.pallas_call(
        functools.partial(_norm_mm_kernel, shift_row=shift_row, scale_row=scale_row),
        grid=(t // tm, n // tn),
        in_specs=[pl.BlockSpec((tm, d), lambda i, j: (i, 0)),
                  pl.BlockSpec((1, d), lambda i, j: (0, 0)),
                  pl.BlockSpec((None, 8, d),
                               lambda i, j: (jnp.minimum(i // tiles_per_batch, last_row), 0, 0)),
                  pl.BlockSpec((d, tn), lambda i, j: (0, j))],
        out_specs=pl.BlockSpec((tm, tn), lambda i, j: (i, j)),
        out_shape=jax.ShapeDtypeStruct((t, n), F32),
        scratch_shapes=[pltpu.VMEM((tm, d), BF16)],
        compiler_params=_params(("arbitrary", "arbitrary")),
        name="norm_matmul",
    )(xa, g.reshape(1, d), slab, w)


def _s5_weights(lam_re, lam_im, log_dt, b_re, b_im, c_re, c_im):
    q = S5_CHUNK
    ngrp, nst = lam_re.shape[1], lam_re.shape[2]
    nch = b_re.shape[-1]
    lr, li = lam_re.astype(F32), lam_im.astype(F32)
    dt = jnp.exp(log_dt.astype(F32))[..., None]
    mag = jnp.exp(lr * dt)
    a_re, a_im = mag * jnp.cos(li * dt), mag * jnp.sin(li * dt)
    den = lr * lr + li * li
    k_re = ((a_re - 1) * lr + a_im * li) / den
    k_im = (a_im * lr - (a_re - 1) * li) / den
    bb_re = k_re[..., None] * b_re - k_im[..., None] * b_im
    bb_im = k_re[..., None] * b_im + k_im[..., None] * b_re
    ks = jnp.arange(q + 1, dtype=F32)[:, None, None, None]
    pmag = jnp.exp(ks * (lr * dt))
    pw_re, pw_im = pmag * jnp.cos(ks * (li * dt)), pmag * jnp.sin(ks * (li * dt))
    ab_re = pw_re[..., None] * bb_re - pw_im[..., None] * bb_im
    ab_im = pw_re[..., None] * bb_im + pw_im[..., None] * bb_re
    contract = lambda c, ab: jnp.sum(ab.transpose(1, 2, 0, 4, 3)[:, :, :, :, None, :]
                                     * c[:, :, None, None, :, :], axis=-1)
    kmat = contract(c_re, ab_re) - contract(c_im, ab_im)
    s_idx = jnp.arange(q)[:, None]
    t_idx = jnp.arange(q)[None, :]
    lag_f = t_idx - s_idx
    lag_b = s_idx - t_idx
    kf = jnp.where((lag_f >= 0)[None, :, :, None, None], kmat[0][:, jnp.clip(lag_f, 0, q - 1)], 0.0)
    kb = jnp.where((lag_b >= 0)[None, :, :, None, None], kmat[1][:, jnp.clip(lag_b, 0, q - 1)], 0.0)
    toep = (kf + kb).transpose(0, 1, 3, 2, 4).reshape(ngrp, q * nch, q * nch)
    sf_re = ab_re[q - 1 - jnp.arange(q), 0]
    sf_im = ab_im[q - 1 - jnp.arange(q), 0]
    sb_re = ab_re[jnp.arange(q), 1]
    sb_im = ab_im[jnp.arange(q), 1]
    summ = jnp.stack([sf_re, sf_im, sb_re, sb_im], 0)
    summ = summ.transpose(2, 1, 4, 0, 3).reshape(ngrp, q * nch, 4, nst)
    npair = ngrp // 2
    summ = summ.reshape(npair, 2, q * nch, 4, nst)
    zero = jnp.zeros_like(summ[:, 0])
    wa = jnp.concatenate([jnp.concatenate([summ[:, 0], zero], -1),
                          jnp.concatenate([zero, summ[:, 1]], -1)], 1)
    wa = wa.reshape(npair, 2 * q * nch, 4 * 2 * nst)
    cp_re = c_re[None] * pw_re[:, :, :, None, :] - c_im[None] * pw_im[:, :, :, None, :]
    cp_im = c_re[None] * pw_im[:, :, :, None, :] + c_im[None] * pw_re[:, :, :, None, :]
    tf = jnp.arange(q) + 1
    tb = q - jnp.arange(q)
    cr = jnp.stack([cp_re[tf, 0], -cp_im[tf, 0], cp_re[tb, 1], -cp_im[tb, 1]], 0)
    cr = cr.transpose(2, 0, 4, 1, 3).reshape(ngrp, 4, nst, q * nch)
    cr = cr.reshape(npair, 2, 4, nst, q * nch)
    zc = jnp.zeros_like(cr[:, 0])
    wc0 = jnp.concatenate([cr[:, 0], zc], 2)
    wc1 = jnp.concatenate([zc, cr[:, 1]], 2)
    wc = jnp.stack([wc0, wc1], 1).reshape(ngrp, 4 * 2 * nst, q * nch)
    a16 = jnp.stack([pw_re[q, 0], pw_im[q, 0], pw_re[q, 1], pw_im[q, 1]], 0)
    a16 = jnp.pad(a16.reshape(4, ngrp * nst), ((0, 4), (0, 0)))
    return toep.astype(BF16), wa.astype(BF16), wc.astype(BF16), a16


def _s5_row_chunk(nc):
    for r in (128, 96, 64, 48, 32, 16):
        if nc % r == 0:
            return r
    raise ValueError(f"unsupported chunk count {nc}")


def _s5_summary_kernel(u_ref, w_ref, ug_ref, o_ref, *, nc):
    q, nch = S5_CHUNK, S5_GROUP
    ngl = u_ref.shape[1] // nch
    rc = _s5_row_chunk(nc)

    def regroup(k, c):
        r0 = pl.multiple_of(k * rc, rc)
        steps = [u_ref[pl.ds(r0 * q + s, rc, stride=q), :] for s in range(q)]
        for gl in range(ngl):
            ug_ref[gl, pl.ds(r0, rc), :] = jnp.concatenate(
                [p[:, gl * nch:(gl + 1) * nch] for p in steps], axis=1).astype(BF16)
        return c

    lax.fori_loop(0, nc // rc, regroup, 0)
    for pr in range(ngl // 2):
        x = jnp.concatenate([ug_ref[2 * pr], ug_ref[2 * pr + 1]], axis=1)
        r = jnp.dot(x, w_ref[pr], preferred_element_type=F32)
        for qd in range(4):
            o_ref[qd, :, pr * S5_PAIR_LANES:(pr + 1) * S5_PAIR_LANES] = (
                r[:, qd * S5_PAIR_LANES:(qd + 1) * S5_PAIR_LANES])


def _s5_scan_kernel(s_ref, a_ref, h_ref, *, nb, n_lat, n_ctx):
    afr, afi = a_ref[0:1, :], a_ref[1:2, :]
    abr, abi = a_ref[2:3, :], a_ref[3:4, :]
    zero = jnp.zeros_like(afr)

    def step(rf, rb, carry):
        hfr, hfi, hbr, hbi = carry
        h_ref[0, pl.ds(rf, 1), :] = hfr
        h_ref[1, pl.ds(rf, 1), :] = hfi
        h_ref[2, pl.ds(rb, 1), :] = hbr
        h_ref[3, pl.ds(rb, 1), :] = hbi
        sfr, sfi = s_ref[0, pl.ds(rf, 1), :], s_ref[1, pl.ds(rf, 1), :]
        sbr, sbi = s_ref[2, pl.ds(rb, 1), :], s_ref[3, pl.ds(rb, 1), :]
        return (afr * hfr - afi * hfi + sfr, afr * hfi + afi * hfr + sfi,
                abr * hbr - abi * hbi + sbr, abr * hbi + abi * hbr + sbi)

    for b in range(nb):
        lat0 = b * n_lat
        ctx0 = nb * n_lat + b * n_ctx
        carry = lax.fori_loop(
            0, n_ctx, lambda i, c: step(ctx0 + i, ctx0 + n_ctx - 1 - i, c), (zero, zero, zero, zero))
        lax.fori_loop(0, n_lat, lambda i, c: step(lat0 + i, lat0 + n_lat - 1 - i, c), carry)


def _s5_output_kernel(ug_ref, t_ref, h_ref, w_ref, o_ref, y_ref, *, nc):
    q, nch = S5_CHUNK, S5_GROUP
    ngl = ug_ref.shape[0]
    rc = _s5_row_chunk(nc)
    for gl in range(ngl):
        pr = gl // 2
        hcat = jnp.concatenate(
            [h_ref[qd, :, pr * S5_PAIR_LANES:(pr + 1) * S5_PAIR_LANES] for qd in range(4)],
            axis=1).astype(BF16)
        y_ref[gl] = (jnp.dot(ug_ref[gl], t_ref[gl], preferred_element_type=F32)
                     + jnp.dot(hcat, w_ref[gl], preferred_element_type=F32))

    def ungroup(k, c):
        r0 = pl.multiple_of(k * rc, rc)
        ys = [y_ref[gl, pl.ds(r0, rc), :] for gl in range(ngl)]
        for s in range(q):
            o_ref[pl.ds(r0 * q + s, rc, stride=q), :] = jnp.concatenate(
                [y[:, s * nch:(s + 1) * nch] for y in ys], axis=1)
        return c

    lax.fori_loop(0, nc // rc, ungroup, 0)


def _s5_mixer(hz, weights, *, width, nb, n_lat, n_ctx):
    toep, wa, wc, a16 = weights
    t = hz.shape[0]
    q, nch = S5_CHUNK, S5_GROUP
    ngrp = width // nch
    ngl = V7X_LANES // nch
    nstrip = ngrp // ngl
    nc = t // q
    cw = q * nch
    lanes = ngrp * S5_STATE
    slanes = ngl * S5_STATE
    ug, summ = pl.pallas_call(
        functools.partial(_s5_summary_kernel, nc=nc),
        grid=(nstrip,),
        in_specs=[pl.BlockSpec((t, V7X_LANES), lambda j: (0, j)),
                  pl.BlockSpec((ngl // 2, 2 * cw, 4 * S5_PAIR_LANES), lambda j: (j, 0, 0))],
        out_specs=[pl.BlockSpec((ngl, nc, cw), lambda j: (j, 0, 0)),
                   pl.BlockSpec((4, nc, slanes), lambda j: (0, 0, j))],
        out_shape=[jax.ShapeDtypeStruct((ngrp, nc, cw), BF16),
                   jax.ShapeDtypeStruct((4, nc, lanes), F32)],
        compiler_params=_params(("arbitrary",)),
        name="s5_summary",
    )(hz, wa)
    wl = S5_SCAN_LANES
    carry = pl.pallas_call(
        functools.partial(_s5_scan_kernel, nb=nb, n_lat=n_lat // q, n_ctx=n_ctx // q),
        grid=(lanes // wl,),
        in_specs=[pl.BlockSpec((4, nc, wl), lambda j: (0, 0, j)),
                  pl.BlockSpec((8, wl), lambda j: (0, j))],
        out_specs=pl.BlockSpec((4, nc, wl), lambda j: (0, 0, j)),
        out_shape=jax.ShapeDtypeStruct((4, nc, lanes), F32),
        compiler_params=_params(("arbitrary",)),
        name="s5_scan",
    )(summ, a16)
    return pl.pallas_call(
        functools.partial(_s5_output_kernel, nc=nc),
        grid=(nstrip,),
        in_specs=[pl.BlockSpec((ngl, nc, cw), lambda j: (j, 0, 0)),
                  pl.BlockSpec((ngl, cw, cw), lambda j: (j, 0, 0)),
                  pl.BlockSpec((4, nc, slanes), lambda j: (0, 0, j)),
                  pl.BlockSpec((ngl, 4 * S5_PAIR_LANES, cw), lambda j: (j, 0, 0))],
        out_specs=pl.BlockSpec((t, V7X_LANES), lambda j: (0, j)),
        out_shape=jax.ShapeDtypeStruct((t, width), F32),
        scratch_shapes=[pltpu.VMEM((ngl, nc, cw), F32)],
        compiler_params=_params(("arbitrary",)),
        name="s5_output",
    )(ug, toep, carry, wc)


def _dwconv_tile(pad_ref, main, prev, nxt, first, last, cw_ref, cb, rows):
    pad_ref[0:HALO, :] = jnp.where(first, 0.0, prev)
    pad_ref[HALO:HALO + rows, :] = main
    pad_ref[HALO + rows:2 * HALO + rows, :] = jnp.where(last, 0.0, nxt)
    acc = cb
    for k in range(CONV_W):
        off = HALO - CONV_PAD_LEFT + k
        acc = acc + cw_ref[k:k + 1, :] * pad_ref[off:off + rows, :]
    return acc


def _lru_kernel(v_ref, vp_ref, vn_ref, cw_ref, cb_ref, wa_ref, wx_ref, ba_ref, bx_ref, sp_ref,
                o_ref, pad_ref, a_ref, b_ref, h_ref, *, n_lat_tiles):
    d = pl.program_id(0)
    k = pl.program_id(2)
    tc = LRU_TILE
    j = jnp.where(d == 0, k - 1, n_lat_tiles - k)
    first = jnp.logical_or(k == 0, j == 0)
    last = jnp.logical_or(k == 0, j == n_lat_tiles - 1)
    vc = _dwconv_tile(pad_ref, v_ref[...], vp_ref[...], vn_ref[...], first, last, cw_ref,
                      cb_ref[...], tc)
    vcb = vc.astype(BF16)
    nblk = vc.shape[1] // LRU_GATE_BLOCK

    def gate(w_ref, bias):
        parts = [jnp.dot(vcb[:, m * LRU_GATE_BLOCK:(m + 1) * LRU_GATE_BLOCK], w_ref[m],
                         preferred_element_type=F32) for m in range(nblk)]
        return jax.nn.sigmoid(jnp.concatenate(parts, axis=1) + bias)

    r = gate(wa_ref, ba_ref[...])
    i = gate(wx_ref, bx_ref[...])
    a = jnp.exp(-LRU_C * r * sp_ref[...])
    a_ref[...] = a
    b_ref[...] = jnp.sqrt(1.0 - a * a) * (i * vc)

    @pl.when(k == 0)
    def _():
        h_ref[...] = jnp.zeros_like(h_ref)

    def body(t, h):
        tt = jnp.where(d == 0, t, tc - 1 - t)
        h = a_ref[pl.ds(tt, 1), :] * h + b_ref[pl.ds(tt, 1), :]
        o_ref[pl.ds(tt, 1), :] = h
        return h

    h_ref[0:1, :] = lax.fori_loop(0, tc, body, h_ref[0:1, :], unroll=8)


def _lru_scan(hz, col_block, conv_w, conv_b, wa, wx, ba, bx, sp, *, nb, n_lat, n_ctx):
    t = hz.shape[0]
    width = conv_w.shape[1]
    tc = LRU_TILE
    assert n_ctx == tc and n_lat % tc == 0
    nlt = n_lat // tc
    hb = tc // HALO
    n_halo_blocks = t // HALO

    def row_block(d, b, k):
        j = jnp.where(d == 0, k - 1, nlt - k)
        return jnp.where(k == 0, nb * nlt + b, b * nlt + j)

    cw = jnp.pad(conv_w, ((0, 8 - CONV_W), (0, 0)))
    nblk = width // LRU_GATE_BLOCK
    vec = lambda: pl.BlockSpec((None, 1, width), lambda d, b, k: (d, 0, 0))
    return pl.pallas_call(
        functools.partial(_lru_kernel, n_lat_tiles=nlt),
        grid=(2, nb, nlt + 1),
        in_specs=[pl.BlockSpec((tc, width), lambda d, b, k: (row_block(d, b, k), col_block)),
                  pl.BlockSpec((HALO, width),
                               lambda d, b, k: (jnp.maximum(row_block(d, b, k) * hb - 1, 0), col_block)),
                  pl.BlockSpec((HALO, width),
                               lambda d, b, k: (jnp.minimum(row_block(d, b, k) * hb + hb,
                                                            n_halo_blocks - 1), col_block)),
                  pl.BlockSpec((8, width), lambda d, b, k: (0, 0)),
                  pl.BlockSpec((1, width), lambda d, b, k: (0, 0)),
                  pl.BlockSpec((None, nblk, LRU_GATE_BLOCK, LRU_GATE_BLOCK), lambda d, b, k: (d, 0, 0, 0)),
                  pl.BlockSpec((None, nblk, LRU_GATE_BLOCK, LRU_GATE_BLOCK), lambda d, b, k: (d, 0, 0, 0)),
                  vec(), vec(), vec()],
        out_specs=pl.BlockSpec((None, tc, width), lambda d, b, k: (d, row_block(d, b, k), 0)),
        out_shape=jax.ShapeDtypeStruct((2, t, width), F32),
        scratch_shapes=[pltpu.VMEM((tc + 2 * HALO, width), F32),
                        pltpu.VMEM((tc, width), F32),
                        pltpu.VMEM((tc, width), F32),
                        pltpu.VMEM((8, width), F32)],
        compiler_params=_params(("arbitrary", "arbitrary", "arbitrary")),
        name="rglru_scan",
    )(hz, hz, hz, cw, conv_b.reshape(1, width), wa, wx, ba, bx, sp)


def _block_diag_gates(w):
    ndir, heads, hd, _ = w.shape
    per = LRU_GATE_BLOCK // hd
    w = w.reshape(ndir, heads // per, per, hd, hd)
    eye = jnp.eye(per, dtype=w.dtype)
    full = jnp.einsum('dmhij,hk->dmhikj', w, eye)
    return full.reshape(ndir, heads // per, per * hd, per * hd).astype(BF16)


def _out0_kernel(ys_ref, u_ref, g_ref, hf_ref, hb_ref, x_ref, m_ref, d_ref, gw_ref, gb_ref,
                 wt_ref, wb_ref, o_ref):
    y = ys_ref[...] + d_ref[...] * u_ref[...]
    z = _gelu(y)
    gate = jax.nn.sigmoid(jnp.dot(z.astype(BF16), gw_ref[...], preferred_element_type=F32) + gb_ref[...])
    a = z * gate
    r = _gelu(g_ref[...]) * (hf_ref[...] + hb_ref[...])
    dx = (jnp.dot(a.astype(BF16), wt_ref[...], preferred_element_type=F32)
          + jnp.dot(r.astype(BF16), wb_ref[...], preferred_element_type=F32))
    o_ref[...] = x_ref[...] + m_ref[2:3, :] * dx


def _out0(ys, hz, hfb, xa, slab, d_skip, glu_w, glu_b, w_out, *, tiles_per_batch):
    t, d = xa.shape
    width = ys.shape[1]
    tm = TOKEN_TILE // 2
    last_row = slab.shape[0] - 1
    tpb = tiles_per_batch * (TOKEN_TILE // tm)
    const = lambda shape: pl.BlockSpec(shape, lambda i: tuple(0 for _ in shape))
    return pl.pallas_call(
        _out0_kernel,
        grid=(t // tm,),
        in_specs=[pl.BlockSpec((tm, width), lambda i: (i, 0)),
                  pl.BlockSpec((tm, width), lambda i: (i, 0)),
                  pl.BlockSpec((tm, width), lambda i: (i, 1)),
                  pl.BlockSpec((None, tm, width), lambda i: (0, i, 0)),
                  pl.BlockSpec((None, tm, width), lambda i: (1, i, 0)),
                  pl.BlockSpec((tm, d), lambda i: (i, 0)),
                  pl.BlockSpec((None, 8, d), lambda i: (jnp.minimum(i // tpb, last_row), 0, 0)),
                  const((1, width)), const((width, width)), const((1, width)),
                  const((width, d)), const((width, d))],
        out_specs=pl.BlockSpec((tm, d), lambda i: (i, 0)),
        out_shape=jax.ShapeDtypeStruct((t, d), F32),
        compiler_params=_params(("arbitrary",)),
        name="hybrid_out",
    )(ys, hz, hz, hfb, hfb, xa, slab, d_skip.reshape(1, width), glu_w.astype(BF16),
      glu_b.reshape(1, width), w_out[:width].astype(BF16), w_out[width:].astype(BF16))


def _store_token_rows(dst_ref, val):
    rows, d = val.shape
    per = d // V7X_LANES
    for c in range(per):
        dst_ref[pl.ds(c, rows, stride=per), :] = val[:, c * V7X_LANES:(c + 1) * V7X_LANES]


def _load_token_rows(src_ref, start, rows, per):
    return [src_ref[pl.ds(start * per + c, rows, stride=per), :] for c in range(per)]


def _route_kernel(x_ref, g_ref, m_ref, w_ref, b_ref, h_ref, r_ref, cnt_ref, carry_ref):
    @pl.when(pl.program_id(0) == 0)
    def _():
        carry_ref[...] = jnp.zeros_like(carry_ref)

    h = _norm_mod(x_ref[...], g_ref[...], m_ref, 3, 4)
    _store_token_rows(h_ref, h)
    logits = jnp.dot(h, w_ref[...], preferred_element_type=F32, precision=HIGHEST) + b_ref[...]
    lane = lax.broadcasted_iota(jnp.int32, logits.shape, 1).astype(F32)
    ninf = -jnp.inf
    big = float(ROUTE_LANES)
    lg = jnp.where(lane < MOE_GROUPS, logits, ninf)
    mg = jnp.max(lg, axis=-1, keepdims=True)
    gidx = jnp.min(jnp.where(lg == mg, lane, big), axis=-1, keepdims=True)
    p_top = 1.0 / jnp.sum(jnp.exp(lg - mg), axis=-1, keepdims=True)
    lo = MOE_GROUPS + gidx * MOE_PER_GROUP
    le = jnp.where(jnp.logical_and(lane >= lo, lane < lo + MOE_PER_GROUP), logits, ninf)
    v1 = jnp.max(le, axis=-1, keepdims=True)
    i1 = jnp.min(jnp.where(le == v1, lane, big), axis=-1, keepdims=True)
    le2 = jnp.where(lane == i1, ninf, le)
    v2 = jnp.max(le2, axis=-1, keepdims=True)
    i2 = jnp.min(jnp.where(le2 == v2, lane, big), axis=-1, keepdims=True)
    tt = jnp.exp(v2 - v1)
    g1 = p_top / (1.0 + tt)
    g2 = p_top * tt / (1.0 + tt)
    e1, e2 = i1 - MOE_GROUPS, i2 - MOE_GROUPS
    tm = h.shape[0]
    tri = (lax.broadcasted_iota(jnp.int32, (tm, tm), 1)
           < lax.broadcasted_iota(jnp.int32, (tm, tm), 0)).astype(BF16)
    hot1, hot2 = lane == e1, lane == e2
    before1 = jnp.dot(tri, hot1.astype(BF16), preferred_element_type=F32)
    before2 = jnp.dot(tri, hot2.astype(BF16), preferred_element_type=F32)
    tot1 = jnp.sum(hot1.astype(F32), axis=0, keepdims=True)
    tot2 = jnp.sum(hot2.astype(F32), axis=0, keepdims=True)
    carry = carry_ref[0:1, :]
    rank1 = jnp.sum(jnp.where(hot1, carry + before1, 0.0), axis=-1, keepdims=True)
    rank2 = jnp.sum(jnp.where(hot2, carry + tot1 + before2, 0.0), axis=-1, keepdims=True)
    carry = carry + tot1 + tot2
    carry_ref[0:1, :] = carry
    cnt_ref[...] = jnp.broadcast_to(carry, cnt_ref.shape)
    out = jnp.where(lane == 0, e1, 0.0)
    out = jnp.where(lane == 1, e2, out)
    out = jnp.where(lane == 2, g1, out)
    out = jnp.where(lane == 3, g2, out)
    out = jnp.where(lane == 4, rank1, out)
    out = jnp.where(lane == 5, rank2, out)
    r_ref[...] = out


def _route(xa, g, slab, wg, bg, we, be, *, tiles_per_batch):
    t, d = xa.shape
    tm = TOKEN_TILE
    per = d // V7X_LANES
    nl = MOE_GROUPS + MOE_EXPERTS
    wr = jnp.pad(jnp.concatenate([wg, we], axis=1), ((0, 0), (0, ROUTE_LANES - nl)))
    br = jnp.pad(jnp.concatenate([bg, be], axis=0), (0, ROUTE_LANES - nl)).reshape(1, ROUTE_LANES)
    last_row = slab.shape[0] - 1
    return pl.pallas_call(
        _route_kernel,
        grid=(t // tm,),
        in_specs=[pl.BlockSpec((tm, d), lambda i: (i, 0)),
                  pl.BlockSpec((1, d), lambda i: (0, 0)),
                  pl.BlockSpec((None, 8, d), lambda i: (jnp.minimum(i // tiles_per_batch, last_row), 0, 0)),
                  pl.BlockSpec((d, ROUTE_LANES), lambda i: (0, 0)),
                  pl.BlockSpec((1, ROUTE_LANES), lambda i: (0, 0))],
        out_specs=[pl.BlockSpec((tm * per, V7X_LANES), lambda i: (i, 0)),
                   pl.BlockSpec((tm, ROUTE_LANES), lambda i: (i, 0)),
                   pl.BlockSpec((8, ROUTE_LANES), lambda i: (0, 0))],
        out_shape=[jax.ShapeDtypeStruct((t * per, V7X_LANES), F32),
                   jax.ShapeDtypeStruct((t, ROUTE_LANES), F32),
                   jax.ShapeDtypeStruct((8, ROUTE_LANES), F32)],
        scratch_shapes=[pltpu.VMEM((8, ROUTE_LANES), F32)],
        compiler_params=_params(("arbitrary",)),
        name="moe_route",
    )(xa, g.reshape(1, d), slab, wr, br)


def _dispatch(route, counts):
    bm = MOE_BLOCK
    n_pairs = route.shape[0] * MOE_TOPK
    experts = route[:, 0:MOE_TOPK].astype(jnp.int32)
    rank = route[:, 4:4 + MOE_TOPK].astype(jnp.int32)
    cnt = counts[0, :MOE_EXPERTS].astype(jnp.int32)
    padded = (cnt + bm - 1) // bm * bm
    pad_end = jnp.cumsum(padded)
    dest = (pad_end - padded)[experts] + rank
    n_blocks = -(-n_pairs // bm) + MOE_EXPERTS
    first_slot = jnp.arange(n_blocks, dtype=jnp.int32) * bm
    block_e = jnp.minimum(jnp.sum((pad_end[None, :] <= first_slot[:, None]).astype(jnp.int32), axis=1),
                          MOE_EXPERTS - 1)
    n_used = (pad_end[-1] // bm).astype(jnp.int32).reshape(1)
    tail = jnp.maximum(pad_end - bm, 0).astype(jnp.int32)
    return dest, block_e, n_used, tail, n_blocks


def _scatter_kernel(tail_ref, nu_ref, d_ref, h_ref, xs_hbm, stage, zsrc, sem, *, n_steps, n_blocks, per):
    i = pl.program_id(0)
    tm = d_ref.shape[1] // MOE_TOPK
    zrows = zsrc.shape[0]

    def fill_at(slot):
        return pltpu.make_async_copy(zsrc, xs_hbm.at[pl.ds(slot * per, zrows), :], sem.at[2])

    def fill(e):
        return fill_at(tail_ref[e])

    def fill_unused(wait):
        def body(blk, c):
            cp = fill_at(blk * MOE_BLOCK)
            cp.wait() if wait else cp.start()
            return c
        lax.fori_loop(nu_ref[0], n_blocks, body, 0)

    def copy(step, r, k):
        return pltpu.make_async_copy(stage.at[lax.rem(step, 2), pl.ds(r * per, per), :],
                                     xs_hbm.at[pl.ds(d_ref[0, MOE_TOPK * r + k] * per, per), :],
                                     sem.at[lax.rem(step, 2)])

    def wait_step(step):
        for r in range(tm):
            for k in range(MOE_TOPK):
                copy(step, r, k).wait()

    @pl.when(i == 0)
    def _():
        zsrc[...] = jnp.zeros_like(zsrc)
        for e in range(MOE_EXPERTS):
            fill(e).start()
        fill_unused(False)
        for e in range(MOE_EXPERTS):
            fill(e).wait()
        fill_unused(True)

    stage[lax.rem(i, 2)] = h_ref[...]
    for r in range(tm):
        for k in range(MOE_TOPK):
            copy(i, r, k).start()

    @pl.when(i > 0)
    def _():
        wait_step(i - 1)

    @pl.when(i == n_steps - 1)
    def _():
        wait_step(i)


def _moe_scatter(h_rows, dest, tail, n_used, n_blocks, per):
    t = dest.shape[0]
    tm = TOKEN_TILE // 2
    n_steps = t // tm
    bm = MOE_BLOCK
    d3 = dest.reshape(n_steps, 1, MOE_TOPK * tm)
    grid_spec = pltpu.PrefetchScalarGridSpec(
        num_scalar_prefetch=2,
        grid=(n_steps,),
        in_specs=[pl.BlockSpec((None, 1, MOE_TOPK * tm), lambda i, tl, nu: (i, 0, 0),
                               memory_space=pltpu.SMEM),
                  pl.BlockSpec((tm * per, V7X_LANES), lambda i, tl, nu: (i, 0))],
        out_specs=pl.BlockSpec(memory_space=pl.ANY),
        scratch_shapes=[pltpu.VMEM((2, tm * per, V7X_LANES), F32),
                        pltpu.VMEM((bm * per, V7X_LANES), F32),
                        pltpu.SemaphoreType.DMA((3,))],
    )
    return pl.pallas_call(
        functools.partial(_scatter_kernel, n_steps=n_steps, n_blocks=n_blocks, per=per),
        grid_spec=grid_spec,
        out_shape=jax.ShapeDtypeStruct((n_blocks * bm * per, V7X_LANES), F32),
        compiler_params=_params(("arbitrary",)),
        name="moe_scatter",
    )(tail, n_used, d3, h_rows)


def _token_copy(src_hbm, tok, dst, slot, r, sem, per):
    return pltpu.make_async_copy(src_hbm.at[pl.ds(tok * per, per), :],
                                 dst.at[slot, pl.ds(r * per, per), :], sem.at[slot])


def _moe_kernel(be_ref, nu_ref, xs_ref, w1_ref, w3_ref, w2_ref, o_ref, w1b, w3b, w2b, prev_ref):
    b = pl.program_id(0)
    bm = MOE_BLOCK
    per = w1_ref.shape[0] // V7X_LANES

    @pl.when(b == 0)
    def _():
        prev_ref[0] = -1

    @pl.when(b < nu_ref[0])
    def _():
        e = be_ref[b]

        @pl.when(e != prev_ref[0])
        def _():
            w1b[...] = w1_ref[...].astype(BF16)
            w3b[...] = w3_ref[...].astype(BF16)
            w2b[...] = w2_ref[...].astype(BF16)
            prev_ref[0] = e

        x = jnp.concatenate(_load_token_rows(xs_ref, 0, bm, per), axis=1).astype(BF16)
        h1 = jnp.dot(x, w1b[...], preferred_element_type=F32)
        h3 = jnp.dot(x, w3b[...], preferred_element_type=F32)
        act = (_silu(h1) * h3).astype(BF16)
        _store_token_rows(o_ref, jnp.dot(act, w2b[...], preferred_element_type=F32))

    @pl.when(b >= nu_ref[0])
    def _():
        o_ref[...] = jnp.zeros_like(o_ref)


def _moe_experts(xs_rows, block_e, n_used, n_blocks, w1, w3, w2):
    d, ff = w1.shape[1], w1.shape[2]
    per = d // V7X_LANES
    bm = MOE_BLOCK
    blk = lambda b, be, nu: jnp.minimum(b, jnp.maximum(nu[0] - 1, 0))
    grid_spec = pltpu.PrefetchScalarGridSpec(
        num_scalar_prefetch=2,
        grid=(n_blocks,),
        in_specs=[pl.BlockSpec((bm * per, V7X_LANES), lambda b, be, nu: (blk(b, be, nu), 0)),
                  pl.BlockSpec((None, d, ff), lambda b, be, nu: (be[blk(b, be, nu)], 0, 0)),
                  pl.BlockSpec((None, d, ff), lambda b, be, nu: (be[blk(b, be, nu)], 0, 0)),
                  pl.BlockSpec((None, ff, d), lambda b, be, nu: (be[blk(b, be, nu)], 0, 0))],
        out_specs=pl.BlockSpec((bm * per, V7X_LANES), lambda b, be, nu: (b, 0)),
        scratch_shapes=[pltpu.VMEM((d, ff), BF16),
                        pltpu.VMEM((d, ff), BF16),
                        pltpu.VMEM((ff, d), BF16),
                        pltpu.SMEM((1,), jnp.int32)],
    )
    return pl.pallas_call(
        _moe_kernel,
        grid_spec=grid_spec,
        out_shape=jax.ShapeDtypeStruct((n_blocks * bm * per, V7X_LANES), F32),
        compiler_params=_params(("arbitrary",)),
        name="moe_experts",
    )(block_e, n_used, xs_rows, w1, w3, w2)


def _combine_kernel(d_ref, dn_ref, yb_hbm, x_ref, r_ref, m_ref, *rest, n_tiles, n_lat_tiles):
    if n_lat_tiles < n_tiles:
        lat_ref, ctx_ref, ybuf, sem = rest
    else:
        lat_ref, ybuf, sem = rest
        ctx_ref = None
    i = pl.program_id(0)
    tm, d = x_ref.shape
    per = d // V7X_LANES
    nrow = MOE_TOPK * tm
    slot = lax.rem(i, 2)

    def start_rows(idx_ref, s):
        for r in range(nrow):
            _token_copy(yb_hbm, idx_ref[0, r], ybuf, s, r, sem, per).start()

    @pl.when(i == 0)
    def _():
        start_rows(d_ref, 0)

    @pl.when(i + 1 < n_tiles)
    def _():
        start_rows(dn_ref, 1 - slot)

    for r in range(nrow):
        _token_copy(yb_hbm, 0, ybuf, slot, r, sem, per).wait()
    r = r_ref[...]
    g0, g1 = r[:, 2:3], r[:, 3:4]
    y0 = _load_token_rows(ybuf.at[slot], 0, tm, per)
    y1 = _load_token_rows(ybuf.at[slot], tm, tm, per)
    def emit(o_ref):
        for c in range(per):
            sl = slice(c * V7X_LANES, (c + 1) * V7X_LANES)
            o_ref[pl.ds(c, tm, stride=per), :] = (
                x_ref[:, sl] + m_ref[5:6, sl] * (g0 * y0[c] + g1 * y1[c]))

    if ctx_ref is None:
        emit(lat_ref)
    else:
        pl.when(i < n_lat_tiles)(lambda: emit(lat_ref))
        pl.when(i >= n_lat_tiles)(lambda: emit(ctx_ref))


def _moe_combine(yb_rows, dest, xa, route, slab, *, tiles_per_batch, t_lat):
    t, d = xa.shape
    per = d // V7X_LANES
    tm = TOKEN_TILE // 2
    n_tiles = t // tm
    n_lat_tiles = t_lat // tm
    row_block = pl.BlockSpec((tm * per, V7X_LANES), lambda i: (jnp.minimum(i, n_lat_tiles - 1), 0))
    out_specs = [row_block]
    out_shape = [jax.ShapeDtypeStruct((t_lat * per, V7X_LANES), F32)]
    if n_lat_tiles < n_tiles:
        out_specs.append(pl.BlockSpec((tm * per, V7X_LANES),
                                      lambda i: (jnp.maximum(i - n_lat_tiles, 0), 0)))
        out_shape.append(jax.ShapeDtypeStruct(((t - t_lat) * per, V7X_LANES), F32))
    tpb = tiles_per_batch * (TOKEN_TILE // tm)
    last_row = slab.shape[0] - 1
    d3 = dest.reshape(n_tiles, tm, MOE_TOPK).transpose(0, 2, 1).reshape(n_tiles, 1, MOE_TOPK * tm)
    smem_block = lambda fn: pl.BlockSpec((None, 1, MOE_TOPK * tm), fn, memory_space=pltpu.SMEM)
    return pl.pallas_call(
        functools.partial(_combine_kernel, n_tiles=n_tiles, n_lat_tiles=n_lat_tiles),
        grid=(n_tiles,),
        in_specs=[smem_block(lambda i: (i, 0, 0)),
                  smem_block(lambda i: (jnp.minimum(i + 1, n_tiles - 1), 0, 0)),
                  pl.BlockSpec(memory_space=pl.ANY),
                  pl.BlockSpec((tm, d), lambda i: (i, 0)),
                  pl.BlockSpec((tm, ROUTE_LANES), lambda i: (i, 0)),
                  pl.BlockSpec((None, 8, d), lambda i: (jnp.minimum(i // tpb, last_row), 0, 0))],
        out_specs=out_specs,
        out_shape=out_shape,
        scratch_shapes=[pltpu.VMEM((2, MOE_TOPK * tm * per, V7X_LANES), F32),
                        pltpu.SemaphoreType.DMA((2,))],
        compiler_params=_params(("arbitrary",)),
        name="moe_combine",
    )(d3, d3, yb_rows, xa, route, slab)


def _moe_layer(xa, g, slab, wg, bg, we, be, w1, w3, w2, *, tiles_per_batch, t_lat):
    per = xa.shape[1] // V7X_LANES
    h_rows, route, counts = _route(xa, g, slab, wg, bg, we, be, tiles_per_batch=tiles_per_batch)
    dest, block_e, n_used, tail, n_blocks = _dispatch(route, counts)
    xs_rows = _moe_scatter(h_rows, dest, tail, n_used, n_blocks, per)
    yb_rows = _moe_experts(xs_rows, block_e, n_used, n_blocks, w1, w3, w2)
    return _moe_combine(yb_rows, dest, xa, route, slab, tiles_per_batch=tiles_per_batch, t_lat=t_lat)


def _ssd_kernel(*refs, rev, n_ctx_chunks, n_lat_chunks):
    if rev:
        (xact_ref, bact_ref, cact_ref, dt_ref, dtb_ref, a_ref, yf_ref, z_ref, ng_ref,
         o_ref, xpair, ypair, cbs, bts, cs, acst, dtt, wtt, tott, state) = refs
    else:
        (xs_ref, xsp_ref, xsn_ref, bm_ref, bmp_ref, bmn_ref, cm_ref, cmp_ref, cmn_ref,
         dt_ref, cwx_ref, cbx_ref, cwb_ref, cbb_ref, cwc_ref, cbc_ref, dtb_ref, a_ref, dsk_ref,
         o_ref, xact_ref, bact_ref, cact_ref,
         padx, padb, padc, xpair, ypair, cbs, bts, cs, acst, dtt, wtt, tott, state) = refs
    k = pl.program_id(1)
    q = SSD_CHUNK
    nst = SSD_STATE
    npair = xact_ref.shape[1] // V7X_LANES
    ngrp = bact_ref.shape[1] // nst
    pairs_per_group = npair // ngrp
    if rev:
        bmat, cmat = bact_ref[...], cact_ref[...]
        for p in range(npair):
            xpair[p] = xact_ref[:, p * V7X_LANES:(p + 1) * V7X_LANES]
    else:
        in_ctx = k < n_ctx_chunks
        jc, jl = k, k - n_ctx_chunks
        first = jnp.where(in_ctx, jc == 0, jl == 0)
        last = jnp.where(in_ctx, jc == n_ctx_chunks - 1, jl == n_lat_chunks - 1)

        def conv_silu(pad_ref, main, prev, nxt, cw_ref, cb_ref):
            return _silu(_dwconv_tile(pad_ref, main[...], prev[...], nxt[...], first, last, cw_ref,
                                      cb_ref[...], q))

        xs = conv_silu(padx, xs_ref, xsp_ref, xsn_ref, cwx_ref, cbx_ref)
        bmat = conv_silu(padb, bm_ref, bmp_ref, bmn_ref, cwb_ref, cbb_ref)
        cmat = conv_silu(padc, cm_ref, cmp_ref, cmn_ref, cwc_ref, cbc_ref)
        bact_ref[...] = bmat
        cact_ref[...] = cmat
        for p in range(npair):
            xb = xs[:, p * V7X_LANES:(p + 1) * V7X_LANES].astype(BF16)
            xpair[p] = xb
            xact_ref[:, p * V7X_LANES:(p + 1) * V7X_LANES] = xb
    for g in range(ngrp):
        bg = bmat[:, g * nst:(g + 1) * nst]
        cg = cmat[:, g * nst:(g + 1) * nst].astype(BF16)
        cs[g] = cg
        cbs[g] = lax.dot_general(cg, bg.astype(BF16), (((1,), (1,)), ((), ())),
                                 preferred_element_type=F32)
        bts[g] = bg.T
    dtr = dt_ref[...] + dtb_ref[...]
    dt = jnp.maximum(dtr, 0.0) + jnp.log(1.0 + jnp.exp(-jnp.abs(dtr)))
    adt = dt * a_ref[...]
    row = lax.broadcasted_iota(jnp.int32, (q, q), 0)
    col = lax.broadcasted_iota(jnp.int32, (q, q), 1)
    causal = (row <= col) if rev else (row >= col)
    acs = jnp.dot(causal.astype(F32), adt, preferred_element_type=F32, precision=HIGHEST)
    acs_t = acs.T
    tot = acs_t[:, 0:1] if rev else acs_t[:, q - 1:q]
    acst[...] = acs_t
    dtt[...] = dt.T
    wtt[...] = jnp.exp(tot - acs_t) * dt.T
    tott[...] = jnp.broadcast_to(jnp.exp(tot), (tott.shape[0], q))

    @pl.when(k == 0)
    def _():
        state[...] = jnp.zeros_like(state)

    hoff = SSD_HEAD_DIM if rev else 0
    lane = lax.broadcasted_iota(jnp.int32, (q, V7X_LANES), 1)
    left = lane < SSD_HEAD_DIM

    def pair_body(p, c):
        g = p // pairs_per_group
        h0 = hoff + 2 * p
        x = xpair[p]
        zero = jnp.zeros_like(x)
        xbd = jnp.concatenate([jnp.where(left, x, zero), jnp.where(left, zero, x)], axis=0)
        cb = cbs[g]
        st = state[p]
        mms, wbs, scales, decs = [], [], [], []
        for m in range(2):
            arow = acst[pl.ds(h0 + m, 1), :]
            acol = jnp.broadcast_to(arow, (q, q)).T
            seg = jnp.where(causal, jnp.exp(acol - arow), 0.0) * dtt[pl.ds(h0 + m, 1), :]
            mms.append((cb * seg).astype(BF16))
            scales.append(jnp.exp(acol))
            wbs.append((bts[g] * wtt[pl.ds(h0 + m, 1), :]).astype(BF16))
            decs.append(tott[pl.ds(h0 + m, 1), :])
        intra = jnp.dot(jnp.concatenate(mms, axis=1), xbd, preferred_element_type=F32)
        new = jnp.dot(jnp.concatenate(wbs, axis=1), xbd, preferred_element_type=F32)
        inter = jnp.dot(cs[g], st.astype(BF16), preferred_element_type=F32)
        ypair[p] = intra + inter * jnp.where(left, scales[0], scales[1])
        state[p] = st * jnp.where(left, decs[0], decs[1]) + new
        return c

    lax.fori_loop(0, npair, pair_body, 0, unroll=SSD_PAIR_UNROLL)
    if not rev:
        for p in range(npair):
            sl = slice(p * V7X_LANES, (p + 1) * V7X_LANES)
            o_ref[:, sl] = ypair[p] + dsk_ref[:, sl] * xs[:, sl]
        return
    sq = jnp.zeros((q, V7X_LANES), F32)
    for p in range(npair):
        sl = slice(p * V7X_LANES, (p + 1) * V7X_LANES)
        gated = (ypair[p] + yf_ref[:, sl]) * _silu(z_ref[:, sl])
        ypair[p] = gated
        sq = sq + gated * gated
    scale = lax.rsqrt(jnp.sum(sq, axis=-1, keepdims=True) / (npair * V7X_LANES) + RMS_EPS)
    for p in range(npair):
        sl = slice(p * V7X_LANES, (p + 1) * V7X_LANES)
        o_ref[:, sl] = (ypair[p] * scale * ng_ref[:, sl]).astype(BF16)


def _ssd_scan(hz, conv_w, conv_b, dt_bias, a_neg, d_skip_lanes, *, rev, nb, n_lat, n_ctx, inner,
              fwd=None, norm_g=None):
    t = hz.shape[0]
    q = SSD_CHUNK
    bc = SSD_GROUPS * SSD_STATE
    ncc, ncl = n_ctx // q, n_lat // q
    hb = q // HALO
    n_halo_blocks = t // HALO
    nheads2 = dt_bias.size
    assert nheads2 == V7X_LANES and q == V7X_LANES and SSD_STATE == V7X_LANES
    npair = inner // V7X_LANES
    xs_col = inner // inner
    b_col = 2 * inner // bc
    c_col = b_col + 1
    dt_col = (2 * inner + 2 * bc) // nheads2

    def chunk(b, k):
        kk = k - ncc
        if rev:
            jc, jl = ncc - 1 - k, ncl - 1 - kk
        else:
            jc, jl = k, kk
        return jnp.where(k < ncc, nb * ncl + b * ncc + jc, b * ncl + jl)

    def main(width, colb):
        return pl.BlockSpec((q, width), lambda b, k: (chunk(b, k), colb))

    def prev(width, colb):
        return pl.BlockSpec((HALO, width), lambda b, k: (jnp.maximum(chunk(b, k) * hb - 1, 0), colb))

    def nxt(width, colb):
        return pl.BlockSpec((HALO, width),
                            lambda b, k: (jnp.minimum(chunk(b, k) * hb + hb, n_halo_blocks - 1), colb))

    const = lambda shape: pl.BlockSpec(shape, lambda b, k: (0, 0))
    shared_scratch = [pltpu.VMEM((npair, q, V7X_LANES), BF16),
                      pltpu.VMEM((npair, q, V7X_LANES), F32),
                      pltpu.VMEM((SSD_GROUPS, q, q), F32),
                      pltpu.VMEM((SSD_GROUPS, SSD_STATE, q), F32),
                      pltpu.VMEM((SSD_GROUPS, q, SSD_STATE), BF16),
                      pltpu.VMEM((nheads2, q), F32),
                      pltpu.VMEM((nheads2, q), F32),
                      pltpu.VMEM((nheads2, q), F32),
                      pltpu.VMEM((nheads2, q), F32),
                      pltpu.VMEM((npair, SSD_STATE, V7X_LANES), F32)]
    kern = functools.partial(_ssd_kernel, rev=rev, n_ctx_chunks=ncc, n_lat_chunks=ncl)
    dtb, an = dt_bias.reshape(1, nheads2), a_neg.reshape(1, nheads2)
    if rev:
        y_fwd, xact, bact, cact = fwd
        return pl.pallas_call(
            kern,
            grid=(nb, ncc + ncl),
            in_specs=[main(inner, 0), main(bc, 0), main(bc, 0), main(nheads2, dt_col),
                      const((1, nheads2)), const((1, nheads2)),
                      main(inner, 0), main(inner, 0), const((1, inner))],
            out_specs=main(inner, 0),
            out_shape=jax.ShapeDtypeStruct((t, inner), BF16),
            scratch_shapes=shared_scratch,
            compiler_params=_params(("arbitrary", "arbitrary")),
            name="ssd_scan_bwd",
        )(xact, bact, cact, hz, dtb, an, y_fwd, hz, norm_g.reshape(1, inner))
    cw = jnp.pad(conv_w, ((0, 8 - CONV_W), (0, 0)))
    cb = conv_b.reshape(1, -1)
    return pl.pallas_call(
        kern,
        grid=(nb, ncc + ncl),
        in_specs=[main(inner, xs_col), prev(inner, xs_col), nxt(inner, xs_col),
                  main(bc, b_col), prev(bc, b_col), nxt(bc, b_col),
                  main(bc, c_col), prev(bc, c_col), nxt(bc, c_col),
                  main(nheads2, dt_col),
                  const((8, inner)), const((1, inner)),
                  const((8, bc)), const((1, bc)),
                  const((8, bc)), const((1, bc)),
                  const((1, nheads2)), const((1, nheads2)), const((1, inner))],
        out_specs=[main(inner, 0), main(inner, 0), main(bc, 0), main(bc, 0)],
        out_shape=[jax.ShapeDtypeStruct((t, inner), F32), jax.ShapeDtypeStruct((t, inner), BF16),
                   jax.ShapeDtypeStruct((t, bc), F32), jax.ShapeDtypeStruct((t, bc), F32)],
        scratch_shapes=[pltpu.VMEM((q + 2 * HALO, inner), F32),
                        pltpu.VMEM((q + 2 * HALO, bc), F32),
                        pltpu.VMEM((q + 2 * HALO, bc), F32)] + shared_scratch,
        compiler_params=_params(("arbitrary", "arbitrary")),
        name="ssd_scan_fwd",
    )(hz, hz, hz, hz, hz, hz, hz, hz, hz, hz,
      cw[:, :inner], cb[:, :inner], cw[:, inner:inner + bc], cb[:, inner:inner + bc],
      cw[:, inner + bc:], cb[:, inner + bc:], dtb, an, d_skip_lanes)


def _out1_kernel(yn_ref, x_ref, m_ref, w_ref, o_ref):
    o_ref[...] = x_ref[...] + m_ref[2:3, :] * jnp.dot(yn_ref[...], w_ref[...], preferred_element_type=F32)


def _out1(yn, xc, slab, w_out, *, t_lat, tiles_per_batch):
    t, d = t_lat, xc.shape[1]
    inner = yn.shape[1]
    tm = TOKEN_TILE // 2
    tpb = tiles_per_batch * (TOKEN_TILE // tm)
    last_row = slab.shape[0] - 1
    return pl.pallas_call(
        _out1_kernel,
        grid=(t // tm,),
        in_specs=[pl.BlockSpec((tm, inner), lambda i: (i, 0)),
                  pl.BlockSpec((tm, d), lambda i: (i, 0)),
                  pl.BlockSpec((None, 8, d), lambda i: (jnp.minimum(i // tpb, last_row), 0, 0)),
                  pl.BlockSpec((inner, d), lambda i: (0, 0))],
        out_specs=pl.BlockSpec((tm, d), lambda i: (i, 0)),
        out_shape=jax.ShapeDtypeStruct((t, d), F32),
        compiler_params=_params(("arbitrary",)),
        name="ssd_out",
    )(yn, xc, slab, w_out.astype(BF16))


def _colmajor_kernel(lat_ref, ctx_ref, o_ref, *, n_lat_steps):
    i = pl.program_id(0)
    n_tok = o_ref.shape[0]
    per = o_ref.shape[1] // V7X_LANES

    def emit(src2d):
        for c, piece in enumerate(_load_token_rows(src2d, 0, n_tok, per)):
            o_ref[:, c * V7X_LANES:(c + 1) * V7X_LANES] = piece

    pl.when(i < n_lat_steps)(lambda: emit(lat_ref.reshape(n_tok * per, V7X_LANES)))
    pl.when(i >= n_lat_steps)(lambda: emit(ctx_ref))


def _to_column_major(lat_rows, ctx_rows, *, nb, rows, d):
    per = d // V7X_LANES
    n_lat_steps = nb * GRID_W
    n_ctx_steps = ctx_rows.shape[0] // (rows * per)
    lat4 = lat_rows.reshape(nb, rows, GRID_W * per, V7X_LANES)

    def lat_index(i):
        j = jnp.minimum(i, n_lat_steps - 1)
        return (j // GRID_W, 0, j % GRID_W, 0)

    return pl.pallas_call(
        functools.partial(_colmajor_kernel, n_lat_steps=n_lat_steps),
        grid=(n_lat_steps + n_ctx_steps,),
        in_specs=[pl.BlockSpec((None, rows, per, V7X_LANES), lat_index),
                  pl.BlockSpec((rows * per, V7X_LANES), lambda i: (jnp.maximum(i - n_lat_steps, 0), 0))],
        out_specs=pl.BlockSpec((rows, d), lambda i: (i, 0)),
        out_shape=jax.ShapeDtypeStruct(((n_lat_steps + n_ctx_steps) * rows, d), F32),
        compiler_params=_params(("arbitrary",)),
        name="to_column_major",
    )(lat4, ctx_rows)


def _final_kernel(x_ref, g_ref, o_ref):
    n_tok, d = o_ref.shape
    per = d // V7X_LANES
    pieces = _load_token_rows(x_ref.reshape(n_tok * per, V7X_LANES), 0, n_tok, per)
    ss = pieces[0] * pieces[0]
    for p in pieces[1:]:
        ss = ss + p * p
    scale = lax.rsqrt(jnp.sum(ss, axis=-1, keepdims=True) / d + RMS_EPS)
    for c, p in enumerate(pieces):
        sl = slice(c * V7X_LANES, (c + 1) * V7X_LANES)
        o_ref[:, sl] = p * scale * g_ref[:, sl]


def _final_norm_row_major(xl_rows, g, *, nb, rows, d):
    per = d // V7X_LANES
    x4 = xl_rows.reshape(nb, GRID_W, rows * per, V7X_LANES)
    return pl.pallas_call(
        _final_kernel,
        grid=(nb, rows),
        in_specs=[pl.BlockSpec((None, GRID_W, per, V7X_LANES), lambda b, r: (b, 0, r, 0)),
                  pl.BlockSpec((1, d), lambda b, r: (0, 0))],
        out_specs=pl.BlockSpec((GRID_W, d), lambda b, r: (b * rows + r, 0)),
        out_shape=jax.ShapeDtypeStruct((nb * rows * GRID_W, d), F32),
        compiler_params=_params(("arbitrary", "arbitrary")),
        name="final_norm",
    )(x4, g.reshape(1, d))


def kernel(x, c, ctx, c_ctx, norm_mix_g, norm_ffn_g, mod_w, mod_b, hy_w_in, hy_w_out, s5_lam_re, s5_lam_im, s5_log_dt, s5_b_re, s5_b_im, s5_c_re, s5_c_im, s5_d, s5_glu_w, s5_glu_b, lru_conv_w, lru_conv_b, lru_wa, lru_ba, lru_wx, lru_bx, lru_lam, ssd_w_in, ssd_conv_w, ssd_conv_b, ssd_dt_bias, ssd_a_log, ssd_d, ssd_norm_g, ssd_w_out, moe_wg, moe_bg, moe_we, moe_be, moe_w1, moe_w3, moe_w2, final_norm_g):
    nb, n_lat, d = x.shape
    n_ctx = ctx.shape[1]
    depth = mod_w.shape[0]
    assert depth == 2 and nb + 1 <= 8
    assert n_lat % TOKEN_TILE == 0 and (nb * n_ctx) % TOKEN_TILE == 0
    t_lat = nb * n_lat
    rows = n_lat // GRID_W
    tiles_per_batch = n_lat // TOKEN_TILE
    n_cond = nb + 1

    xa = jnp.concatenate([x.reshape(t_lat, d), ctx.reshape(nb * n_ctx, d)], axis=0)
    cvec = jnp.zeros((8, d), F32).at[:nb].set(c).at[nb].set(c_ctx)
    mods = _mod_vectors(cvec, mod_w, mod_b)
    slab0 = _mod_slab(mods[0], n_cond, d)
    slab1 = _mod_slab(mods[1], n_cond, d)

    width = s5_d.shape[1]
    hz = _norm_matmul(xa, norm_mix_g[0], slab0, hy_w_in[0].astype(BF16), shift_row=0, scale_row=1,
                      tn=width, tiles_per_batch=tiles_per_batch)
    s5w = _s5_weights(s5_lam_re[0], s5_lam_im[0], s5_log_dt[0], s5_b_re[0], s5_b_im[0],
                      s5_c_re[0], s5_c_im[0])
    ys = _s5_mixer(hz, s5w, width=width, nb=nb, n_lat=n_lat, n_ctx=n_ctx)
    sp = jax.nn.softplus(-lru_lam[0].astype(F32)).reshape(2, 1, width)
    hfb = _lru_scan(hz, 2, lru_conv_w[0], lru_conv_b[0], _block_diag_gates(lru_wa[0]),
                    _block_diag_gates(lru_wx[0]), lru_ba[0].reshape(2, 1, width),
                    lru_bx[0].reshape(2, 1, width), sp, nb=nb, n_lat=n_lat, n_ctx=n_ctx)
    xa = _out0(ys, hz, hfb, xa, slab0, s5_d[0], s5_glu_w[0], s5_glu_b[0], hy_w_out[0],
               tiles_per_batch=tiles_per_batch)
    lat_rows, ctx_rows = _moe_layer(xa, norm_ffn_g[0], slab0, moe_wg[0], moe_bg[0], moe_we[0],
                                    moe_be[0], moe_w1[0], moe_w3[0], moe_w2[0],
                                    tiles_per_batch=tiles_per_batch, t_lat=t_lat)

    xc = _to_column_major(lat_rows, ctx_rows, nb=nb, rows=rows, d=d)
    inner = ssd_norm_g.shape[1]
    hz1 = _norm_matmul(xc, norm_mix_g[1], slab1, ssd_w_in[0].astype(BF16), shift_row=0, scale_row=1,
                       tn=ssd_w_in.shape[2] // 9, tiles_per_batch=tiles_per_batch)
    a_neg = -jnp.exp(ssd_a_log[0].astype(F32))
    dsk = jnp.repeat(ssd_d[0], SSD_HEAD_DIM).reshape(1, inner)
    ssd_args = dict(nb=nb, n_lat=n_lat, n_ctx=n_ctx, inner=inner)
    fwd = _ssd_scan(hz1, ssd_conv_w[0], ssd_conv_b[0], ssd_dt_bias[0], a_neg, dsk, rev=False, **ssd_args)
    yn = _ssd_scan(hz1, ssd_conv_w[0], ssd_conv_b[0], ssd_dt_bias[0], a_neg, dsk, rev=True,
                   fwd=fwd, norm_g=ssd_norm_g[0], **ssd_args)
    xl = _out1(yn, xc, slab1, ssd_w_out[0], t_lat=t_lat, tiles_per_batch=tiles_per_batch)
    (xl_rows,) = _moe_layer(xl, norm_ffn_g[1], slab1, moe_wg[1], moe_bg[1], moe_we[1], moe_be[1],
                            moe_w1[1], moe_w3[1], moe_w2[1], tiles_per_batch=tiles_per_batch,
                            t_lat=t_lat)
    out = _final_norm_row_major(xl_rows, final_norm_g, nb=nb, rows=rows, d=d)
    return out.reshape(nb, n_lat, d)
```

```python
import functools
import math

import jax
import jax.numpy as jnp
from jax import lax
from jax.experimental import pallas as pl
from jax.experimental.pallas import tpu as pltpu

F32 = jnp.float32
BF16 = jnp.bfloat16
HIGHEST = lax.Precision.HIGHEST

GRID_W = 64
N_MOD = 6
RMS_EPS = 1e-6
CONV_W = 4
CONV_PAD_LEFT = CONV_W // 2
S5_GROUP = 16
S5_STATE = 64
LRU_HEADS = 16
LRU_C = 8.0
SSD_HEAD_DIM = 64
SSD_GROUPS = 8
SSD_STATE = 128
SSD_CHUNK = 128
MOE_GROUPS = 4
MOE_PER_GROUP = 8
MOE_EXPERTS = MOE_GROUPS * MOE_PER_GROUP
MOE_TOPK = 2

V7X_LANES = 128
V7X_SUBLANES = 8
V7X_MXU_DIM = 256
V7X_VMEM_LIMIT_BYTES = 60000 * 1024

TOKEN_TILE = 512
S5_CHUNK = V7X_MXU_DIM // S5_GROUP
S5_PAIR_LANES = 2 * S5_STATE
S5_SCAN_LANES = 512
LRU_TILE = 256
LRU_GATE_BLOCK = V7X_MXU_DIM
HALO = V7X_SUBLANES
MOE_BLOCK = 256
SSD_PAIR_UNROLL = 4
FINAL_ROWS_PER_STEP = 4
ROUTE_LANES = V7X_LANES


def _params(sem):
    return pltpu.CompilerParams(dimension_semantics=sem, vmem_limit_bytes=V7X_VMEM_LIMIT_BYTES)


def _silu(v):
    return v * jax.nn.sigmoid(v)


def _gelu(v):
    return jax.nn.gelu(v, approximate=True)


def _mod_kernel(c_ref, w_ref, b_ref, o_ref):
    s = _silu(c_ref[...])
    o_ref[...] = jnp.dot(s, w_ref[...], preferred_element_type=F32, precision=HIGHEST) + b_ref[...]


def _mod_vectors(cvec, mod_w, mod_b):
    depth, d, n = mod_w.shape
    tn = n // 8
    return pl.pallas_call(
        _mod_kernel,
        grid=(depth, n // tn),
        in_specs=[pl.BlockSpec((8, d), lambda l, j: (0, 0)),
                  pl.BlockSpec((None, d, tn), lambda l, j: (l, 0, j)),
                  pl.BlockSpec((None, 1, tn), lambda l, j: (l, 0, j))],
        out_specs=pl.BlockSpec((None, 8, tn), lambda l, j: (l, 0, j)),
        out_shape=jax.ShapeDtypeStruct((depth, 8, n), F32),
        compiler_params=_params(("arbitrary", "arbitrary")),
        name="mod_vectors",
    )(cvec, mod_w, mod_b.reshape(depth, 1, n))


def _mod_slab(mods_layer, n_rows, d):
    m = mods_layer[:n_rows].reshape(n_rows, N_MOD, d)
    return jnp.pad(m, ((0, 0), (0, 8 - N_MOD), (0, 0)))


def _norm_mod(x, g, m_ref, shift_row, scale_row):
    ms = jnp.mean(x * x, axis=-1, keepdims=True)
    y = x * lax.rsqrt(ms + RMS_EPS) * g
    return y * (1.0 + m_ref[scale_row:scale_row + 1, :]) + m_ref[shift_row:shift_row + 1, :]


def _norm_mm_kernel(x_ref, g_ref, m_ref, w_ref, o_ref, xn_ref, *, shift_row, scale_row):
    @pl.when(pl.program_id(1) == 0)
    def _():
        xn_ref[...] = _norm_mod(x_ref[...], g_ref[...], m_ref, shift_row, scale_row).astype(BF16)

    o_ref[...] = jnp.dot(xn_ref[...], w_ref[...], preferred_element_type=F32)


def _norm_matmul(xa, g, slab, w, *, shift_row, scale_row, tn, tiles_per_batch):
    t, d = xa.shape
    n = w.shape[1]
    tm = TOKEN_TILE
    last_row = slab.shape[0] - 1
    return pl.pallas_call(
        functools.partial(_norm_mm_kernel, shift_row=shift_row, scale_row=scale_row),
        grid=(t // tm, n // tn),
        in_specs=[pl.BlockSpec((tm, d), lambda i, j: (i, 0)),
                  pl.BlockSpec((1, d), lambda i, j: (0, 0)),
                  pl.BlockSpec((None, 8, d),
                               lambda i, j: (jnp.minimum(i // tiles_per_batch, last_row), 0, 0)),
                  pl.BlockSpec((d, tn), lambda i, j: (0, j))],
        out_specs=pl.BlockSpec((tm, tn), lambda i, j: (i, j)),
        out_shape=jax.ShapeDtypeStruct((t, n), F32),
        scratch_shapes=[pltpu.VMEM((tm, d), BF16)],
        compiler_params=_params(("arbitrary", "arbitrary")),
        name="norm_matmul",
    )(xa, g.reshape(1, d), slab, w)


def _s5_weights(lam_re, lam_im, log_dt, b_re, b_im, c_re, c_im):
    q = S5_CHUNK
    ngrp, nst = lam_re.shape[1], lam_re.shape[2]
    nch = b_re.shape[-1]
    lr, li = lam_re.astype(F32), lam_im.astype(F32)
    dt = jnp.exp(log_dt.astype(F32))[..., None]
    mag = jnp.exp(lr * dt)
    a_re, a_im = mag * jnp.cos(li * dt), mag * jnp.sin(li * dt)
    den = lr * lr + li * li
    k_re = ((a_re - 1) * lr + a_im * li) / den
    k_im = (a_im * lr - (a_re - 1) * li) / den
    bb_re = k_re[..., None] * b_re - k_im[..., None] * b_im
    bb_im = k_re[..., None] * b_im + k_im[..., None] * b_re
    ks = jnp.arange(q + 1, dtype=F32)[:, None, None, None]
    pmag = jnp.exp(ks * (lr * dt))
    pw_re, pw_im = pmag * jnp.cos(ks * (li * dt)), pmag * jnp.sin(ks * (li * dt))
    ab_re = pw_re[..., None] * bb_re - pw_im[..., None] * bb_im
    ab_im = pw_re[..., None] * bb_im + pw_im[..., None] * bb_re
    contract = lambda c, ab: jnp.sum(ab.transpose(1, 2, 0, 4, 3)[:, :, :, :, None, :]
                                     * c[:, :, None, None, :, :], axis=-1)
    kmat = contract(c_re, ab_re) - contract(c_im, ab_im)
    s_idx = jnp.arange(q)[:, None]
    t_idx = jnp.arange(q)[None, :]
    lag_f = t_idx - s_idx
    lag_b = s_idx - t_idx
    kf = jnp.where((lag_f >= 0)[None, :, :, None, None], kmat[0][:, jnp.clip(lag_f, 0, q - 1)], 0.0)
    kb = jnp.where((lag_b >= 0)[None, :, :, None, None], kmat[1][:, jnp.clip(lag_b, 0, q - 1)], 0.0)
    toep = (kf + kb).transpose(0, 1, 3, 2, 4).reshape(ngrp, q * nch, q * nch)
    sf_re = ab_re[q - 1 - jnp.arange(q), 0]
    sf_im = ab_im[q - 1 - jnp.arange(q), 0]
    sb_re = ab_re[jnp.arange(q), 1]
    sb_im = ab_im[jnp.arange(q), 1]
    summ = jnp.stack([sf_re, sf_im, sb_re, sb_im], 0)
    summ = summ.transpose(2, 1, 4, 0, 3).reshape(ngrp, q * nch, 4, nst)
    npair = ngrp // 2
    summ = summ.reshape(npair, 2, q * nch, 4, nst)
    zero = jnp.zeros_like(summ[:, 0])
    wa = jnp.concatenate([jnp.concatenate([summ[:, 0], zero], -1),
                          jnp.concatenate([zero, summ[:, 1]], -1)], 1)
    wa = wa.reshape(npair, 2 * q * nch, 4 * 2 * nst)
    cp_re = c_re[None] * pw_re[:, :, :, None, :] - c_im[None] * pw_im[:, :, :, None, :]
    cp_im = c_re[None] * pw_im[:, :, :, None, :] + c_im[None] * pw_re[:, :, :, None, :]
    tf = jnp.arange(q) + 1
    tb = q - jnp.arange(q)
    cr = jnp.stack([cp_re[tf, 0], -cp_im[tf, 0], cp_re[tb, 1], -cp_im[tb, 1]], 0)
    cr = cr.transpose(2, 0, 4, 1, 3).reshape(ngrp, 4, nst, q * nch)
    cr = cr.reshape(npair, 2, 4, nst, q * nch)
    zc = jnp.zeros_like(cr[:, 0])
    wc0 = jnp.concatenate([cr[:, 0], zc], 2)
    wc1 = jnp.concatenate([zc, cr[:, 1]], 2)
    wc = jnp.stack([wc0, wc1], 1).reshape(ngrp, 4 * 2 * nst, q * nch)
    a16 = jnp.stack([pw_re[q, 0], pw_im[q, 0], pw_re[q, 1], pw_im[q, 1]], 0)
    a16 = jnp.pad(a16.reshape(4, ngrp * nst), ((0, 4), (0, 0)))
    return toep.astype(BF16), wa.astype(BF16), wc.astype(BF16), a16


def _s5_row_chunk(nc):
    for r in (128, 96, 64, 48, 32, 16):
        if nc % r == 0:
            return r
    raise ValueError(f"unsupported chunk count {nc}")


def _s5_summary_kernel(u_ref, w_ref, ug_ref, o_ref, *, nc):
    q, nch = S5_CHUNK, S5_GROUP
    ngl = u_ref.shape[1] // nch
    rc = _s5_row_chunk(nc)

    def regroup(k, c):
        r0 = pl.multiple_of(k * rc, rc)
        steps = [u_ref[pl.ds(r0 * q + s, rc, stride=q), :] for s in range(q)]
        for gl in range(ngl):
            ug_ref[gl, pl.ds(r0, rc), :] = jnp.concatenate(
                [p[:, gl * nch:(gl + 1) * nch] for p in steps], axis=1).astype(BF16)
        return c

    lax.fori_loop(0, nc // rc, regroup, 0)
    for pr in range(ngl // 2):
        x = jnp.concatenate([ug_ref[2 * pr], ug_ref[2 * pr + 1]], axis=1)
        r = jnp.dot(x, w_ref[pr], preferred_element_type=F32)
        for qd in range(4):
            o_ref[qd, :, pr * S5_PAIR_LANES:(pr + 1) * S5_PAIR_LANES] = (
                r[:, qd * S5_PAIR_LANES:(qd + 1) * S5_PAIR_LANES])


def _s5_scan_kernel(s_ref, a_ref, h_ref, *, nb, n_lat, n_ctx):
    afr, afi = a_ref[0:1, :], a_ref[1:2, :]
    abr, abi = a_ref[2:3, :], a_ref[3:4, :]
    zero = jnp.zeros_like(afr)

    def step(rf, rb, carry):
        hfr, hfi, hbr, hbi = carry
        h_ref[0, pl.ds(rf, 1), :] = hfr
        h_ref[1, pl.ds(rf, 1), :] = hfi
        h_ref[2, pl.ds(rb, 1), :] = hbr
        h_ref[3, pl.ds(rb, 1), :] = hbi
        sfr, sfi = s_ref[0, pl.ds(rf, 1), :], s_ref[1, pl.ds(rf, 1), :]
        sbr, sbi = s_ref[2, pl.ds(rb, 1), :], s_ref[3, pl.ds(rb, 1), :]
        return (afr * hfr - afi * hfi + sfr, afr * hfi + afi * hfr + sfi,
                abr * hbr - abi * hbi + sbr, abr * hbi + abi * hbr + sbi)

    for b in range(nb):
        lat0 = b * n_lat
        ctx0 = nb * n_lat + b * n_ctx
        carry = lax.fori_loop(
            0, n_ctx, lambda i, c: step(ctx0 + i, ctx0 + n_ctx - 1 - i, c), (zero, zero, zero, zero))
        lax.fori_loop(0, n_lat, lambda i, c: step(lat0 + i, lat0 + n_lat - 1 - i, c), carry)


def _s5_output_kernel(ug_ref, t_ref, h_ref, w_ref, o_ref, y_ref, *, nc):
    q, nch = S5_CHUNK, S5_GROUP
    ngl = ug_ref.shape[0]
    rc = _s5_row_chunk(nc)
    for gl in range(ngl):
        pr = gl // 2
        hcat = jnp.concatenate(
            [h_ref[qd, :, pr * S5_PAIR_LANES:(pr + 1) * S5_PAIR_LANES] for qd in range(4)],
            axis=1).astype(BF16)
        y_ref[gl] = (jnp.dot(ug_ref[gl], t_ref[gl], preferred_element_type=F32)
                     + jnp.dot(hcat, w_ref[gl], preferred_element_type=F32))

    def ungroup(k, c):
        r0 = pl.multiple_of(k * rc, rc)
        ys = [y_ref[gl, pl.ds(r0, rc), :] for gl in range(ngl)]
        for s in range(q):
            o_ref[pl.ds(r0 * q + s, rc, stride=q), :] = jnp.concatenate(
                [y[:, s * nch:(s + 1) * nch] for y in ys], axis=1)
        return c

    lax.fori_loop(0, nc // rc, ungroup, 0)


def _s5_mixer(hz, weights, *, width, nb, n_lat, n_ctx):
    toep, wa, wc, a16 = weights
    t = hz.shape[0]
    q, nch = S5_CHUNK, S5_GROUP
    ngrp = width // nch
    ngl = V7X_LANES // nch
    nstrip = ngrp // ngl
    nc = t // q
    cw = q * nch
    lanes = ngrp * S5_STATE
    slanes = ngl * S5_STATE
    ug, summ = pl.pallas_call(
        functools.partial(_s5_summary_kernel, nc=nc),
        grid=(nstrip,),
        in_specs=[pl.BlockSpec((t, V7X_LANES), lambda j: (0, j)),
                  pl.BlockSpec((ngl // 2, 2 * cw, 4 * S5_PAIR_LANES), lambda j: (j, 0, 0))],
        out_specs=[pl.BlockSpec((ngl, nc, cw), lambda j: (j, 0, 0)),
                   pl.BlockSpec((4, nc, slanes), lambda j: (0, 0, j))],
        out_shape=[jax.ShapeDtypeStruct((ngrp, nc, cw), BF16),
                   jax.ShapeDtypeStruct((4, nc, lanes), F32)],
        compiler_params=_params(("arbitrary",)),
        name="s5_summary",
    )(hz, wa)
    wl = S5_SCAN_LANES
    carry = pl.pallas_call(
        functools.partial(_s5_scan_kernel, nb=nb, n_lat=n_lat // q, n_ctx=n_ctx // q),
        grid=(lanes // wl,),
        in_specs=[pl.BlockSpec((4, nc, wl), lambda j: (0, 0, j)),
                  pl.BlockSpec((8, wl), lambda j: (0, j))],
        out_specs=pl.BlockSpec((4, nc, wl), lambda j: (0, 0, j)),
        out_shape=jax.ShapeDtypeStruct((4, nc, lanes), F32),
        compiler_params=_params(("arbitrary",)),
        name="s5_scan",
    )(summ, a16)
    return pl.pallas_call(
        functools.partial(_s5_output_kernel, nc=nc),
        grid=(nstrip,),
        in_specs=[pl.BlockSpec((ngl, nc, cw), lambda j: (j, 0, 0)),
                  pl.BlockSpec((ngl, cw, cw), lambda j: (j, 0, 0)),
                  pl.BlockSpec((4, nc, slanes), lambda j: (0, 0, j)),
                  pl.BlockSpec((ngl, 4 * S5_PAIR_LANES, cw), lambda j: (j, 0, 0))],
        out_specs=pl.BlockSpec((t, V7X_LANES), lambda j: (0, j)),
        out_shape=jax.ShapeDtypeStruct((t, width), F32),
        scratch_shapes=[pltpu.VMEM((ngl, nc, cw), F32)],
        compiler_params=_params(("arbitrary",)),
        name="s5_output",
    )(ug, toep, carry, wc)


def _dwconv_tile(pad_ref, main, prev, nxt, first, last, cw_ref, cb, rows):
    pad_ref[0:HALO, :] = jnp.where(first, 0.0, prev)
    pad_ref[HALO:HALO + rows, :] = main
    pad_ref[HALO + rows:2 * HALO + rows, :] = jnp.where(last, 0.0, nxt)
    total = rows + 2 * HALO
    pieces = []
    for ct in range(pad_ref.shape[1] // V7X_LANES):
        sl = slice(ct * V7X_LANES, (ct + 1) * V7X_LANES)
        col = pad_ref[:, sl]
        acc = cb[:, sl]
        for k in range(CONV_W):
            shift = (CONV_PAD_LEFT - k) % total
            tap = col if shift == 0 else pltpu.roll(col, shift, axis=0)
            acc = acc + cw_ref[k:k + 1, sl] * tap[HALO:HALO + rows, :]
        pieces.append(acc)
    return jnp.concatenate(pieces, axis=1)


def _lru_kernel(v_ref, vp_ref, vn_ref, cw_ref, cb_ref, wa_ref, wx_ref, ba_ref, bx_ref, sp_ref,
                o_ref, pad_ref, a_ref, b_ref, h_ref, *, n_lat_tiles):
    d = pl.program_id(0)
    k = pl.program_id(2)
    tc = LRU_TILE
    j = jnp.where(d == 0, k - 1, n_lat_tiles - k)
    first = jnp.logical_or(k == 0, j == 0)
    last = jnp.logical_or(k == 0, j == n_lat_tiles - 1)
    vc = _dwconv_tile(pad_ref, v_ref[...], vp_ref[...], vn_ref[...], first, last, cw_ref,
                      cb_ref[...], tc)
    vcb = vc.astype(BF16)
    nblk = vc.shape[1] // LRU_GATE_BLOCK

    def gate(w_ref, bias):
        parts = [jnp.dot(vcb[:, m * LRU_GATE_BLOCK:(m + 1) * LRU_GATE_BLOCK], w_ref[m],
                         preferred_element_type=F32) for m in range(nblk)]
        return jax.nn.sigmoid(jnp.concatenate(parts, axis=1) + bias)

    r = gate(wa_ref, ba_ref[...])
    i = gate(wx_ref, bx_ref[...])
    a = jnp.exp(-LRU_C * r * sp_ref[...])
    a_ref[...] = a
    b_ref[...] = jnp.sqrt(1.0 - a * a) * (i * vc)

    @pl.when(k == 0)
    def _():
        h_ref[...] = jnp.zeros_like(h_ref)

    def body(t, h):
        tt = jnp.where(d == 0, t, tc - 1 - t)
        h = a_ref[pl.ds(tt, 1), :] * h + b_ref[pl.ds(tt, 1), :]
        o_ref[pl.ds(tt, 1), :] = h
        return h

    h_ref[0:1, :] = lax.fori_loop(0, tc, body, h_ref[0:1, :], unroll=8)


def _lru_scan(hz, col_block, conv_w, conv_b, wa, wx, ba, bx, sp, *, nb, n_lat, n_ctx):
    t = hz.shape[0]
    width = conv_w.shape[1]
    tc = LRU_TILE
    assert n_ctx == tc and n_lat % tc == 0
    nlt = n_lat // tc
    hb = tc // HALO
    n_halo_blocks = t // HALO

    def row_block(d, b, k):
        j = jnp.where(d == 0, k - 1, nlt - k)
        return jnp.where(k == 0, nb * nlt + b, b * nlt + j)

    cw = jnp.pad(conv_w, ((0, 8 - CONV_W), (0, 0)))
    nblk = width // LRU_GATE_BLOCK
    vec = lambda: pl.BlockSpec((None, 1, width), lambda d, b, k: (d, 0, 0))
    return pl.pallas_call(
        functools.partial(_lru_kernel, n_lat_tiles=nlt),
        grid=(2, nb, nlt + 1),
        in_specs=[pl.BlockSpec((tc, width), lambda d, b, k: (row_block(d, b, k), col_block)),
                  pl.BlockSpec((HALO, width),
                               lambda d, b, k: (jnp.maximum(row_block(d, b, k) * hb - 1, 0), col_block)),
                  pl.BlockSpec((HALO, width),
                               lambda d, b, k: (jnp.minimum(row_block(d, b, k) * hb + hb,
                                                            n_halo_blocks - 1), col_block)),
                  pl.BlockSpec((8, width), lambda d, b, k: (0, 0)),
                  pl.BlockSpec((1, width), lambda d, b, k: (0, 0)),
                  pl.BlockSpec((None, nblk, LRU_GATE_BLOCK, LRU_GATE_BLOCK), lambda d, b, k: (d, 0, 0, 0)),
                  pl.BlockSpec((None, nblk, LRU_GATE_BLOCK, LRU_GATE_BLOCK), lambda d, b, k: (d, 0, 0, 0)),
                  vec(), vec(), vec()],
        out_specs=pl.BlockSpec((None, tc, width), lambda d, b, k: (d, row_block(d, b, k), 0)),
        out_shape=jax.ShapeDtypeStruct((2, t, width), F32),
        scratch_shapes=[pltpu.VMEM((tc + 2 * HALO, width), F32),
                        pltpu.VMEM((tc, width), F32),
                        pltpu.VMEM((tc, width), F32),
                        pltpu.VMEM((8, width), F32)],
        compiler_params=_params(("arbitrary", "arbitrary", "arbitrary")),
        name="rglru_scan",
    )(hz, hz, hz, cw, conv_b.reshape(1, width), wa, wx, ba, bx, sp)


def _block_diag_gates(w):
    ndir, heads, hd, _ = w.shape
    per = LRU_GATE_BLOCK // hd
    w = w.reshape(ndir, heads // per, per, hd, hd)
    eye = jnp.eye(per, dtype=w.dtype)
    full = jnp.einsum('dmhij,hk->dmhikj', w, eye)
    return full.reshape(ndir, heads // per, per * hd, per * hd).astype(BF16)


def _out0_kernel(ys_ref, u_ref, g_ref, hf_ref, hb_ref, x_ref, m_ref, d_ref, gw_ref, gb_ref,
                 wt_ref, wb_ref, o_ref):
    y = ys_ref[...] + d_ref[...] * u_ref[...]
    z = _gelu(y)
    gate = jax.nn.sigmoid(jnp.dot(z.astype(BF16), gw_ref[...], preferred_element_type=F32) + gb_ref[...])
    a = z * gate
    r = _gelu(g_ref[...]) * (hf_ref[...] + hb_ref[...])
    dx = (jnp.dot(a.astype(BF16), wt_ref[...], preferred_element_type=F32)
          + jnp.dot(r.astype(BF16), wb_ref[...], preferred_element_type=F32))
    o_ref[...] = x_ref[...] + m_ref[2:3, :] * dx


def _out0(ys, hz, hfb, xa, slab, d_skip, glu_w, glu_b, w_out, *, tiles_per_batch):
    t, d = xa.shape
    width = ys.shape[1]
    tm = TOKEN_TILE // 2
    last_row = slab.shape[0] - 1
    tpb = tiles_per_batch * (TOKEN_TILE // tm)
    const = lambda shape: pl.BlockSpec(shape, lambda i: tuple(0 for _ in shape))
    return pl.pallas_call(
        _out0_kernel,
        grid=(t // tm,),
        in_specs=[pl.BlockSpec((tm, width), lambda i: (i, 0)),
                  pl.BlockSpec((tm, width), lambda i: (i, 0)),
                  pl.BlockSpec((tm, width), lambda i: (i, 1)),
                  pl.BlockSpec((None, tm, width), lambda i: (0, i, 0)),
                  pl.BlockSpec((None, tm, width), lambda i: (1, i, 0)),
                  pl.BlockSpec((tm, d), lambda i: (i, 0)),
                  pl.BlockSpec((None, 8, d), lambda i: (jnp.minimum(i // tpb, last_row), 0, 0)),
                  const((1, width)), const((width, width)), const((1, width)),
                  const((width, d)), const((width, d))],
        out_specs=pl.BlockSpec((tm, d), lambda i: (i, 0)),
        out_shape=jax.ShapeDtypeStruct((t, d), F32),
        compiler_params=_params(("arbitrary",)),
        name="hybrid_out",
    )(ys, hz, hz, hfb, hfb, xa, slab, d_skip.reshape(1, width), glu_w.astype(BF16),
      glu_b.reshape(1, width), w_out[:width].astype(BF16), w_out[width:].astype(BF16))


def _store_token_rows(dst_ref, val):
    rows, d = val.shape
    per = d // V7X_LANES
    for c in range(per):
        dst_ref[pl.ds(c, rows, stride=per), :] = val[:, c * V7X_LANES:(c + 1) * V7X_LANES]


def _load_token_rows(src_ref, start, rows, per):
    return [src_ref[pl.ds(start * per + c, rows, stride=per), :] for c in range(per)]


def _route_kernel(x_ref, g_ref, m_ref, w_ref, b_ref, h_ref, r_ref, cnt_ref, carry_ref):
    @pl.when(pl.program_id(0) == 0)
    def _():
        carry_ref[...] = jnp.zeros_like(carry_ref)

    h = _norm_mod(x_ref[...], g_ref[...], m_ref, 3, 4)
    _store_token_rows(h_ref, h)
    logits = jnp.dot(h, w_ref[...], preferred_element_type=F32, precision=HIGHEST) + b_ref[...]
    lane = lax.broadcasted_iota(jnp.int32, logits.shape, 1).astype(F32)
    ninf = -jnp.inf
    big = float(ROUTE_LANES)
    lg = jnp.where(lane < MOE_GROUPS, logits, ninf)
    mg = jnp.max(lg, axis=-1, keepdims=True)
    gidx = jnp.min(jnp.where(lg == mg, lane, big), axis=-1, keepdims=True)
    p_top = 1.0 / jnp.sum(jnp.exp(lg - mg), axis=-1, keepdims=True)
    lo = MOE_GROUPS + gidx * MOE_PER_GROUP
    le = jnp.where(jnp.logical_and(lane >= lo, lane < lo + MOE_PER_GROUP), logits, ninf)
    v1 = jnp.max(le, axis=-1, keepdims=True)
    i1 = jnp.min(jnp.where(le == v1, lane, big), axis=-1, keepdims=True)
    le2 = jnp.where(lane == i1, ninf, le)
    v2 = jnp.max(le2, axis=-1, keepdims=True)
    i2 = jnp.min(jnp.where(le2 == v2, lane, big), axis=-1, keepdims=True)
    tt = jnp.exp(v2 - v1)
    g1 = p_top / (1.0 + tt)
    g2 = p_top * tt / (1.0 + tt)
    e1, e2 = i1 - MOE_GROUPS, i2 - MOE_GROUPS
    tm = h.shape[0]
    tri = (lax.broadcasted_iota(jnp.int32, (tm, tm), 1)
           < lax.broadcasted_iota(jnp.int32, (tm, tm), 0)).astype(BF16)
    hot1, hot2 = lane == e1, lane == e2
    before1 = jnp.dot(tri, hot1.astype(BF16), preferred_element_type=F32)
    before2 = jnp.dot(tri, hot2.astype(BF16), preferred_element_type=F32)
    tot1 = jnp.sum(hot1.astype(F32), axis=0, keepdims=True)
    tot2 = jnp.sum(hot2.astype(F32), axis=0, keepdims=True)
    carry = carry_ref[0:1, :]
    rank1 = jnp.sum(jnp.where(hot1, carry + before1, 0.0), axis=-1, keepdims=True)
    rank2 = jnp.sum(jnp.where(hot2, carry + tot1 + before2, 0.0), axis=-1, keepdims=True)
    carry = carry + tot1 + tot2
    carry_ref[0:1, :] = carry
    cnt_ref[...] = jnp.broadcast_to(carry, cnt_ref.shape)
    out = jnp.where(lane == 0, e1, 0.0)
    out = jnp.where(lane == 1, e2, out)
    out = jnp.where(lane == 2, g1, out)
    out = jnp.where(lane == 3, g2, out)
    out = jnp.where(lane == 4, rank1, out)
    out = jnp.where(lane == 5, rank2, out)
    r_ref[...] = out


def _route(xa, g, slab, wg, bg, we, be, *, tiles_per_batch):
    t, d = xa.shape
    tm = TOKEN_TILE
    per = d // V7X_LANES
    nl = MOE_GROUPS + MOE_EXPERTS
    wr = jnp.pad(jnp.concatenate([wg, we], axis=1), ((0, 0), (0, ROUTE_LANES - nl)))
    br = jnp.pad(jnp.concatenate([bg, be], axis=0), (0, ROUTE_LANES - nl)).reshape(1, ROUTE_LANES)
    last_row = slab.shape[0] - 1
    return pl.pallas_call(
        _route_kernel,
        grid=(t // tm,),
        in_specs=[pl.BlockSpec((tm, d), lambda i: (i, 0)),
                  pl.BlockSpec((1, d), lambda i: (0, 0)),
                  pl.BlockSpec((None, 8, d), lambda i: (jnp.minimum(i // tiles_per_batch, last_row), 0, 0)),
                  pl.BlockSpec((d, ROUTE_LANES), lambda i: (0, 0)),
                  pl.BlockSpec((1, ROUTE_LANES), lambda i: (0, 0))],
        out_specs=[pl.BlockSpec((tm * per, V7X_LANES), lambda i: (i, 0)),
                   pl.BlockSpec((tm, ROUTE_LANES), lambda i: (i, 0)),
                   pl.BlockSpec((8, ROUTE_LANES), lambda i: (0, 0))],
        out_shape=[jax.ShapeDtypeStruct((t * per, V7X_LANES), F32),
                   jax.ShapeDtypeStruct((t, ROUTE_LANES), F32),
                   jax.ShapeDtypeStruct((8, ROUTE_LANES), F32)],
        scratch_shapes=[pltpu.VMEM((8, ROUTE_LANES), F32)],
        compiler_params=_params(("arbitrary",)),
        name="moe_route",
    )(xa, g.reshape(1, d), slab, wr, br)


def _dispatch(route, counts):
    bm = MOE_BLOCK
    n_pairs = route.shape[0] * MOE_TOPK
    experts = route[:, 0:MOE_TOPK].astype(jnp.int32)
    rank = route[:, 4:4 + MOE_TOPK].astype(jnp.int32)
    cnt = counts[0, :MOE_EXPERTS].astype(jnp.int32)
    padded = (cnt + bm - 1) // bm * bm
    pad_end = jnp.cumsum(padded)
    dest = (pad_end - padded)[experts] + rank
    n_blocks = -(-n_pairs // bm) + MOE_EXPERTS
    first_slot = jnp.arange(n_blocks, dtype=jnp.int32) * bm
    block_e = jnp.minimum(jnp.sum((pad_end[None, :] <= first_slot[:, None]).astype(jnp.int32), axis=1),
                          MOE_EXPERTS - 1)
    n_used = (pad_end[-1] // bm).astype(jnp.int32).reshape(1)
    tail = jnp.maximum(pad_end - bm, 0).astype(jnp.int32)
    return dest, block_e, n_used, tail, n_blocks


def _scatter_kernel(tail_ref, nu_ref, d_ref, h_ref, xs_hbm, stage, zsrc, sem, *, n_steps, n_blocks, per):
    i = pl.program_id(0)
    tm = d_ref.shape[1] // MOE_TOPK
    zrows = zsrc.shape[0]

    def fill_at(slot):
        return pltpu.make_async_copy(zsrc, xs_hbm.at[pl.ds(slot * per, zrows), :], sem.at[2])

    def fill(e):
        return fill_at(tail_ref[e])

    def fill_unused(wait):
        def body(blk, c):
            cp = fill_at(blk * MOE_BLOCK)
            cp.wait() if wait else cp.start()
            return c
        lax.fori_loop(nu_ref[0], n_blocks, body, 0)

    def copy(step, r, k):
        return pltpu.make_async_copy(stage.at[lax.rem(step, 2), pl.ds(r * per, per), :],
                                     xs_hbm.at[pl.ds(d_ref[0, MOE_TOPK * r + k] * per, per), :],
                                     sem.at[lax.rem(step, 2)])

    def wait_step(step):
        for r in range(tm):
            for k in range(MOE_TOPK):
                copy(step, r, k).wait()

    @pl.when(i == 0)
    def _():
        zsrc[...] = jnp.zeros_like(zsrc)
        for e in range(MOE_EXPERTS):
            fill(e).start()
        fill_unused(False)
        for e in range(MOE_EXPERTS):
            fill(e).wait()
        fill_unused(True)

    stage[lax.rem(i, 2)] = h_ref[...]
    for r in range(tm):
        for k in range(MOE_TOPK):
            copy(i, r, k).start()

    @pl.when(i > 0)
    def _():
        wait_step(i - 1)

    @pl.when(i == n_steps - 1)
    def _():
        wait_step(i)


def _moe_scatter(h_rows, dest, tail, n_used, n_blocks, per):
    t = dest.shape[0]
    tm = TOKEN_TILE // 2
    n_steps = t // tm
    bm = MOE_BLOCK
    d3 = dest.reshape(n_steps, 1, MOE_TOPK * tm)
    grid_spec = pltpu.PrefetchScalarGridSpec(
        num_scalar_prefetch=2,
        grid=(n_steps,),
        in_specs=[pl.BlockSpec((None, 1, MOE_TOPK * tm), lambda i, tl, nu: (i, 0, 0),
                               memory_space=pltpu.SMEM),
                  pl.BlockSpec((tm * per, V7X_LANES), lambda i, tl, nu: (i, 0))],
        out_specs=pl.BlockSpec(memory_space=pl.ANY),
        scratch_shapes=[pltpu.VMEM((2, tm * per, V7X_LANES), F32),
                        pltpu.VMEM((bm * per, V7X_LANES), F32),
                        pltpu.SemaphoreType.DMA((3,))],
    )
    return pl.pallas_call(
        functools.partial(_scatter_kernel, n_steps=n_steps, n_blocks=n_blocks, per=per),
        grid_spec=grid_spec,
        out_shape=jax.ShapeDtypeStruct((n_blocks * bm * per, V7X_LANES), F32),
        compiler_params=_params(("arbitrary",)),
        name="moe_scatter",
    )(tail, n_used, d3, h_rows)


def _token_copy(src_hbm, tok, dst, slot, r, sem, per):
    return pltpu.make_async_copy(src_hbm.at[pl.ds(tok * per, per), :],
                                 dst.at[slot, pl.ds(r * per, per), :], sem.at[slot])


def _moe_kernel(be_ref, nu_ref, xs_ref, w1_ref, w3_ref, w2_ref, o_ref, w1b, w3b, w2b, prev_ref):
    b = pl.program_id(0)
    bm = MOE_BLOCK
    per = w1_ref.shape[0] // V7X_LANES

    @pl.when(b == 0)
    def _():
        prev_ref[0] = -1

    @pl.when(b < nu_ref[0])
    def _():
        e = be_ref[b]

        @pl.when(e != prev_ref[0])
        def _():
            w1b[...] = w1_ref[...].astype(BF16)
            w3b[...] = w3_ref[...].astype(BF16)
            w2b[...] = w2_ref[...].astype(BF16)
            prev_ref[0] = e

        x = jnp.concatenate(_load_token_rows(xs_ref, 0, bm, per), axis=1).astype(BF16)
        h1 = jnp.dot(x, w1b[...], preferred_element_type=F32)
        h3 = jnp.dot(x, w3b[...], preferred_element_type=F32)
        act = (_silu(h1) * h3).astype(BF16)
        _store_token_rows(o_ref, jnp.dot(act, w2b[...], preferred_element_type=F32))

    @pl.when(b >= nu_ref[0])
    def _():
        o_ref[...] = jnp.zeros_like(o_ref)


def _moe_experts(xs_rows, block_e, n_used, n_blocks, w1, w3, w2, layer):
    d, ff = w1.shape[2], w1.shape[3]
    per = d // V7X_LANES
    bm = MOE_BLOCK
    blk = lambda b, be, nu: jnp.minimum(b, jnp.maximum(nu[0] - 1, 0))
    grid_spec = pltpu.PrefetchScalarGridSpec(
        num_scalar_prefetch=2,
        grid=(n_blocks,),
        in_specs=[pl.BlockSpec((bm * per, V7X_LANES), lambda b, be, nu: (blk(b, be, nu), 0)),
                  pl.BlockSpec((None, None, d, ff), lambda b, be, nu: (layer, be[blk(b, be, nu)], 0, 0)),
                  pl.BlockSpec((None, None, d, ff), lambda b, be, nu: (layer, be[blk(b, be, nu)], 0, 0)),
                  pl.BlockSpec((None, None, ff, d), lambda b, be, nu: (layer, be[blk(b, be, nu)], 0, 0))],
        out_specs=pl.BlockSpec((bm * per, V7X_LANES), lambda b, be, nu: (b, 0)),
        scratch_shapes=[pltpu.VMEM((d, ff), BF16),
                        pltpu.VMEM((d, ff), BF16),
                        pltpu.VMEM((ff, d), BF16),
                        pltpu.SMEM((1,), jnp.int32)],
    )
    return pl.pallas_call(
        _moe_kernel,
        grid_spec=grid_spec,
        out_shape=jax.ShapeDtypeStruct((n_blocks * bm * per, V7X_LANES), F32),
        compiler_params=_params(("arbitrary",)),
        name="moe_experts",
    )(block_e, n_used, xs_rows, w1, w3, w2)


def _combine_kernel(d_ref, dn_ref, yb_hbm, x_ref, r_ref, m_ref, *rest, n_tiles, n_lat_tiles):
    if n_lat_tiles < n_tiles:
        lat_ref, ctx_ref, ybuf, sem = rest
    else:
        lat_ref, ybuf, sem = rest
        ctx_ref = None
    i = pl.program_id(0)
    tm, d = x_ref.shape
    per = d // V7X_LANES
    nrow = MOE_TOPK * tm
    slot = lax.rem(i, 2)

    def start_rows(idx_ref, s):
        for r in range(nrow):
            _token_copy(yb_hbm, idx_ref[0, r], ybuf, s, r, sem, per).start()

    @pl.when(i == 0)
    def _():
        start_rows(d_ref, 0)

    @pl.when(i + 1 < n_tiles)
    def _():
        start_rows(dn_ref, 1 - slot)

    for r in range(nrow):
        _token_copy(yb_hbm, 0, ybuf, slot, r, sem, per).wait()
    r = r_ref[...]
    g0, g1 = r[:, 2:3], r[:, 3:4]
    y0 = _load_token_rows(ybuf.at[slot], 0, tm, per)
    y1 = _load_token_rows(ybuf.at[slot], tm, tm, per)
    def emit(o_ref):
        for c in range(per):
            sl = slice(c * V7X_LANES, (c + 1) * V7X_LANES)
            o_ref[pl.ds(c, tm, stride=per), :] = (
                x_ref[:, sl] + m_ref[5:6, sl] * (g0 * y0[c] + g1 * y1[c]))

    if ctx_ref is None:
        emit(lat_ref)
    else:
        pl.when(i < n_lat_tiles)(lambda: emit(lat_ref))
        pl.when(i >= n_lat_tiles)(lambda: emit(ctx_ref))


def _moe_combine(yb_rows, dest, xa, route, slab, *, tiles_per_batch, t_lat):
    t, d = xa.shape
    per = d // V7X_LANES
    tm = TOKEN_TILE // 2
    n_tiles = t // tm
    n_lat_tiles = t_lat // tm
    row_block = pl.BlockSpec((tm * per, V7X_LANES), lambda i: (jnp.minimum(i, n_lat_tiles - 1), 0))
    out_specs = [row_block]
    out_shape = [jax.ShapeDtypeStruct((t_lat * per, V7X_LANES), F32)]
    if n_lat_tiles < n_tiles:
        out_specs.append(pl.BlockSpec((tm * per, V7X_LANES),
                                      lambda i: (jnp.maximum(i - n_lat_tiles, 0), 0)))
        out_shape.append(jax.ShapeDtypeStruct(((t - t_lat) * per, V7X_LANES), F32))
    tpb = tiles_per_batch * (TOKEN_TILE // tm)
    last_row = slab.shape[0] - 1
    d3 = dest.reshape(n_tiles, tm, MOE_TOPK).transpose(0, 2, 1).reshape(n_tiles, 1, MOE_TOPK * tm)
    smem_block = lambda fn: pl.BlockSpec((None, 1, MOE_TOPK * tm), fn, memory_space=pltpu.SMEM)
    return pl.pallas_call(
        functools.partial(_combine_kernel, n_tiles=n_tiles, n_lat_tiles=n_lat_tiles),
        grid=(n_tiles,),
        in_specs=[smem_block(lambda i: (i, 0, 0)),
                  smem_block(lambda i: (jnp.minimum(i + 1, n_tiles - 1), 0, 0)),
                  pl.BlockSpec(memory_space=pl.ANY),
                  pl.BlockSpec((tm, d), lambda i: (i, 0)),
                  pl.BlockSpec((tm, ROUTE_LANES), lambda i: (i, 0)),
                  pl.BlockSpec((None, 8, d), lambda i: (jnp.minimum(i // tpb, last_row), 0, 0))],
        out_specs=out_specs,
        out_shape=out_shape,
        scratch_shapes=[pltpu.VMEM((2, MOE_TOPK * tm * per, V7X_LANES), F32),
                        pltpu.SemaphoreType.DMA((2,))],
        compiler_params=_params(("arbitrary",)),
        name="moe_combine",
    )(d3, d3, yb_rows, xa, route, slab)


def _moe_layer(xa, g, slab, wg, bg, we, be, w1, w3, w2, *, layer, tiles_per_batch, t_lat):
    per = xa.shape[1] // V7X_LANES
    h_rows, route, counts = _route(xa, g, slab, wg, bg, we, be, tiles_per_batch=tiles_per_batch)
    dest, block_e, n_used, tail, n_blocks = _dispatch(route, counts)
    xs_rows = _moe_scatter(h_rows, dest, tail, n_used, n_blocks, per)
    yb_rows = _moe_experts(xs_rows, block_e, n_used, n_blocks, w1, w3, w2, layer)
    return _moe_combine(yb_rows, dest, xa, route, slab, tiles_per_batch=tiles_per_batch, t_lat=t_lat)


def _ssd_kernel(*refs, rev, n_ctx_chunks, n_lat_chunks):
    if rev:
        (xact_ref, bact_ref, cact_ref, dt_ref, dtb_ref, a_ref, yf_ref, z_ref, ng_ref,
         o_ref, xpair, ypair, cbs, bts, cs, acst, dtt, wtt, tott, state) = refs
    else:
        (xs_ref, xsp_ref, xsn_ref, bm_ref, bmp_ref, bmn_ref, cm_ref, cmp_ref, cmn_ref,
         dt_ref, cwx_ref, cbx_ref, cwb_ref, cbb_ref, cwc_ref, cbc_ref, dtb_ref, a_ref, dsk_ref,
         o_ref, xact_ref, bact_ref, cact_ref,
         padx, padb, padc, xpair, ypair, cbs, bts, cs, acst, dtt, wtt, tott, state) = refs
    k = pl.program_id(1)
    q = SSD_CHUNK
    nst = SSD_STATE
    npair = xact_ref.shape[1] // V7X_LANES
    ngrp = bact_ref.shape[1] // nst
    pairs_per_group = npair // ngrp
    if rev:
        bmat, cmat = bact_ref[...], cact_ref[...]
        for p in range(npair):
            xpair[p] = xact_ref[:, p * V7X_LANES:(p + 1) * V7X_LANES]
    else:
        in_ctx = k < n_ctx_chunks
        jc, jl = k, k - n_ctx_chunks
        first = jnp.where(in_ctx, jc == 0, jl == 0)
        last = jnp.where(in_ctx, jc == n_ctx_chunks - 1, jl == n_lat_chunks - 1)

        def conv_silu(pad_ref, main, prev, nxt, cw_ref, cb_ref):
            return _silu(_dwconv_tile(pad_ref, main[...], prev[...], nxt[...], first, last, cw_ref,
                                      cb_ref[...], q))

        xs = conv_silu(padx, xs_ref, xsp_ref, xsn_ref, cwx_ref, cbx_ref)
        bmat = conv_silu(padb, bm_ref, bmp_ref, bmn_ref, cwb_ref, cbb_ref)
        cmat = conv_silu(padc, cm_ref, cmp_ref, cmn_ref, cwc_ref, cbc_ref)
        bact_ref[...] = bmat
        cact_ref[...] = cmat
        for p in range(npair):
            xb = xs[:, p * V7X_LANES:(p + 1) * V7X_LANES].astype(BF16)
            xpair[p] = xb
            xact_ref[:, p * V7X_LANES:(p + 1) * V7X_LANES] = xb
    for g in range(ngrp):
        bg = bmat[:, g * nst:(g + 1) * nst]
        cg = cmat[:, g * nst:(g + 1) * nst].astype(BF16)
        cs[g] = cg
        cbs[g] = lax.dot_general(cg, bg.astype(BF16), (((1,), (1,)), ((), ())),
                                 preferred_element_type=F32)
        bts[g] = bg.T
    dtr = dt_ref[...] + dtb_ref[...]
    dt = jnp.maximum(dtr, 0.0) + jnp.log(1.0 + jnp.exp(-jnp.abs(dtr)))
    adt = dt * a_ref[...]
    row = lax.broadcasted_iota(jnp.int32, (q, q), 0)
    col = lax.broadcasted_iota(jnp.int32, (q, q), 1)
    causal = (row <= col) if rev else (row >= col)
    acs = jnp.dot(causal.astype(F32), adt, preferred_element_type=F32, precision=HIGHEST)
    acs_t = acs.T
    tot = acs_t[:, 0:1] if rev else acs_t[:, q - 1:q]
    acst[...] = acs_t
    dtt[...] = dt.T
    wtt[...] = jnp.exp(tot - acs_t) * dt.T
    tott[...] = jnp.broadcast_to(jnp.exp(tot), (tott.shape[0], q))

    @pl.when(k == 0)
    def _():
        state[...] = jnp.zeros_like(state)

    hoff = SSD_HEAD_DIM if rev else 0
    lane = lax.broadcasted_iota(jnp.int32, (q, V7X_LANES), 1)
    left = lane < SSD_HEAD_DIM

    def pair_body(p, c):
        g = p // pairs_per_group
        h0 = hoff + 2 * p
        x = xpair[p]
        zero = jnp.zeros_like(x)
        xbd = jnp.concatenate([jnp.where(left, x, zero), jnp.where(left, zero, x)], axis=0)
        cb = cbs[g]
        st = state[p]
        mms, wbs, scales, decs = [], [], [], []
        for m in range(2):
            arow = acst[pl.ds(h0 + m, 1), :]
            acol = jnp.broadcast_to(arow, (q, q)).T
            seg = jnp.where(causal, jnp.exp(acol - arow), 0.0) * dtt[pl.ds(h0 + m, 1), :]
            mms.append((cb * seg).astype(BF16))
            scales.append(jnp.exp(acol))
            wbs.append((bts[g] * wtt[pl.ds(h0 + m, 1), :]).astype(BF16))
            decs.append(tott[pl.ds(h0 + m, 1), :])
        intra = jnp.dot(jnp.concatenate(mms, axis=1), xbd, preferred_element_type=F32)
        new = jnp.dot(jnp.concatenate(wbs, axis=1), xbd, preferred_element_type=F32)
        inter = jnp.dot(cs[g], st.astype(BF16), preferred_element_type=F32)
        ypair[p] = intra + inter * jnp.where(left, scales[0], scales[1])
        state[p] = st * jnp.where(left, decs[0], decs[1]) + new
        return c

    lax.fori_loop(0, npair, pair_body, 0, unroll=SSD_PAIR_UNROLL)
    if not rev:
        for p in range(npair):
            sl = slice(p * V7X_LANES, (p + 1) * V7X_LANES)
            o_ref[:, sl] = ypair[p] + dsk_ref[:, sl] * xs[:, sl]
        return
    sq = jnp.zeros((q, V7X_LANES), F32)
    for p in range(npair):
        sl = slice(p * V7X_LANES, (p + 1) * V7X_LANES)
        gated = (ypair[p] + yf_ref[:, sl]) * _silu(z_ref[:, sl])
        ypair[p] = gated
        sq = sq + gated * gated
    scale = lax.rsqrt(jnp.sum(sq, axis=-1, keepdims=True) / (npair * V7X_LANES) + RMS_EPS)
    for p in range(npair):
        sl = slice(p * V7X_LANES, (p + 1) * V7X_LANES)
        o_ref[:, sl] = (ypair[p] * scale * ng_ref[:, sl]).astype(BF16)


def _ssd_scan(hz, conv_w, conv_b, dt_bias, a_neg, d_skip_lanes, *, rev, nb, n_lat, n_ctx, inner,
              fwd=None, norm_g=None):
    t = hz.shape[0]
    q = SSD_CHUNK
    bc = SSD_GROUPS * SSD_STATE
    ncc, ncl = n_ctx // q, n_lat // q
    hb = q // HALO
    n_halo_blocks = t // HALO
    nheads2 = dt_bias.size
    assert nheads2 == V7X_LANES and q == V7X_LANES and SSD_STATE == V7X_LANES
    npair = inner // V7X_LANES
    xs_col = inner // inner
    b_col = 2 * inner // bc
    c_col = b_col + 1
    dt_col = (2 * inner + 2 * bc) // nheads2

    def chunk(b, k):
        kk = k - ncc
        if rev:
            jc, jl = ncc - 1 - k, ncl - 1 - kk
        else:
            jc, jl = k, kk
        return jnp.where(k < ncc, nb * ncl + b * ncc + jc, b * ncl + jl)

    def main(width, colb):
        return pl.BlockSpec((q, width), lambda b, k: (chunk(b, k), colb))

    def prev(width, colb):
        return pl.BlockSpec((HALO, width), lambda b, k: (jnp.maximum(chunk(b, k) * hb - 1, 0), colb))

    def nxt(width, colb):
        return pl.BlockSpec((HALO, width),
                            lambda b, k: (jnp.minimum(chunk(b, k) * hb + hb, n_halo_blocks - 1), colb))

    const = lambda shape: pl.BlockSpec(shape, lambda b, k: (0, 0))
    shared_scratch = [pltpu.VMEM((npair, q, V7X_LANES), BF16),
                      pltpu.VMEM((npair, q, V7X_LANES), F32),
                      pltpu.VMEM((SSD_GROUPS, q, q), F32),
                      pltpu.VMEM((SSD_GROUPS, SSD_STATE, q), F32),
                      pltpu.VMEM((SSD_GROUPS, q, SSD_STATE), BF16),
                      pltpu.VMEM((nheads2, q), F32),
                      pltpu.VMEM((nheads2, q), F32),
                      pltpu.VMEM((nheads2, q), F32),
                      pltpu.VMEM((nheads2, q), F32),
                      pltpu.VMEM((npair, SSD_STATE, V7X_LANES), F32)]
    kern = functools.partial(_ssd_kernel, rev=rev, n_ctx_chunks=ncc, n_lat_chunks=ncl)
    dtb, an = dt_bias.reshape(1, nheads2), a_neg.reshape(1, nheads2)
    if rev:
        y_fwd, xact, bact, cact = fwd
        return pl.pallas_call(
            kern,
            grid=(nb, ncc + ncl),
            in_specs=[main(inner, 0), main(bc, 0), main(bc, 0), main(nheads2, dt_col),
                      const((1, nheads2)), const((1, nheads2)),
                      main(inner, 0), main(inner, 0), const((1, inner))],
            out_specs=main(inner, 0),
            out_shape=jax.ShapeDtypeStruct((t, inner), BF16),
            scratch_shapes=shared_scratch,
            compiler_params=_params(("arbitrary", "arbitrary")),
            name="ssd_scan_bwd",
        )(xact, bact, cact, hz, dtb, an, y_fwd, hz, norm_g.reshape(1, inner))
    cw = jnp.pad(conv_w, ((0, 8 - CONV_W), (0, 0)))
    cb = conv_b.reshape(1, -1)
    return pl.pallas_call(
        kern,
        grid=(nb, ncc + ncl),
        in_specs=[main(inner, xs_col), prev(inner, xs_col), nxt(inner, xs_col),
                  main(bc, b_col), prev(bc, b_col), nxt(bc, b_col),
                  main(bc, c_col), prev(bc, c_col), nxt(bc, c_col),
                  main(nheads2, dt_col),
                  const((8, inner)), const((1, inner)),
                  const((8, bc)), const((1, bc)),
                  const((8, bc)), const((1, bc)),
                  const((1, nheads2)), const((1, nheads2)), const((1, inner))],
        out_specs=[main(inner, 0), main(inner, 0), main(bc, 0), main(bc, 0)],
        out_shape=[jax.ShapeDtypeStruct((t, inner), F32), jax.ShapeDtypeStruct((t, inner), BF16),
                   jax.ShapeDtypeStruct((t, bc), F32), jax.ShapeDtypeStruct((t, bc), F32)],
        scratch_shapes=[pltpu.VMEM((q + 2 * HALO, inner), F32),
                        pltpu.VMEM((q + 2 * HALO, bc), F32),
                        pltpu.VMEM((q + 2 * HALO, bc), F32)] + shared_scratch,
        compiler_params=_params(("arbitrary", "arbitrary")),
        name="ssd_scan_fwd",
    )(hz, hz, hz, hz, hz, hz, hz, hz, hz, hz,
      cw[:, :inner], cb[:, :inner], cw[:, inner:inner + bc], cb[:, inner:inner + bc],
      cw[:, inner + bc:], cb[:, inner + bc:], dtb, an, d_skip_lanes)


def _out1_kernel(yn_ref, x_ref, m_ref, w_ref, o_ref):
    o_ref[...] = x_ref[...] + m_ref[2:3, :] * jnp.dot(yn_ref[...], w_ref[...], preferred_element_type=F32)


def _out1(yn, xc, slab, w_out, *, t_lat, tiles_per_batch):
    t, d = t_lat, xc.shape[1]
    inner = yn.shape[1]
    tm = TOKEN_TILE // 2
    tpb = tiles_per_batch * (TOKEN_TILE // tm)
    last_row = slab.shape[0] - 1
    return pl.pallas_call(
        _out1_kernel,
        grid=(t // tm,),
        in_specs=[pl.BlockSpec((tm, inner), lambda i: (i, 0)),
                  pl.BlockSpec((tm, d), lambda i: (i, 0)),
                  pl.BlockSpec((None, 8, d), lambda i: (jnp.minimum(i // tpb, last_row), 0, 0)),
                  pl.BlockSpec((inner, d), lambda i: (0, 0))],
        out_specs=pl.BlockSpec((tm, d), lambda i: (i, 0)),
        out_shape=jax.ShapeDtypeStruct((t, d), F32),
        compiler_params=_params(("arbitrary",)),
        name="ssd_out",
    )(yn, xc, slab, w_out.astype(BF16))


def _colmajor_kernel(lat_ref, ctx_ref, o_ref, *, n_lat_steps):
    i = pl.program_id(0)
    n_tok = o_ref.shape[0]
    per = o_ref.shape[1] // V7X_LANES

    def emit(src2d):
        for c, piece in enumerate(_load_token_rows(src2d, 0, n_tok, per)):
            o_ref[:, c * V7X_LANES:(c + 1) * V7X_LANES] = piece

    pl.when(i < n_lat_steps)(lambda: emit(lat_ref.reshape(n_tok * per, V7X_LANES)))
    pl.when(i >= n_lat_steps)(lambda: emit(ctx_ref))


def _to_column_major(lat_rows, ctx_rows, *, nb, rows, d):
    per = d // V7X_LANES
    n_lat_steps = nb * GRID_W
    n_ctx_steps = ctx_rows.shape[0] // (rows * per)
    lat4 = lat_rows.reshape(nb, rows, GRID_W * per, V7X_LANES)

    def lat_index(i):
        j = jnp.minimum(i, n_lat_steps - 1)
        return (j // GRID_W, 0, j % GRID_W, 0)

    return pl.pallas_call(
        functools.partial(_colmajor_kernel, n_lat_steps=n_lat_steps),
        grid=(n_lat_steps + n_ctx_steps,),
        in_specs=[pl.BlockSpec((None, rows, per, V7X_LANES), lat_index),
                  pl.BlockSpec((rows * per, V7X_LANES), lambda i: (jnp.maximum(i - n_lat_steps, 0), 0))],
        out_specs=pl.BlockSpec((rows, d), lambda i: (i, 0)),
        out_shape=jax.ShapeDtypeStruct(((n_lat_steps + n_ctx_steps) * rows, d), F32),
        compiler_params=_params(("arbitrary",)),
        name="to_column_major",
    )(lat4, ctx_rows)


def _final_kernel(x_ref, g_ref, o_ref):
    d = o_ref.shape[1]
    per = d // V7X_LANES
    nr = x_ref.shape[1] // per
    src = x_ref.reshape(GRID_W * nr * per, V7X_LANES)
    for rr in range(nr):
        pieces = [src[pl.ds(rr * per + c, GRID_W, stride=nr * per), :] for c in range(per)]
        ss = pieces[0] * pieces[0]
        for p in pieces[1:]:
            ss = ss + p * p
        scale = lax.rsqrt(jnp.sum(ss, axis=-1, keepdims=True) / d + RMS_EPS)
        for c, p in enumerate(pieces):
            sl = slice(c * V7X_LANES, (c + 1) * V7X_LANES)
            o_ref[rr * GRID_W:(rr + 1) * GRID_W, sl] = p * scale * g_ref[:, sl]


def _final_norm_row_major(xl_rows, g, *, nb, rows, d):
    per = d // V7X_LANES
    nr = math.gcd(rows, FINAL_ROWS_PER_STEP)
    x4 = xl_rows.reshape(nb, GRID_W, rows * per, V7X_LANES)
    return pl.pallas_call(
        _final_kernel,
        grid=(nb, rows // nr),
        in_specs=[pl.BlockSpec((None, GRID_W, nr * per, V7X_LANES), lambda b, r: (b, 0, r, 0)),
                  pl.BlockSpec((1, d), lambda b, r: (0, 0))],
        out_specs=pl.BlockSpec((nr * GRID_W, d), lambda b, r: (b * (rows // nr) + r, 0)),
        out_shape=jax.ShapeDtypeStruct((nb * rows * GRID_W, d), F32),
        compiler_params=_params(("arbitrary", "arbitrary")),
        name="final_norm",
    )(x4, g.reshape(1, d))


def kernel(x, c, ctx, c_ctx, norm_mix_g, norm_ffn_g, mod_w, mod_b, hy_w_in, hy_w_out, s5_lam_re, s5_lam_im, s5_log_dt, s5_b_re, s5_b_im, s5_c_re, s5_c_im, s5_d, s5_glu_w, s5_glu_b, lru_conv_w, lru_conv_b, lru_wa, lru_ba, lru_wx, lru_bx, lru_lam, ssd_w_in, ssd_conv_w, ssd_conv_b, ssd_dt_bias, ssd_a_log, ssd_d, ssd_norm_g, ssd_w_out, moe_wg, moe_bg, moe_we, moe_be, moe_w1, moe_w3, moe_w2, final_norm_g):
    nb, n_lat, d = x.shape
    n_ctx = ctx.shape[1]
    depth = mod_w.shape[0]
    assert depth == 2 and nb + 1 <= 8
    assert n_lat % TOKEN_TILE == 0 and (nb * n_ctx) % TOKEN_TILE == 0
    t_lat = nb * n_lat
    rows = n_lat // GRID_W
    tiles_per_batch = n_lat // TOKEN_TILE
    n_cond = nb + 1

    xa = jnp.concatenate([x.reshape(t_lat, d), ctx.reshape(nb * n_ctx, d)], axis=0)
    cvec = jnp.zeros((8, d), F32).at[:nb].set(c).at[nb].set(c_ctx)
    mods = _mod_vectors(cvec, mod_w, mod_b)
    slab0 = _mod_slab(mods[0], n_cond, d)
    slab1 = _mod_slab(mods[1], n_cond, d)

    width = s5_d.shape[1]
    hz = _norm_matmul(xa, norm_mix_g[0], slab0, hy_w_in[0].astype(BF16), shift_row=0, scale_row=1,
                      tn=width, tiles_per_batch=tiles_per_batch)
    s5w = _s5_weights(s5_lam_re[0], s5_lam_im[0], s5_log_dt[0], s5_b_re[0], s5_b_im[0],
                      s5_c_re[0], s5_c_im[0])
    ys = _s5_mixer(hz, s5w, width=width, nb=nb, n_lat=n_lat, n_ctx=n_ctx)
    sp = jax.nn.softplus(-lru_lam[0].astype(F32)).reshape(2, 1, width)
    hfb = _lru_scan(hz, 2, lru_conv_w[0], lru_conv_b[0], _block_diag_gates(lru_wa[0]),
                    _block_diag_gates(lru_wx[0]), lru_ba[0].reshape(2, 1, width),
                    lru_bx[0].reshape(2, 1, width), sp, nb=nb, n_lat=n_lat, n_ctx=n_ctx)
    xa = _out0(ys, hz, hfb, xa, slab0, s5_d[0], s5_glu_w[0], s5_glu_b[0], hy_w_out[0],
               tiles_per_batch=tiles_per_batch)
    lat_rows, ctx_rows = _moe_layer(xa, norm_ffn_g[0], slab0, moe_wg[0], moe_bg[0], moe_we[0],
                                    moe_be[0], moe_w1, moe_w3, moe_w2, layer=0,
                                    tiles_per_batch=tiles_per_batch, t_lat=t_lat)

    xc = _to_column_major(lat_rows, ctx_rows, nb=nb, rows=rows, d=d)
    inner = ssd_norm_g.shape[1]
    hz1 = _norm_matmul(xc, norm_mix_g[1], slab1, ssd_w_in[0].astype(BF16), shift_row=0, scale_row=1,
                       tn=ssd_w_in.shape[2] // 9, tiles_per_batch=tiles_per_batch)
    a_neg = -jnp.exp(ssd_a_log[0].astype(F32))
    dsk = jnp.repeat(ssd_d[0], SSD_HEAD_DIM).reshape(1, inner)
    ssd_args = dict(nb=nb, n_lat=n_lat, n_ctx=n_ctx, inner=inner)
    fwd = _ssd_scan(hz1, ssd_conv_w[0], ssd_conv_b[0], ssd_dt_bias[0], a_neg, dsk, rev=False, **ssd_args)
    yn = _ssd_scan(hz1, ssd_conv_w[0], ssd_conv_b[0], ssd_dt_bias[0], a_neg, dsk, rev=True,
                   fwd=fwd, norm_g=ssd_norm_g[0], **ssd_args)
    xl = _out1(yn, xc, slab1, ssd_w_out[0], t_lat=t_lat, tiles_per_batch=tiles_per_batch)
    (xl_rows,) = _moe_layer(xl, norm_ffn_g[1], slab1, moe_wg[1], moe_bg[1], moe_we[1], moe_be[1],
                            moe_w1, moe_w3, moe_w2, layer=1, tiles_per_batch=tiles_per_batch,
                            t_lat=t_lat)
    out = _final_norm_row_major(xl_rows, final_norm_g, nb=nb, rows=rows, d=d)
    return out.reshape(nb, n_lat, d)
```

```python
import functools
import math

import jax
import jax.numpy as jnp
from jax import lax
from jax.experimental import pallas as pl
from jax.experimental.pallas import tpu as pltpu

F32 = jnp.float32
BF16 = jnp.bfloat16
HIGHEST = lax.Precision.HIGHEST

GRID_W = 64
N_MOD = 6
RMS_EPS = 1e-6
CONV_W = 4
CONV_PAD_LEFT = CONV_W // 2
S5_GROUP = 16
S5_STATE = 64
LRU_HEADS = 16
LRU_C = 8.0
SSD_HEAD_DIM = 64
SSD_GROUPS = 8
SSD_STATE = 128
SSD_CHUNK = 128
MOE_GROUPS = 4
MOE_PER_GROUP = 8
MOE_EXPERTS = MOE_GROUPS * MOE_PER_GROUP
MOE_TOPK = 2

V7X_LANES = 128
V7X_SUBLANES = 8
V7X_MXU_DIM = 256
V7X_VMEM_LIMIT_BYTES = 60000 * 1024

TOKEN_TILE = 512
IN_PROJ_TILE = 1024
S5_CHUNK = V7X_MXU_DIM // S5_GROUP
S5_PAIR_LANES = 2 * S5_STATE
S5_SCAN_LANES = 512
LRU_TILE = 256
LRU_GATE_BLOCK = V7X_MXU_DIM
HALO = V7X_SUBLANES
MOE_BLOCK = 256
SSD_PAIR_UNROLL = 4
FINAL_ROWS_PER_STEP = 4
ROUTE_LANES = V7X_LANES


def _params(sem):
    return pltpu.CompilerParams(dimension_semantics=sem, vmem_limit_bytes=V7X_VMEM_LIMIT_BYTES)


def _silu(v):
    return v * jax.nn.sigmoid(v)


def _gelu(v):
    return jax.nn.gelu(v, approximate=True)


def _mod_kernel(c_ref, w_ref, b_ref, o_ref):
    s = _silu(c_ref[...])
    o_ref[...] = jnp.dot(s, w_ref[...], preferred_element_type=F32, precision=HIGHEST) + b_ref[...]


def _mod_vectors(cvec, mod_w, mod_b):
    depth, d, n = mod_w.shape
    tn = n // 8
    return pl.pallas_call(
        _mod_kernel,
        grid=(depth, n // tn),
        in_specs=[pl.BlockSpec((8, d), lambda l, j: (0, 0)),
                  pl.BlockSpec((None, d, tn), lambda l, j: (l, 0, j)),
                  pl.BlockSpec((None, 1, tn), lambda l, j: (l, 0, j))],
        out_specs=pl.BlockSpec((None, 8, tn), lambda l, j: (l, 0, j)),
        out_shape=jax.ShapeDtypeStruct((depth, 8, n), F32),
        compiler_params=_params(("arbitrary", "arbitrary")),
        name="mod_vectors",
    )(cvec, mod_w, mod_b.reshape(depth, 1, n))


def _mod_slab(mods_layer, n_rows, d):
    m = mods_layer[:n_rows].reshape(n_rows, N_MOD, d)
    return jnp.pad(m, ((0, 0), (0, 8 - N_MOD), (0, 0)))


def _norm_mod(x, g, m_ref, shift_row, scale_row):
    ms = jnp.mean(x * x, axis=-1, keepdims=True)
    y = x * lax.rsqrt(ms + RMS_EPS) * g
    return y * (1.0 + m_ref[scale_row:scale_row + 1, :]) + m_ref[shift_row:shift_row + 1, :]


def _norm_mm_kernel(x_ref, g_ref, m_ref, w_ref, o_ref, xn_ref, *, shift_row, scale_row):
    @pl.when(pl.program_id(1) == 0)
    def _():
        xn_ref[...] = _norm_mod(x_ref[...], g_ref[...], m_ref, shift_row, scale_row).astype(BF16)

    o_ref[...] = jnp.dot(xn_ref[...], w_ref[...], preferred_element_type=F32)


def _norm_matmul(xa, g, slab, w, *, shift_row, scale_row, tn, n_lat):
    t, d = xa.shape
    n = w.shape[1]
    tm = math.gcd(IN_PROJ_TILE, n_lat)
    tiles_per_batch = n_lat // tm
    last_row = slab.shape[0] - 1
    return pl.pallas_call(
        functools.partial(_norm_mm_kernel, shift_row=shift_row, scale_row=scale_row),
        grid=(pl.cdiv(t, tm), n // tn),
        in_specs=[pl.BlockSpec((tm, d), lambda i, j: (i, 0)),
                  pl.BlockSpec((1, d), lambda i, j: (0, 0)),
                  pl.BlockSpec((None, 8, d),
                               lambda i, j: (jnp.minimum(i // tiles_per_batch, last_row), 0, 0)),
                  pl.BlockSpec((d, tn), lambda i, j: (0, j))],
        out_specs=pl.BlockSpec((tm, tn), lambda i, j: (i, j)),
        out_shape=jax.ShapeDtypeStruct((t, n), F32),
        scratch_shapes=[pltpu.VMEM((tm, d), BF16)],
        compiler_params=_params(("arbitrary", "arbitrary")),
        name="norm_matmul",
    )(xa, g.reshape(1, d), slab, w)


def _s5_weights(lam_re, lam_im, log_dt, b_re, b_im, c_re, c_im):
    q = S5_CHUNK
    ngrp, nst = lam_re.shape[1], lam_re.shape[2]
    nch = b_re.shape[-1]
    lr, li = lam_re.astype(F32), lam_im.astype(F32)
    dt = jnp.exp(log_dt.astype(F32))[..., None]
    mag = jnp.exp(lr * dt)
    a_re, a_im = mag * jnp.cos(li * dt), mag * jnp.sin(li * dt)
    den = lr * lr + li * li
    k_re = ((a_re - 1) * lr + a_im * li) / den
    k_im = (a_im * lr - (a_re - 1) * li) / den
    bb_re = k_re[..., None] * b_re - k_im[..., None] * b_im
    bb_im = k_re[..., None] * b_im + k_im[..., None] * b_re
    ks = jnp.arange(q + 1, dtype=F32)[:, None, None, None]
    pmag = jnp.exp(ks * (lr * dt))
    pw_re, pw_im = pmag * jnp.cos(ks * (li * dt)), pmag * jnp.sin(ks * (li * dt))
    ab_re = pw_re[..., None] * bb_re - pw_im[..., None] * bb_im
    ab_im = pw_re[..., None] * bb_im + pw_im[..., None] * bb_re
    contract = lambda c, ab: jnp.sum(ab.transpose(1, 2, 0, 4, 3)[:, :, :, :, None, :]
                                     * c[:, :, None, None, :, :], axis=-1)
    kmat = contract(c_re, ab_re) - contract(c_im, ab_im)
    s_idx = jnp.arange(q)[:, None]
    t_idx = jnp.arange(q)[None, :]
    lag_f = t_idx - s_idx
    lag_b = s_idx - t_idx
    kf = jnp.where((lag_f >= 0)[None, :, :, None, None], kmat[0][:, jnp.clip(lag_f, 0, q - 1)], 0.0)
    kb = jnp.where((lag_b >= 0)[None, :, :, None, None], kmat[1][:, jnp.clip(lag_b, 0, q - 1)], 0.0)
    toep = (kf + kb).transpose(0, 1, 3, 2, 4).reshape(ngrp, q * nch, q * nch)
    sf_re = ab_re[q - 1 - jnp.arange(q), 0]
    sf_im = ab_im[q - 1 - jnp.arange(q), 0]
    sb_re = ab_re[jnp.arange(q), 1]
    sb_im = ab_im[jnp.arange(q), 1]
    summ = jnp.stack([sf_re, sf_im, sb_re, sb_im], 0)
    summ = summ.transpose(2, 1, 4, 0, 3).reshape(ngrp, q * nch, 4, nst)
    npair = ngrp // 2
    summ = summ.reshape(npair, 2, q * nch, 4, nst)
    zero = jnp.zeros_like(summ[:, 0])
    wa = jnp.concatenate([jnp.concatenate([summ[:, 0], zero], -1),
                          jnp.concatenate([zero, summ[:, 1]], -1)], 1)
    wa = wa.reshape(npair, 2 * q * nch, 4 * 2 * nst)
    cp_re = c_re[None] * pw_re[:, :, :, None, :] - c_im[None] * pw_im[:, :, :, None, :]
    cp_im = c_re[None] * pw_im[:, :, :, None, :] + c_im[None] * pw_re[:, :, :, None, :]
    tf = jnp.arange(q) + 1
    tb = q - jnp.arange(q)
    cr = jnp.stack([cp_re[tf, 0], -cp_im[tf, 0], cp_re[tb, 1], -cp_im[tb, 1]], 0)
    cr = cr.transpose(2, 0, 4, 1, 3).reshape(ngrp, 4, nst, q * nch)
    cr = cr.reshape(npair, 2, 4, nst, q * nch)
    zc = jnp.zeros_like(cr[:, 0])
    wc0 = jnp.concatenate([cr[:, 0], zc], 2)
    wc1 = jnp.concatenate([zc, cr[:, 1]], 2)
    wc = jnp.stack([wc0, wc1], 1).reshape(ngrp, 4 * 2 * nst, q * nch)
    a16 = jnp.stack([pw_re[q, 0], pw_im[q, 0], pw_re[q, 1], pw_im[q, 1]], 0)
    a16 = jnp.pad(a16.reshape(4, ngrp * nst), ((0, 4), (0, 0)))
    return toep.astype(BF16), wa.astype(BF16), wc.astype(BF16), a16


def _s5_row_chunk(nc):
    for r in (128, 96, 64, 48, 32, 16):
        if nc % r == 0:
            return r
    raise ValueError(f"unsupported chunk count {nc}")


def _s5_summary_kernel(u_ref, w_ref, ug_ref, o_ref, *, nc):
    q, nch = S5_CHUNK, S5_GROUP
    ngl = u_ref.shape[1] // nch
    rc = _s5_row_chunk(nc)

    def regroup(k, c):
        r0 = pl.multiple_of(k * rc, rc)
        steps = [u_ref[pl.ds(r0 * q + s, rc, stride=q), :] for s in range(q)]
        for gl in range(ngl):
            ug_ref[gl, pl.ds(r0, rc), :] = jnp.concatenate(
                [p[:, gl * nch:(gl + 1) * nch] for p in steps], axis=1).astype(BF16)
        return c

    lax.fori_loop(0, nc // rc, regroup, 0)
    for pr in range(ngl // 2):
        x = jnp.concatenate([ug_ref[2 * pr], ug_ref[2 * pr + 1]], axis=1)
        r = jnp.dot(x, w_ref[pr], preferred_element_type=F32)
        for qd in range(4):
            o_ref[qd, :, pr * S5_PAIR_LANES:(pr + 1) * S5_PAIR_LANES] = (
                r[:, qd * S5_PAIR_LANES:(qd + 1) * S5_PAIR_LANES])


def _s5_scan_kernel(s_ref, a_ref, h_ref, *, nb, n_lat, n_ctx):
    afr, afi = a_ref[0:1, :], a_ref[1:2, :]
    abr, abi = a_ref[2:3, :], a_ref[3:4, :]
    zero = jnp.zeros_like(afr)

    def step(rf, rb, carry):
        hfr, hfi, hbr, hbi = carry
        h_ref[0, pl.ds(rf, 1), :] = hfr
        h_ref[1, pl.ds(rf, 1), :] = hfi
        h_ref[2, pl.ds(rb, 1), :] = hbr
        h_ref[3, pl.ds(rb, 1), :] = hbi
        sfr, sfi = s_ref[0, pl.ds(rf, 1), :], s_ref[1, pl.ds(rf, 1), :]
        sbr, sbi = s_ref[2, pl.ds(rb, 1), :], s_ref[3, pl.ds(rb, 1), :]
        return (afr * hfr - afi * hfi + sfr, afr * hfi + afi * hfr + sfi,
                abr * hbr - abi * hbi + sbr, abr * hbi + abi * hbr + sbi)

    for b in range(nb):
        lat0 = b * n_lat
        ctx0 = nb * n_lat + b * n_ctx
        carry = lax.fori_loop(
            0, n_ctx, lambda i, c: step(ctx0 + i, ctx0 + n_ctx - 1 - i, c), (zero, zero, zero, zero))
        lax.fori_loop(0, n_lat, lambda i, c: step(lat0 + i, lat0 + n_lat - 1 - i, c), carry)


def _s5_output_kernel(ug_ref, t_ref, h_ref, w_ref, o_ref, y_ref, *, nc):
    q, nch = S5_CHUNK, S5_GROUP
    ngl = ug_ref.shape[0]
    rc = _s5_row_chunk(nc)
    for gl in range(ngl):
        pr = gl // 2
        hcat = jnp.concatenate(
            [h_ref[qd, :, pr * S5_PAIR_LANES:(pr + 1) * S5_PAIR_LANES] for qd in range(4)],
            axis=1).astype(BF16)
        y_ref[gl] = (jnp.dot(ug_ref[gl], t_ref[gl], preferred_element_type=F32)
                     + jnp.dot(hcat, w_ref[gl], preferred_element_type=F32))

    def ungroup(k, c):
        r0 = pl.multiple_of(k * rc, rc)
        ys = [y_ref[gl, pl.ds(r0, rc), :] for gl in range(ngl)]
        for s in range(q):
            o_ref[pl.ds(r0 * q + s, rc, stride=q), :] = jnp.concatenate(
                [y[:, s * nch:(s + 1) * nch] for y in ys], axis=1)
        return c

    lax.fori_loop(0, nc // rc, ungroup, 0)


def _s5_mixer(hz, weights, *, width, nb, n_lat, n_ctx):
    toep, wa, wc, a16 = weights
    t = hz.shape[0]
    q, nch = S5_CHUNK, S5_GROUP
    ngrp = width // nch
    ngl = V7X_LANES // nch
    nstrip = ngrp // ngl
    nc = t // q
    cw = q * nch
    lanes = ngrp * S5_STATE
    slanes = ngl * S5_STATE
    ug, summ = pl.pallas_call(
        functools.partial(_s5_summary_kernel, nc=nc),
        grid=(nstrip,),
        in_specs=[pl.BlockSpec((t, V7X_LANES), lambda j: (0, j)),
                  pl.BlockSpec((ngl // 2, 2 * cw, 4 * S5_PAIR_LANES), lambda j: (j, 0, 0))],
        out_specs=[pl.BlockSpec((ngl, nc, cw), lambda j: (j, 0, 0)),
                   pl.BlockSpec((4, nc, slanes), lambda j: (0, 0, j))],
        out_shape=[jax.ShapeDtypeStruct((ngrp, nc, cw), BF16),
                   jax.ShapeDtypeStruct((4, nc, lanes), F32)],
        compiler_params=_params(("arbitrary",)),
        name="s5_summary",
    )(hz, wa)
    wl = S5_SCAN_LANES
    carry = pl.pallas_call(
        functools.partial(_s5_scan_kernel, nb=nb, n_lat=n_lat // q, n_ctx=n_ctx // q),
        grid=(lanes // wl,),
        in_specs=[pl.BlockSpec((4, nc, wl), lambda j: (0, 0, j)),
                  pl.BlockSpec((8, wl), lambda j: (0, j))],
        out_specs=pl.BlockSpec((4, nc, wl), lambda j: (0, 0, j)),
        out_shape=jax.ShapeDtypeStruct((4, nc, lanes), F32),
        compiler_params=_params(("arbitrary",)),
        name="s5_scan",
    )(summ, a16)
    return pl.pallas_call(
        functools.partial(_s5_output_kernel, nc=nc),
        grid=(nstrip,),
        in_specs=[pl.BlockSpec((ngl, nc, cw), lambda j: (j, 0, 0)),
                  pl.BlockSpec((ngl, cw, cw), lambda j: (j, 0, 0)),
                  pl.BlockSpec((4, nc, slanes), lambda j: (0, 0, j)),
                  pl.BlockSpec((ngl, 4 * S5_PAIR_LANES, cw), lambda j: (j, 0, 0))],
        out_specs=pl.BlockSpec((t, V7X_LANES), lambda j: (0, j)),
        out_shape=jax.ShapeDtypeStruct((t, width), F32),
        scratch_shapes=[pltpu.VMEM((ngl, nc, cw), F32)],
        compiler_params=_params(("arbitrary",)),
        name="s5_output",
    )(ug, toep, carry, wc)


def _dwconv_tile(pad_ref, main, prev, nxt, first, last, cw_ref, cb, rows):
    pad_ref[0:HALO, :] = jnp.where(first, 0.0, prev)
    pad_ref[HALO:HALO + rows, :] = main
    pad_ref[HALO + rows:2 * HALO + rows, :] = jnp.where(last, 0.0, nxt)
    total = rows + 2 * HALO
    pieces = []
    for ct in range(pad_ref.shape[1] // V7X_LANES):
        sl = slice(ct * V7X_LANES, (ct + 1) * V7X_LANES)
        col = pad_ref[:, sl]
        acc = cb[:, sl]
        for k in range(CONV_W):
            shift = (CONV_PAD_LEFT - k) % total
            tap = col if shift == 0 else pltpu.roll(col, shift, axis=0)
            acc = acc + cw_ref[k:k + 1, sl] * tap[HALO:HALO + rows, :]
        pieces.append(acc)
    return jnp.concatenate(pieces, axis=1)


def _lru_kernel(v_ref, vp_ref, vn_ref, cw_ref, cb_ref, wa_ref, wx_ref, ba_ref, bx_ref, sp_ref,
                o_ref, pad_ref, a_ref, b_ref, h_ref, *, n_lat_tiles):
    d = pl.program_id(0)
    k = pl.program_id(2)
    tc = LRU_TILE
    j = jnp.where(d == 0, k - 1, n_lat_tiles - k)
    first = jnp.logical_or(k == 0, j == 0)
    last = jnp.logical_or(k == 0, j == n_lat_tiles - 1)
    vc = _dwconv_tile(pad_ref, v_ref[...], vp_ref[...], vn_ref[...], first, last, cw_ref,
                      cb_ref[...], tc)
    vcb = vc.astype(BF16)
    nblk = vc.shape[1] // LRU_GATE_BLOCK

    def gate(w_ref, bias):
        parts = [jnp.dot(vcb[:, m * LRU_GATE_BLOCK:(m + 1) * LRU_GATE_BLOCK], w_ref[m],
                         preferred_element_type=F32) for m in range(nblk)]
        return jax.nn.sigmoid(jnp.concatenate(parts, axis=1) + bias)

    r = gate(wa_ref, ba_ref[...])
    i = gate(wx_ref, bx_ref[...])
    a = jnp.exp(-LRU_C * r * sp_ref[...])
    a_ref[...] = a
    b_ref[...] = jnp.sqrt(1.0 - a * a) * (i * vc)

    @pl.when(k == 0)
    def _():
        h_ref[...] = jnp.zeros_like(h_ref)

    def body(t, h):
        tt = jnp.where(d == 0, t, tc - 1 - t)
        h = a_ref[pl.ds(tt, 1), :] * h + b_ref[pl.ds(tt, 1), :]
        o_ref[pl.ds(tt, 1), :] = h
        return h

    h_ref[0:1, :] = lax.fori_loop(0, tc, body, h_ref[0:1, :], unroll=8)


def _lru_scan(hz, col_block, conv_w, conv_b, wa, wx, ba, bx, sp, *, nb, n_lat, n_ctx):
    t = hz.shape[0]
    width = conv_w.shape[1]
    tc = LRU_TILE
    assert n_ctx == tc and n_lat % tc == 0
    nlt = n_lat // tc
    hb = tc // HALO
    n_halo_blocks = t // HALO

    def row_block(d, b, k):
        j = jnp.where(d == 0, k - 1, nlt - k)
        return jnp.where(k == 0, nb * nlt + b, b * nlt + j)

    cw = jnp.pad(conv_w, ((0, 8 - CONV_W), (0, 0)))
    nblk = width // LRU_GATE_BLOCK
    vec = lambda: pl.BlockSpec((None, 1, width), lambda d, b, k: (d, 0, 0))
    return pl.pallas_call(
        functools.partial(_lru_kernel, n_lat_tiles=nlt),
        grid=(2, nb, nlt + 1),
        in_specs=[pl.BlockSpec((tc, width), lambda d, b, k: (row_block(d, b, k), col_block)),
                  pl.BlockSpec((HALO, width),
                               lambda d, b, k: (jnp.maximum(row_block(d, b, k) * hb - 1, 0), col_block)),
                  pl.BlockSpec((HALO, width),
                               lambda d, b, k: (jnp.minimum(row_block(d, b, k) * hb + hb,
                                                            n_halo_blocks - 1), col_block)),
                  pl.BlockSpec((8, width), lambda d, b, k: (0, 0)),
                  pl.BlockSpec((1, width), lambda d, b, k: (0, 0)),
                  pl.BlockSpec((None, nblk, LRU_GATE_BLOCK, LRU_GATE_BLOCK), lambda d, b, k: (d, 0, 0, 0)),
                  pl.BlockSpec((None, nblk, LRU_GATE_BLOCK, LRU_GATE_BLOCK), lambda d, b, k: (d, 0, 0, 0)),
                  vec(), vec(), vec()],
        out_specs=pl.BlockSpec((None, tc, width), lambda d, b, k: (d, row_block(d, b, k), 0)),
        out_shape=jax.ShapeDtypeStruct((2, t, width), F32),
        scratch_shapes=[pltpu.VMEM((tc + 2 * HALO, width), F32),
                        pltpu.VMEM((tc, width), F32),
                        pltpu.VMEM((tc, width), F32),
                        pltpu.VMEM((8, width), F32)],
        compiler_params=_params(("arbitrary", "arbitrary", "arbitrary")),
        name="rglru_scan",
    )(hz, hz, hz, cw, conv_b.reshape(1, width), wa, wx, ba, bx, sp)


def _block_diag_gates(w):
    ndir, heads, hd, _ = w.shape
    per = LRU_GATE_BLOCK // hd
    w = w.reshape(ndir, heads // per, per, hd, hd)
    eye = jnp.eye(per, dtype=w.dtype)
    full = jnp.einsum('dmhij,hk->dmhikj', w, eye)
    return full.reshape(ndir, heads // per, per * hd, per * hd).astype(BF16)


def _out0_kernel(ys_ref, u_ref, g_ref, hf_ref, hb_ref, x_ref, m_ref, d_ref, gw_ref, gb_ref,
                 wt_ref, wb_ref, o_ref):
    y = ys_ref[...] + d_ref[...] * u_ref[...]
    z = _gelu(y)
    gate = jax.nn.sigmoid(jnp.dot(z.astype(BF16), gw_ref[...], preferred_element_type=F32) + gb_ref[...])
    a = z * gate
    r = _gelu(g_ref[...]) * (hf_ref[...] + hb_ref[...])
    dx = (jnp.dot(a.astype(BF16), wt_ref[...], preferred_element_type=F32)
          + jnp.dot(r.astype(BF16), wb_ref[...], preferred_element_type=F32))
    o_ref[...] = x_ref[...] + m_ref[2:3, :] * dx


def _out0(ys, hz, hfb, xa, slab, d_skip, glu_w, glu_b, w_out, *, tiles_per_batch):
    t, d = xa.shape
    width = ys.shape[1]
    tm = TOKEN_TILE // 2
    last_row = slab.shape[0] - 1
    tpb = tiles_per_batch * (TOKEN_TILE // tm)
    const = lambda shape: pl.BlockSpec(shape, lambda i: tuple(0 for _ in shape))
    return pl.pallas_call(
        _out0_kernel,
        grid=(t // tm,),
        in_specs=[pl.BlockSpec((tm, width), lambda i: (i, 0)),
                  pl.BlockSpec((tm, width), lambda i: (i, 0)),
                  pl.BlockSpec((tm, width), lambda i: (i, 1)),
                  pl.BlockSpec((None, tm, width), lambda i: (0, i, 0)),
                  pl.BlockSpec((None, tm, width), lambda i: (1, i, 0)),
                  pl.BlockSpec((tm, d), lambda i: (i, 0)),
                  pl.BlockSpec((None, 8, d), lambda i: (jnp.minimum(i // tpb, last_row), 0, 0)),
                  const((1, width)), const((width, width)), const((1, width)),
                  const((width, d)), const((width, d))],
        out_specs=pl.BlockSpec((tm, d), lambda i: (i, 0)),
        out_shape=jax.ShapeDtypeStruct((t, d), F32),
        compiler_params=_params(("arbitrary",)),
        name="hybrid_out",
    )(ys, hz, hz, hfb, hfb, xa, slab, d_skip.reshape(1, width), glu_w.astype(BF16),
      glu_b.reshape(1, width), w_out[:width].astype(BF16), w_out[width:].astype(BF16))


def _store_token_rows(dst_ref, val):
    rows, d = val.shape
    per = d // V7X_LANES
    for c in range(per):
        dst_ref[pl.ds(c, rows, stride=per), :] = val[:, c * V7X_LANES:(c + 1) * V7X_LANES]


def _load_token_rows(src_ref, start, rows, per):
    return [src_ref[pl.ds(start * per + c, rows, stride=per), :] for c in range(per)]


def _route_kernel(x_ref, g_ref, m_ref, w_ref, b_ref, h_ref, r_ref, cnt_ref, carry_ref):
    @pl.when(pl.program_id(0) == 0)
    def _():
        carry_ref[...] = jnp.zeros_like(carry_ref)

    h = _norm_mod(x_ref[...], g_ref[...], m_ref, 3, 4)
    _store_token_rows(h_ref, h)
    h_hi = h.astype(BF16)
    h_lo = (h - h_hi.astype(F32)).astype(BF16)
    p = jnp.dot(h_hi, w_ref[...], preferred_element_type=F32)
    logits = (p[:, :ROUTE_LANES] + p[:, ROUTE_LANES:] + b_ref[...]
              + jnp.dot(h_lo, w_ref[:, :ROUTE_LANES], preferred_element_type=F32))
    lane = lax.broadcasted_iota(jnp.int32, logits.shape, 1).astype(F32)
    ninf = -jnp.inf
    big = float(ROUTE_LANES)
    lg = jnp.where(lane < MOE_GROUPS, logits, ninf)
    mg = jnp.max(lg, axis=-1, keepdims=True)
    gidx = jnp.min(jnp.where(lg == mg, lane, big), axis=-1, keepdims=True)
    p_top = 1.0 / jnp.sum(jnp.exp(lg - mg), axis=-1, keepdims=True)
    lo = MOE_GROUPS + gidx * MOE_PER_GROUP
    le = jnp.where(jnp.logical_and(lane >= lo, lane < lo + MOE_PER_GROUP), logits, ninf)
    v1 = jnp.max(le, axis=-1, keepdims=True)
    i1 = jnp.min(jnp.where(le == v1, lane, big), axis=-1, keepdims=True)
    le2 = jnp.where(lane == i1, ninf, le)
    v2 = jnp.max(le2, axis=-1, keepdims=True)
    i2 = jnp.min(jnp.where(le2 == v2, lane, big), axis=-1, keepdims=True)
    tt = jnp.exp(v2 - v1)
    g1 = p_top / (1.0 + tt)
    g2 = p_top * tt / (1.0 + tt)
    e1, e2 = i1 - MOE_GROUPS, i2 - MOE_GROUPS
    tm = h.shape[0]
    tri = (lax.broadcasted_iota(jnp.int32, (tm, tm), 1)
           < lax.broadcasted_iota(jnp.int32, (tm, tm), 0)).astype(BF16)
    hot1, hot2 = lane == e1, lane == e2
    before1 = jnp.dot(tri, hot1.astype(BF16), preferred_element_type=F32)
    before2 = jnp.dot(tri, hot2.astype(BF16), preferred_element_type=F32)
    tot1 = jnp.sum(hot1.astype(F32), axis=0, keepdims=True)
    tot2 = jnp.sum(hot2.astype(F32), axis=0, keepdims=True)
    carry = carry_ref[0:1, :]
    rank1 = jnp.sum(jnp.where(hot1, carry + before1, 0.0), axis=-1, keepdims=True)
    rank2 = jnp.sum(jnp.where(hot2, carry + tot1 + before2, 0.0), axis=-1, keepdims=True)
    carry = carry + tot1 + tot2
    carry_ref[0:1, :] = carry
    cnt_ref[...] = jnp.broadcast_to(carry, cnt_ref.shape)
    out = jnp.where(lane == 0, e1, 0.0)
    out = jnp.where(lane == 1, e2, out)
    out = jnp.where(lane == 2, g1, out)
    out = jnp.where(lane == 3, g2, out)
    out = jnp.where(lane == 4, rank1, out)
    out = jnp.where(lane == 5, rank2, out)
    r_ref[...] = out


def _route(xa, g, slab, wg, bg, we, be, *, tiles_per_batch):
    t, d = xa.shape
    tm = TOKEN_TILE
    per = d // V7X_LANES
    nl = MOE_GROUPS + MOE_EXPERTS
    wr = jnp.pad(jnp.concatenate([wg, we], axis=1), ((0, 0), (0, ROUTE_LANES - nl)))
    wr_hi = wr.astype(BF16)
    wr = jnp.concatenate([wr_hi, (wr - wr_hi.astype(F32)).astype(BF16)], axis=1)
    br =jnp.pad(jnp.concatenate([bg, be], axis=0), (0, ROUTE_LANES - nl)).reshape(1, ROUTE_LANES)
    last_row = slab.shape[0] - 1
    return pl.pallas_call(
        _route_kernel,
        grid=(t // tm,),
        in_specs=[pl.BlockSpec((tm, d), lambda i: (i, 0)),
                  pl.BlockSpec((1, d), lambda i: (0, 0)),
                  pl.BlockSpec((None, 8, d), lambda i: (jnp.minimum(i // tiles_per_batch, last_row), 0, 0)),
                  pl.BlockSpec((d, 2 * ROUTE_LANES), lambda i: (0, 0)),
                  pl.BlockSpec((1, ROUTE_LANES), lambda i: (0, 0))],
        out_specs=[pl.BlockSpec((tm * per, V7X_LANES), lambda i: (i, 0)),
                   pl.BlockSpec((tm, ROUTE_LANES), lambda i: (i, 0)),
                   pl.BlockSpec((8, ROUTE_LANES), lambda i: (0, 0))],
        out_shape=[jax.ShapeDtypeStruct((t * per, V7X_LANES), F32),
                   jax.ShapeDtypeStruct((t, ROUTE_LANES), F32),
                   jax.ShapeDtypeStruct((8, ROUTE_LANES), F32)],
        scratch_shapes=[pltpu.VMEM((8, ROUTE_LANES), F32)],
        compiler_params=_params(("arbitrary",)),
        name="moe_route",
    )(xa, g.reshape(1, d), slab, wr, br)


def _dispatch(route, counts):
    bm = MOE_BLOCK
    n_pairs = route.shape[0] * MOE_TOPK
    experts = route[:, 0:MOE_TOPK].astype(jnp.int32)
    rank = route[:, 4:4 + MOE_TOPK].astype(jnp.int32)
    cnt = counts[0, :MOE_EXPERTS].astype(jnp.int32)
    padded = (cnt + bm - 1) // bm * bm
    pad_end = jnp.cumsum(padded)
    dest = (pad_end - padded)[experts] + rank
    n_blocks = -(-n_pairs // bm) + MOE_EXPERTS
    first_slot = jnp.arange(n_blocks, dtype=jnp.int32) * bm
    block_e = jnp.minimum(jnp.sum((pad_end[None, :] <= first_slot[:, None]).astype(jnp.int32), axis=1),
                          MOE_EXPERTS - 1)
    n_used = (pad_end[-1] // bm).astype(jnp.int32).reshape(1)
    tail = jnp.maximum(pad_end - bm, 0).astype(jnp.int32)
    return dest, block_e, n_used, tail, n_blocks


def _scatter_kernel(tail_ref, nu_ref, d_ref, h_ref, xs_hbm, stage, zsrc, sem, *, n_steps, n_blocks, per):
    i = pl.program_id(0)
    tm = d_ref.shape[1] // MOE_TOPK
    zrows = zsrc.shape[0]

    def fill_at(slot):
        return pltpu.make_async_copy(zsrc, xs_hbm.at[pl.ds(slot * per, zrows), :], sem.at[2])

    def fill(e):
        return fill_at(tail_ref[e])

    def fill_unused(wait):
        def body(blk, c):
            cp = fill_at(blk * MOE_BLOCK)
            cp.wait() if wait else cp.start()
            return c
        lax.fori_loop(nu_ref[0], n_blocks, body, 0)

    def copy(step, r, k):
        return pltpu.make_async_copy(stage.at[lax.rem(step, 2), pl.ds(r * per, per), :],
                                     xs_hbm.at[pl.ds(d_ref[0, MOE_TOPK * r + k] * per, per), :],
                                     sem.at[lax.rem(step, 2)])

    def wait_step(step):
        for r in range(tm):
            for k in range(MOE_TOPK):
                copy(step, r, k).wait()

    @pl.when(i == 0)
    def _():
        zsrc[...] = jnp.zeros_like(zsrc)
        for e in range(MOE_EXPERTS):
            fill(e).start()
        fill_unused(False)
        for e in range(MOE_EXPERTS):
            fill(e).wait()
        fill_unused(True)

    stage[lax.rem(i, 2)] = h_ref[...]
    for r in range(tm):
        for k in range(MOE_TOPK):
            copy(i, r, k).start()

    @pl.when(i > 0)
    def _():
        wait_step(i - 1)

    @pl.when(i == n_steps - 1)
    def _():
        wait_step(i)


def _moe_scatter(h_rows, dest, tail, n_used, n_blocks, per):
    t = dest.shape[0]
    tm = TOKEN_TILE // 2
    n_steps = t // tm
    bm = MOE_BLOCK
    d3 = dest.reshape(n_steps, 1, MOE_TOPK * tm)
    grid_spec = pltpu.PrefetchScalarGridSpec(
        num_scalar_prefetch=2,
        grid=(n_steps,),
        in_specs=[pl.BlockSpec((None, 1, MOE_TOPK * tm), lambda i, tl, nu: (i, 0, 0),
                               memory_space=pltpu.SMEM),
                  pl.BlockSpec((tm * per, V7X_LANES), lambda i, tl, nu: (i, 0))],
        out_specs=pl.BlockSpec(memory_space=pl.ANY),
        scratch_shapes=[pltpu.VMEM((2, tm * per, V7X_LANES), F32),
                        pltpu.VMEM((bm * per, V7X_LANES), F32),
                        pltpu.SemaphoreType.DMA((3,))],
    )
    return pl.pallas_call(
        functools.partial(_scatter_kernel, n_steps=n_steps, n_blocks=n_blocks, per=per),
        grid_spec=grid_spec,
        out_shape=jax.ShapeDtypeStruct((n_blocks * bm * per, V7X_LANES), F32),
        compiler_params=_params(("arbitrary",)),
        name="moe_scatter",
    )(tail, n_used, d3, h_rows)


def _token_copy(src_hbm, tok, dst, slot, r, sem, per):
    return pltpu.make_async_copy(src_hbm.at[pl.ds(tok * per, per), :],
                                 dst.at[slot, pl.ds(r * per, per), :], sem.at[slot])


def _moe_kernel(be_ref, nu_ref, xs_ref, w1_ref, w3_ref, w2_ref, o_ref, w1b, w3b, w2b, prev_ref):
    b = pl.program_id(0)
    bm = MOE_BLOCK
    per = w1_ref.shape[0] // V7X_LANES

    @pl.when(b == 0)
    def _():
        prev_ref[0] = -1

    @pl.when(b < nu_ref[0])
    def _():
        e = be_ref[b]

        @pl.when(e != prev_ref[0])
        def _():
            w1b[...] = w1_ref[...].astype(BF16)
            w3b[...] = w3_ref[...].astype(BF16)
            w2b[...] = w2_ref[...].astype(BF16)
            prev_ref[0] = e

        x = jnp.concatenate(_load_token_rows(xs_ref, 0, bm, per), axis=1).astype(BF16)
        h1 = jnp.dot(x, w1b[...], preferred_element_type=F32)
        h3 = jnp.dot(x, w3b[...], preferred_element_type=F32)
        act = (_silu(h1) * h3).astype(BF16)
        _store_token_rows(o_ref, jnp.dot(act, w2b[...], preferred_element_type=F32))

    @pl.when(b >= nu_ref[0])
    def _():
        o_ref[...] = jnp.zeros_like(o_ref)


def _moe_experts(xs_rows, block_e, n_used, n_blocks, w1, w3, w2, layer):
    d, ff = w1.shape[2], w1.shape[3]
    per = d // V7X_LANES
    bm = MOE_BLOCK
    blk = lambda b, be, nu: jnp.minimum(b, jnp.maximum(nu[0] - 1, 0))
    grid_spec = pltpu.PrefetchScalarGridSpec(
        num_scalar_prefetch=2,
        grid=(n_blocks,),
        in_specs=[pl.BlockSpec((bm * per, V7X_LANES), lambda b, be, nu: (blk(b, be, nu), 0)),
                  pl.BlockSpec((None, None, d, ff), lambda b, be, nu: (layer, be[blk(b, be, nu)], 0, 0)),
                  pl.BlockSpec((None, None, d, ff), lambda b, be, nu: (layer, be[blk(b, be, nu)], 0, 0)),
                  pl.BlockSpec((None, None, ff, d), lambda b, be, nu: (layer, be[blk(b, be, nu)], 0, 0))],
        out_specs=pl.BlockSpec((bm * per, V7X_LANES), lambda b, be, nu: (b, 0)),
        scratch_shapes=[pltpu.VMEM((d, ff), BF16),
                        pltpu.VMEM((d, ff), BF16),
                        pltpu.VMEM((ff, d), BF16),
                        pltpu.SMEM((1,), jnp.int32)],
    )
    return pl.pallas_call(
        _moe_kernel,
        grid_spec=grid_spec,
        out_shape=jax.ShapeDtypeStruct((n_blocks * bm * per, V7X_LANES), F32),
        compiler_params=_params(("arbitrary",)),
        name="moe_experts",
    )(block_e, n_used, xs_rows, w1, w3, w2)


def _combine_kernel(d_ref, dn_ref, yb_hbm, x_ref, r_ref, m_ref, *rest, n_tiles, n_lat_tiles):
    if n_lat_tiles < n_tiles:
        lat_ref, ctx_ref, ybuf, sem = rest
    else:
        lat_ref, ybuf, sem = rest
        ctx_ref = None
    i = pl.program_id(0)
    tm, d = x_ref.shape
    per = d // V7X_LANES
    nrow = MOE_TOPK * tm
    slot = lax.rem(i, 2)

    def start_rows(idx_ref, s):
        for r in range(nrow):
            _token_copy(yb_hbm, idx_ref[0, r], ybuf, s, r, sem, per).start()

    @pl.when(i == 0)
    def _():
        start_rows(d_ref, 0)

    @pl.when(i + 1 < n_tiles)
    def _():
        start_rows(dn_ref, 1 - slot)

    for r in range(nrow):
        _token_copy(yb_hbm, 0, ybuf, slot, r, sem, per).wait()
    r = r_ref[...]
    g0, g1 = r[:, 2:3], r[:, 3:4]
    y0 = _load_token_rows(ybuf.at[slot], 0, tm, per)
    y1 = _load_token_rows(ybuf.at[slot], tm, tm, per)
    def emit(o_ref):
        for c in range(per):
            sl = slice(c * V7X_LANES, (c + 1) * V7X_LANES)
            o_ref[pl.ds(c, tm, stride=per), :] = (
                x_ref[:, sl] + m_ref[5:6, sl] * (g0 * y0[c] + g1 * y1[c]))

    if ctx_ref is None:
        emit(lat_ref)
    else:
        pl.when(i < n_lat_tiles)(lambda: emit(lat_ref))
        pl.when(i >= n_lat_tiles)(lambda: emit(ctx_ref))


def _moe_combine(yb_rows, dest, xa, route, slab, *, tiles_per_batch, t_lat):
    t, d = xa.shape
    per = d // V7X_LANES
    tm = TOKEN_TILE // 2
    n_tiles = t // tm
    n_lat_tiles = t_lat // tm
    row_block = pl.BlockSpec((tm * per, V7X_LANES), lambda i: (jnp.minimum(i, n_lat_tiles - 1), 0))
    out_specs = [row_block]
    out_shape = [jax.ShapeDtypeStruct((t_lat * per, V7X_LANES), F32)]
    if n_lat_tiles < n_tiles:
        out_specs.append(pl.BlockSpec((tm * per, V7X_LANES),
                                      lambda i: (jnp.maximum(i - n_lat_tiles, 0), 0)))
        out_shape.append(jax.ShapeDtypeStruct(((t - t_lat) * per, V7X_LANES), F32))
    tpb = tiles_per_batch * (TOKEN_TILE // tm)
    last_row = slab.shape[0] - 1
    d3 = dest.reshape(n_tiles, tm, MOE_TOPK).transpose(0, 2, 1).reshape(n_tiles, 1, MOE_TOPK * tm)
    smem_block = lambda fn: pl.BlockSpec((None, 1, MOE_TOPK * tm), fn, memory_space=pltpu.SMEM)
    return pl.pallas_call(
        functools.partial(_combine_kernel, n_tiles=n_tiles, n_lat_tiles=n_lat_tiles),
        grid=(n_tiles,),
        in_specs=[smem_block(lambda i: (i, 0, 0)),
                  smem_block(lambda i: (jnp.minimum(i + 1, n_tiles - 1), 0, 0)),
                  pl.BlockSpec(memory_space=pl.ANY),
                  pl.BlockSpec((tm, d), lambda i: (i, 0)),
                  pl.BlockSpec((tm, ROUTE_LANES), lambda i: (i, 0)),
                  pl.BlockSpec((None, 8, d), lambda i: (jnp.minimum(i // tpb, last_row), 0, 0))],
        out_specs=out_specs,
        out_shape=out_shape,
        scratch_shapes=[pltpu.VMEM((2, MOE_TOPK * tm * per, V7X_LANES), F32),
                        pltpu.SemaphoreType.DMA((2,))],
        compiler_params=_params(("arbitrary",)),
        name="moe_combine",
    )(d3, d3, yb_rows, xa, route, slab)


def _moe_layer(xa, g, slab, wg, bg, we, be, w1, w3, w2, *, layer, tiles_per_batch, t_lat):
    per = xa.shape[1] // V7X_LANES
    h_rows, route, counts = _route(xa, g, slab, wg, bg, we, be, tiles_per_batch=tiles_per_batch)
    dest, block_e, n_used, tail, n_blocks = _dispatch(route, counts)
    xs_rows = _moe_scatter(h_rows, dest, tail, n_used, n_blocks, per)
    yb_rows = _moe_experts(xs_rows, block_e, n_used, n_blocks, w1, w3, w2, layer)
    return _moe_combine(yb_rows, dest, xa, route, slab, tiles_per_batch=tiles_per_batch, t_lat=t_lat)


def _ssd_kernel(*refs, rev, n_ctx_chunks, n_lat_chunks):
    if rev:
        (xact_ref, bact_ref, cact_ref, dt_ref, dtb_ref, a_ref, yf_ref, z_ref, ng_ref,
         o_ref, xpair, ypair, cbs, bts, cs, acst, dtt, wtt, tott, state) = refs
    else:
        (xs_ref, xsp_ref, xsn_ref, bm_ref, bmp_ref, bmn_ref, cm_ref, cmp_ref, cmn_ref,
         dt_ref, cwx_ref, cbx_ref, cwb_ref, cbb_ref, cwc_ref, cbc_ref, dtb_ref, a_ref, dsk_ref,
         o_ref, xact_ref, bact_ref, cact_ref,
         padx, padb, padc, xpair, ypair, cbs, bts, cs, acst, dtt, wtt, tott, state) = refs
    k = pl.program_id(1)
    q = SSD_CHUNK
    nst = SSD_STATE
    npair = xact_ref.shape[1] // V7X_LANES
    ngrp = bact_ref.shape[1] // nst
    pairs_per_group = npair // ngrp
    if rev:
        bmat, cmat = bact_ref[...], cact_ref[...]
        for p in range(npair):
            xpair[p] = xact_ref[:, p * V7X_LANES:(p + 1) * V7X_LANES]
    else:
        in_ctx = k < n_ctx_chunks
        jc, jl = k, k - n_ctx_chunks
        first = jnp.where(in_ctx, jc == 0, jl == 0)
        last = jnp.where(in_ctx, jc == n_ctx_chunks - 1, jl == n_lat_chunks - 1)

        def conv_silu(pad_ref, main, prev, nxt, cw_ref, cb_ref):
            return _silu(_dwconv_tile(pad_ref, main[...], prev[...], nxt[...], first, last, cw_ref,
                                      cb_ref[...], q))

        xs = conv_silu(padx, xs_ref, xsp_ref, xsn_ref, cwx_ref, cbx_ref)
        bmat = conv_silu(padb, bm_ref, bmp_ref, bmn_ref, cwb_ref, cbb_ref)
        cmat = conv_silu(padc, cm_ref, cmp_ref, cmn_ref, cwc_ref, cbc_ref)
        bact_ref[...] = bmat
        cact_ref[...] = cmat
        for p in range(npair):
            xb = xs[:, p * V7X_LANES:(p + 1) * V7X_LANES].astype(BF16)
            xpair[p] = xb
            xact_ref[:, p * V7X_LANES:(p + 1) * V7X_LANES] = xb
    for g in range(ngrp):
        bg = bmat[:, g * nst:(g + 1) * nst]
        cg = cmat[:, g * nst:(g + 1) * nst].astype(BF16)
        cs[g] = cg
        cbs[g] = lax.dot_general(cg, bg.astype(BF16), (((1,), (1,)), ((), ())),
                                 preferred_element_type=F32)
        bts[g] = bg.T
    dtr = dt_ref[...] + dtb_ref[...]
    dt = jnp.maximum(dtr, 0.0) + jnp.log(1.0 + jnp.exp(-jnp.abs(dtr)))
    adt = dt * a_ref[...]
    row = lax.broadcasted_iota(jnp.int32, (q, q), 0)
    col = lax.broadcasted_iota(jnp.int32, (q, q), 1)
    causal = (row <= col) if rev else (row >= col)
    acs = jnp.dot(causal.astype(F32), adt, preferred_element_type=F32, precision=HIGHEST)
    acs_t = acs.T
    tot = acs_t[:, 0:1] if rev else acs_t[:, q - 1:q]
    acst[...] = acs_t
    dtt[...] = dt.T
    wtt[...] = jnp.exp(tot - acs_t) * dt.T
    tott[...] = jnp.broadcast_to(jnp.exp(tot), (tott.shape[0], q))

    @pl.when(k == 0)
    def _():
        state[...] = jnp.zeros_like(state)

    hoff = SSD_HEAD_DIM if rev else 0
    lane = lax.broadcasted_iota(jnp.int32, (q, V7X_LANES), 1)
    left = lane < SSD_HEAD_DIM

    def pair_body(p, c):
        g = p // pairs_per_group
        h0 = hoff + 2 * p
        x = xpair[p]
        zero = jnp.zeros_like(x)
        xbd = jnp.concatenate([jnp.where(left, x, zero), jnp.where(left, zero, x)], axis=0)
        cb = cbs[g]
        st = state[p]
        mms, wbs, scales, decs = [], [], [], []
        for m in range(2):
            arow = acst[pl.ds(h0 + m, 1), :]
            acol = jnp.broadcast_to(arow, (q, q)).T
            seg = jnp.where(causal, jnp.exp(acol - arow), 0.0) * dtt[pl.ds(h0 + m, 1), :]
            mms.append((cb * seg).astype(BF16))
            scales.append(jnp.exp(acol))
            wbs.append((bts[g] * wtt[pl.ds(h0 + m, 1), :]).astype(BF16))
            decs.append(tott[pl.ds(h0 + m, 1), :])
        intra = jnp.dot(jnp.concatenate(mms, axis=1), xbd, preferred_element_type=F32)
        new = jnp.dot(jnp.concatenate(wbs, axis=1), xbd, preferred_element_type=F32)
        inter = jnp.dot(cs[g], st.astype(BF16), preferred_element_type=F32)
        ypair[p] = intra + inter * jnp.where(left, scales[0], scales[1])
        state[p] = st * jnp.where(left, decs[0], decs[1]) + new
        return c

    lax.fori_loop(0, npair, pair_body, 0, unroll=SSD_PAIR_UNROLL)
    if not rev:
        for p in range(npair):
            sl = slice(p * V7X_LANES, (p + 1) * V7X_LANES)
            o_ref[:, sl] = ypair[p] + dsk_ref[:, sl] * xs[:, sl]
        return
    sq = jnp.zeros((q, V7X_LANES), F32)
    for p in range(npair):
        sl = slice(p * V7X_LANES, (p + 1) * V7X_LANES)
        gated = (ypair[p] + yf_ref[:, sl]) * _silu(z_ref[:, sl])
        ypair[p] = gated
        sq = sq + gated * gated
    scale = lax.rsqrt(jnp.sum(sq, axis=-1, keepdims=True) / (npair * V7X_LANES) + RMS_EPS)
    for p in range(npair):
        sl = slice(p * V7X_LANES, (p + 1) * V7X_LANES)
        o_ref[:, sl] = (ypair[p] * scale * ng_ref[:, sl]).astype(BF16)


def _ssd_scan(hz, conv_w, conv_b, dt_bias, a_neg, d_skip_lanes, *, rev, nb, n_lat, n_ctx, inner,
              fwd=None, norm_g=None):
    t = hz.shape[0]
    q = SSD_CHUNK
    bc = SSD_GROUPS * SSD_STATE
    ncc, ncl = n_ctx // q, n_lat // q
    hb = q // HALO
    n_halo_blocks = t // HALO
    nheads2 = dt_bias.size
    assert nheads2 == V7X_LANES and q == V7X_LANES and SSD_STATE == V7X_LANES
    npair = inner // V7X_LANES
    xs_col = inner // inner
    b_col = 2 * inner // bc
    c_col = b_col + 1
    dt_col = (2 * inner + 2 * bc) // nheads2

    def chunk(b, k):
        kk = k - ncc
        if rev:
            jc, jl = ncc - 1 - k, ncl - 1 - kk
        else:
            jc, jl = k, kk
        return jnp.where(k < ncc, nb * ncl + b * ncc + jc, b * ncl + jl)

    def main(width, colb):
        return pl.BlockSpec((q, width), lambda b, k: (chunk(b, k), colb))

    def prev(width, colb):
        return pl.BlockSpec((HALO, width), lambda b, k: (jnp.maximum(chunk(b, k) * hb - 1, 0), colb))

    def nxt(width, colb):
        return pl.BlockSpec((HALO, width),
                            lambda b, k: (jnp.minimum(chunk(b, k) * hb + hb, n_halo_blocks - 1), colb))

    const = lambda shape: pl.BlockSpec(shape, lambda b, k: (0, 0))
    shared_scratch = [pltpu.VMEM((npair, q, V7X_LANES), BF16),
                      pltpu.VMEM((npair, q, V7X_LANES), F32),
                      pltpu.VMEM((SSD_GROUPS, q, q), F32),
                      pltpu.VMEM((SSD_GROUPS, SSD_STATE, q), F32),
                      pltpu.VMEM((SSD_GROUPS, q, SSD_STATE), BF16),
                      pltpu.VMEM((nheads2, q), F32),
                      pltpu.VMEM((nheads2, q), F32),
                      pltpu.VMEM((nheads2, q), F32),
                      pltpu.VMEM((nheads2, q), F32),
                      pltpu.VMEM((npair, SSD_STATE, V7X_LANES), F32)]
    kern = functools.partial(_ssd_kernel, rev=rev, n_ctx_chunks=ncc, n_lat_chunks=ncl)
    dtb, an = dt_bias.reshape(1, nheads2), a_neg.reshape(1, nheads2)
    if rev:
        y_fwd, xact, bact, cact = fwd
        return pl.pallas_call(
            kern,
            grid=(nb, ncc + ncl),
            in_specs=[main(inner, 0), main(bc, 0), main(bc, 0), main(nheads2, dt_col),
                      const((1, nheads2)), const((1, nheads2)),
                      main(inner, 0), main(inner, 0), const((1, inner))],
            out_specs=main(inner, 0),
            out_shape=jax.ShapeDtypeStruct((t, inner), BF16),
            scratch_shapes=shared_scratch,
            compiler_params=_params(("arbitrary", "arbitrary")),
            name="ssd_scan_bwd",
        )(xact, bact, cact, hz, dtb, an, y_fwd, hz, norm_g.reshape(1, inner))
    cw = jnp.pad(conv_w, ((0, 8 - CONV_W), (0, 0)))
    cb = conv_b.reshape(1, -1)
    return pl.pallas_call(
        kern,
        grid=(nb, ncc + ncl),
        in_specs=[main(inner, xs_col), prev(inner, xs_col), nxt(inner, xs_col),
                  main(bc, b_col), prev(bc, b_col), nxt(bc, b_col),
                  main(bc, c_col), prev(bc, c_col), nxt(bc, c_col),
                  main(nheads2, dt_col),
                  const((8, inner)), const((1, inner)),
                  const((8, bc)), const((1, bc)),
                  const((8, bc)), const((1, bc)),
                  const((1, nheads2)), const((1, nheads2)), const((1, inner))],
        out_specs=[main(inner, 0), main(inner, 0), main(bc, 0), main(bc, 0)],
        out_shape=[jax.ShapeDtypeStruct((t, inner), F32), jax.ShapeDtypeStruct((t, inner), BF16),
                   jax.ShapeDtypeStruct((t, bc), F32), jax.ShapeDtypeStruct((t, bc), F32)],
        scratch_shapes=[pltpu.VMEM((q + 2 * HALO, inner), F32),
                        pltpu.VMEM((q + 2 * HALO, bc), F32),
                        pltpu.VMEM((q + 2 * HALO, bc), F32)] + shared_scratch,
        compiler_params=_params(("arbitrary", "arbitrary")),
        name="ssd_scan_fwd",
    )(hz, hz, hz, hz, hz, hz, hz, hz, hz, hz,
      cw[:, :inner], cb[:, :inner], cw[:, inner:inner + bc], cb[:, inner:inner + bc],
      cw[:, inner + bc:], cb[:, inner + bc:], dtb, an, d_skip_lanes)


def _out1_kernel(yn_ref, x_ref, m_ref, w_ref, o_ref):
    o_ref[...] = x_ref[...] + m_ref[2:3, :] * jnp.dot(yn_ref[...], w_ref[...], preferred_element_type=F32)


def _out1(yn, xc, slab, w_out, *, t_lat, tiles_per_batch):
    t, d = t_lat, xc.shape[1]
    inner = yn.shape[1]
    tm = TOKEN_TILE // 2
    tpb = tiles_per_batch * (TOKEN_TILE // tm)
    last_row = slab.shape[0] - 1
    return pl.pallas_call(
        _out1_kernel,
        grid=(t // tm,),
        in_specs=[pl.BlockSpec((tm, inner), lambda i: (i, 0)),
                  pl.BlockSpec((tm, d), lambda i: (i, 0)),
                  pl.BlockSpec((None, 8, d), lambda i: (jnp.minimum(i // tpb, last_row), 0, 0)),
                  pl.BlockSpec((inner, d), lambda i: (0, 0))],
        out_specs=pl.BlockSpec((tm, d), lambda i: (i, 0)),
        out_shape=jax.ShapeDtypeStruct((t, d), F32),
        compiler_params=_params(("arbitrary",)),
        name="ssd_out",
    )(yn, xc, slab, w_out.astype(BF16))


def _colmajor_kernel(lat_ref, ctx_ref, o_ref, *, n_lat_steps):
    i = pl.program_id(0)
    n_tok = o_ref.shape[0]
    per = o_ref.shape[1] // V7X_LANES

    def emit(src2d):
        for c, piece in enumerate(_load_token_rows(src2d, 0, n_tok, per)):
            o_ref[:, c * V7X_LANES:(c + 1) * V7X_LANES] = piece

    pl.when(i < n_lat_steps)(lambda: emit(lat_ref.reshape(n_tok * per, V7X_LANES)))
    pl.when(i >= n_lat_steps)(lambda: emit(ctx_ref))


def _to_column_major(lat_rows, ctx_rows, *, nb, rows, d):
    per = d // V7X_LANES
    n_lat_steps = nb * GRID_W
    n_ctx_steps = ctx_rows.shape[0] // (rows * per)
    lat4 = lat_rows.reshape(nb, rows, GRID_W * per, V7X_LANES)

    def lat_index(i):
        j = jnp.minimum(i, n_lat_steps - 1)
        return (j // GRID_W, 0, j % GRID_W, 0)

    return pl.pallas_call(
        functools.partial(_colmajor_kernel, n_lat_steps=n_lat_steps),
        grid=(n_lat_steps + n_ctx_steps,),
        in_specs=[pl.BlockSpec((None, rows, per, V7X_LANES), lat_index),
                  pl.BlockSpec((rows * per, V7X_LANES), lambda i: (jnp.maximum(i - n_lat_steps, 0), 0))],
        out_specs=pl.BlockSpec((rows, d), lambda i: (i, 0)),
        out_shape=jax.ShapeDtypeStruct(((n_lat_steps + n_ctx_steps) * rows, d), F32),
        compiler_params=_params(("arbitrary",)),
        name="to_column_major",
    )(lat4, ctx_rows)


def _final_kernel(x_ref, g_ref, o_ref):
    d = o_ref.shape[1]
    per = d // V7X_LANES
    nr = x_ref.shape[1] // per
    src = x_ref.reshape(GRID_W * nr * per, V7X_LANES)
    for rr in range(nr):
        pieces = [src[pl.ds(rr * per + c, GRID_W, stride=nr * per), :] for c in range(per)]
        ss = pieces[0] * pieces[0]
        for p in pieces[1:]:
            ss = ss + p * p
        scale = lax.rsqrt(jnp.sum(ss, axis=-1, keepdims=True) / d + RMS_EPS)
        for c, p in enumerate(pieces):
            sl = slice(c * V7X_LANES, (c + 1) * V7X_LANES)
            o_ref[rr * GRID_W:(rr + 1) * GRID_W, sl] = p * scale * g_ref[:, sl]


def _final_norm_row_major(xl_rows, g, *, nb, rows, d):
    per = d // V7X_LANES
    nr = math.gcd(rows, FINAL_ROWS_PER_STEP)
    x4 = xl_rows.reshape(nb, GRID_W, rows * per, V7X_LANES)
    return pl.pallas_call(
        _final_kernel,
        grid=(nb, rows // nr),
        in_specs=[pl.BlockSpec((None, GRID_W, nr * per, V7X_LANES), lambda b, r: (b, 0, r, 0)),
                  pl.BlockSpec((1, d), lambda b, r: (0, 0))],
        out_specs=pl.BlockSpec((nr * GRID_W, d), lambda b, r: (b * (rows // nr) + r, 0)),
        out_shape=jax.ShapeDtypeStruct((nb * rows * GRID_W, d), F32),
        compiler_params=_params(("arbitrary", "arbitrary")),
        name="final_norm",
    )(x4, g.reshape(1, d))


def kernel(x, c, ctx, c_ctx, norm_mix_g, norm_ffn_g, mod_w, mod_b, hy_w_in, hy_w_out, s5_lam_re, s5_lam_im, s5_log_dt, s5_b_re, s5_b_im, s5_c_re, s5_c_im, s5_d, s5_glu_w, s5_glu_b, lru_conv_w, lru_conv_b, lru_wa, lru_ba, lru_wx, lru_bx, lru_lam, ssd_w_in, ssd_conv_w, ssd_conv_b, ssd_dt_bias, ssd_a_log, ssd_d, ssd_norm_g, ssd_w_out, moe_wg, moe_bg, moe_we, moe_be, moe_w1, moe_w3, moe_w2, final_norm_g):
    nb, n_lat, d = x.shape
    n_ctx = ctx.shape[1]
    depth = mod_w.shape[0]
    assert depth == 2 and nb + 1 <= 8
    assert n_lat % TOKEN_TILE == 0 and (nb * n_ctx) % TOKEN_TILE == 0
    t_lat = nb * n_lat
    rows = n_lat // GRID_W
    tiles_per_batch = n_lat // TOKEN_TILE
    n_cond = nb + 1

    xa = jnp.concatenate([x.reshape(t_lat, d), ctx.reshape(nb * n_ctx, d)], axis=0)
    cvec = jnp.zeros((8, d), F32).at[:nb].set(c).at[nb].set(c_ctx)
    mods = _mod_vectors(cvec, mod_w, mod_b)
    slab0 = _mod_slab(mods[0], n_cond, d)
    slab1 = _mod_slab(mods[1], n_cond, d)

    width = s5_d.shape[1]
    hz = _norm_matmul(xa, norm_mix_g[0], slab0, hy_w_in[0].astype(BF16), shift_row=0, scale_row=1,
                      tn=width, n_lat=n_lat)
    s5w = _s5_weights(s5_lam_re[0], s5_lam_im[0], s5_log_dt[0], s5_b_re[0], s5_b_im[0],
                      s5_c_re[0], s5_c_im[0])
    ys = _s5_mixer(hz, s5w, width=width, nb=nb, n_lat=n_lat, n_ctx=n_ctx)
    sp = jax.nn.softplus(-lru_lam[0].astype(F32)).reshape(2, 1, width)
    hfb = _lru_scan(hz, 2, lru_conv_w[0], lru_conv_b[0], _block_diag_gates(lru_wa[0]),
                    _block_diag_gates(lru_wx[0]), lru_ba[0].reshape(2, 1, width),
                    lru_bx[0].reshape(2, 1, width), sp, nb=nb, n_lat=n_lat, n_ctx=n_ctx)
    xa = _out0(ys, hz, hfb, xa, slab0, s5_d[0], s5_glu_w[0], s5_glu_b[0], hy_w_out[0],
               tiles_per_batch=tiles_per_batch)
    lat_rows, ctx_rows = _moe_layer(xa, norm_ffn_g[0], slab0, moe_wg[0], moe_bg[0], moe_we[0],
                                    moe_be[0], moe_w1, moe_w3, moe_w2, layer=0,
                                    tiles_per_batch=tiles_per_batch, t_lat=t_lat)

    xc = _to_column_major(lat_rows, ctx_rows, nb=nb, rows=rows, d=d)
    inner = ssd_norm_g.shape[1]
    hz1 = _norm_matmul(xc, norm_mix_g[1], slab1, ssd_w_in[0].astype(BF16), shift_row=0, scale_row=1,
                       tn=ssd_w_in.shape[2] // 9, n_lat=n_lat)
    a_neg = -jnp.exp(ssd_a_log[0].astype(F32))
    dsk = jnp.repeat(ssd_d[0], SSD_HEAD_DIM).reshape(1, inner)
    ssd_args = dict(nb=nb, n_lat=n_lat, n_ctx=n_ctx, inner=inner)
    fwd = _ssd_scan(hz1, ssd_conv_w[0], ssd_conv_b[0], ssd_dt_bias[0], a_neg, dsk, rev=False, **ssd_args)
    yn = _ssd_scan(hz1, ssd_conv_w[0], ssd_conv_b[0], ssd_dt_bias[0], a_neg, dsk, rev=True,
                   fwd=fwd, norm_g=ssd_norm_g[0], **ssd_args)
    xl = _out1(yn, xc, slab1, ssd_w_out[0], t_lat=t_lat, tiles_per_batch=tiles_per_batch)
    (xl_rows,) = _moe_layer(xl, norm_ffn_g[1], slab1, moe_wg[1], moe_bg[1], moe_we[1], moe_be[1],
                            moe_w1, moe_w3, moe_w2, layer=1, tiles_per_batch=tiles_per_batch,
                            t_lat=t_lat)
    out = _final_norm_row_major(xl_rows, final_norm_g, nb=nb, rows=rows, d=d)
    return out.reshape(nb, n_lat, d)
```

```python
import functools
import math

import jax
import jax.numpy as jnp
from jax import lax
from jax.experimental import pallas as pl
from jax.experimental.pallas import tpu as pltpu

F32 = jnp.float32
BF16 = jnp.bfloat16
HIGHEST = lax.Precision.HIGHEST

GRID_W = 64
N_MOD = 6
RMS_EPS = 1e-6
CONV_W = 4
CONV_PAD_LEFT = CONV_W // 2
S5_GROUP = 16
S5_STATE = 64
LRU_HEADS = 16
LRU_C = 8.0
SSD_HEAD_DIM = 64
SSD_GROUPS = 8
SSD_STATE = 128
SSD_CHUNK = 128
MOE_GROUPS = 4
MOE_PER_GROUP = 8
MOE_EXPERTS = MOE_GROUPS * MOE_PER_GROUP
MOE_TOPK = 2

V7X_LANES = 128
V7X_SUBLANES = 8
V7X_MXU_DIM = 256
V7X_VMEM_LIMIT_BYTES = 60000 * 1024

TOKEN_TILE = 512
IN_PROJ_TILE = 1024
S5_CHUNK = V7X_MXU_DIM // S5_GROUP
S5_PAIR_LANES = 2 * S5_STATE
S5_SCAN_LANES = 512
LRU_TILE = 256
LRU_GATE_BLOCK = V7X_MXU_DIM
HALO = V7X_SUBLANES
MOE_BLOCK = 256
SSD_PAIR_UNROLL = 4
FINAL_ROWS_PER_STEP = 4
ROUTE_LANES = V7X_LANES


def _params(sem):
    return pltpu.CompilerParams(dimension_semantics=sem, vmem_limit_bytes=V7X_VMEM_LIMIT_BYTES)


def _silu(v):
    return v * jax.nn.sigmoid(v)


def _gelu(v):
    return jax.nn.gelu(v, approximate=True)


def _mod_kernel(c_ref, w_ref, b_ref, o_ref):
    s = _silu(c_ref[...])
    o_ref[...] = jnp.dot(s, w_ref[...], preferred_element_type=F32, precision=HIGHEST) + b_ref[...]


def _mod_vectors(cvec, mod_w, mod_b):
    depth, d, n = mod_w.shape
    tn = n // 8
    return pl.pallas_call(
        _mod_kernel,
        grid=(depth, n // tn),
        in_specs=[pl.BlockSpec((8, d), lambda l, j: (0, 0)),
                  pl.BlockSpec((None, d, tn), lambda l, j: (l, 0, j)),
                  pl.BlockSpec((None, 1, tn), lambda l, j: (l, 0, j))],
        out_specs=pl.BlockSpec((None, 8, tn), lambda l, j: (l, 0, j)),
        out_shape=jax.ShapeDtypeStruct((depth, 8, n), F32),
        compiler_params=_params(("arbitrary", "arbitrary")),
        name="mod_vectors",
    )(cvec, mod_w, mod_b.reshape(depth, 1, n))


def _mod_slab(mods_layer, n_rows, d):
    m = mods_layer[:n_rows].reshape(n_rows, N_MOD, d)
    return jnp.pad(m, ((0, 0), (0, 8 - N_MOD), (0, 0)))


def _norm_mod(x, g, m_ref, shift_row, scale_row):
    ms = jnp.mean(x * x, axis=-1, keepdims=True)
    y = x * lax.rsqrt(ms + RMS_EPS) * g
    return y * (1.0 + m_ref[scale_row:scale_row + 1, :]) + m_ref[shift_row:shift_row + 1, :]


def _norm_mm_kernel(x_ref, g_ref, m_ref, w_ref, o_ref, xn_ref, *, shift_row, scale_row):
    @pl.when(pl.program_id(1) == 0)
    def _():
        xn_ref[...] = _norm_mod(x_ref[...], g_ref[...], m_ref, shift_row, scale_row).astype(BF16)

    o_ref[...] = jnp.dot(xn_ref[...], w_ref[...], preferred_element_type=F32)


def _norm_matmul(xa, g, slab, w, *, shift_row, scale_row, tn, n_lat):
    t, d = xa.shape
    n = w.shape[1]
    tm = math.gcd(IN_PROJ_TILE, n_lat)
    tiles_per_batch = n_lat // tm
    last_row = slab.shape[0] - 1
    return pl.pallas_call(
        functools.partial(_norm_mm_kernel, shift_row=shift_row, scale_row=scale_row),
        grid=(pl.cdiv(t, tm), n // tn),
        in_specs=[pl.BlockSpec((tm, d), lambda i, j: (i, 0)),
                  pl.BlockSpec((1, d), lambda i, j: (0, 0)),
                  pl.BlockSpec((None, 8, d),
                               lambda i, j: (jnp.minimum(i // tiles_per_batch, last_row), 0, 0)),
                  pl.BlockSpec((d, tn), lambda i, j: (0, j))],
        out_specs=pl.BlockSpec((tm, tn), lambda i, j: (i, j)),
        out_shape=jax.ShapeDtypeStruct((t, n), F32),
        scratch_shapes=[pltpu.VMEM((tm, d), BF16)],
        compiler_params=_params(("arbitrary", "arbitrary")),
        name="norm_matmul",
    )(xa, g.reshape(1, d), slab, w)


def _s5_weights(lam_re, lam_im, log_dt, b_re, b_im, c_re, c_im):
    q = S5_CHUNK
    ngrp, nst = lam_re.shape[1], lam_re.shape[2]
    nch = b_re.shape[-1]
    lr, li = lam_re.astype(F32), lam_im.astype(F32)
    dt = jnp.exp(log_dt.astype(F32))[..., None]
    mag = jnp.exp(lr * dt)
    a_re, a_im = mag * jnp.cos(li * dt), mag * jnp.sin(li * dt)
    den = lr * lr + li * li
    k_re = ((a_re - 1) * lr + a_im * li) / den
    k_im = (a_im * lr - (a_re - 1) * li) / den
    bb_re = k_re[..., None] * b_re - k_im[..., None] * b_im
    bb_im = k_re[..., None] * b_im + k_im[..., None] * b_re
    ks = jnp.arange(q + 1, dtype=F32)[:, None, None, None]
    pmag = jnp.exp(ks * (lr * dt))
    pw_re, pw_im = pmag * jnp.cos(ks * (li * dt)), pmag * jnp.sin(ks * (li * dt))
    ab_re = pw_re[..., None] * bb_re - pw_im[..., None] * bb_im
    ab_im = pw_re[..., None] * bb_im + pw_im[..., None] * bb_re
    contract = lambda c, ab: jnp.sum(ab.transpose(1, 2, 0, 4, 3)[:, :, :, :, None, :]
                                     * c[:, :, None, None, :, :], axis=-1)
    kmat = contract(c_re, ab_re) - contract(c_im, ab_im)
    s_idx = jnp.arange(q)[:, None]
    t_idx = jnp.arange(q)[None, :]
    lag_f = t_idx - s_idx
    lag_b = s_idx - t_idx
    kf = jnp.where((lag_f >= 0)[None, :, :, None, None], kmat[0][:, jnp.clip(lag_f, 0, q - 1)], 0.0)
    kb = jnp.where((lag_b >= 0)[None, :, :, None, None], kmat[1][:, jnp.clip(lag_b, 0, q - 1)], 0.0)
    toep = (kf + kb).transpose(0, 1, 3, 2, 4).reshape(ngrp, q * nch, q * nch)
    sf_re = ab_re[q - 1 - jnp.arange(q), 0]
    sf_im = ab_im[q - 1 - jnp.arange(q), 0]
    sb_re = ab_re[jnp.arange(q), 1]
    sb_im = ab_im[jnp.arange(q), 1]
    summ = jnp.stack([sf_re, sf_im, sb_re, sb_im], 0)
    summ = summ.transpose(2, 1, 4, 0, 3).reshape(ngrp, q * nch, 4, nst)
    npair = ngrp // 2
    summ = summ.reshape(npair, 2, q * nch, 4, nst)
    zero = jnp.zeros_like(summ[:, 0])
    wa = jnp.concatenate([jnp.concatenate([summ[:, 0], zero], -1),
                          jnp.concatenate([zero, summ[:, 1]], -1)], 1)
    wa = wa.reshape(npair, 2 * q * nch, 4 * 2 * nst)
    cp_re = c_re[None] * pw_re[:, :, :, None, :] - c_im[None] * pw_im[:, :, :, None, :]
    cp_im = c_re[None] * pw_im[:, :, :, None, :] + c_im[None] * pw_re[:, :, :, None, :]
    tf = jnp.arange(q) + 1
    tb = q - jnp.arange(q)
    cr = jnp.stack([cp_re[tf, 0], -cp_im[tf, 0], cp_re[tb, 1], -cp_im[tb, 1]], 0)
    cr = cr.transpose(2, 0, 4, 1, 3).reshape(ngrp, 4, nst, q * nch)
    cr = cr.reshape(npair, 2, 4, nst, q * nch)
    zc = jnp.zeros_like(cr[:, 0])
    wc0 = jnp.concatenate([cr[:, 0], zc], 2)
    wc1 = jnp.concatenate([zc, cr[:, 1]], 2)
    wc = jnp.stack([wc0, wc1], 1).reshape(ngrp, 4 * 2 * nst, q * nch)
    a16 = jnp.stack([pw_re[q, 0], pw_im[q, 0], pw_re[q, 1], pw_im[q, 1]], 0)
    a16 = jnp.pad(a16.reshape(4, ngrp * nst), ((0, 4), (0, 0)))
    return toep.astype(BF16), wa.astype(BF16), wc.astype(BF16), a16


def _s5_row_chunk(nc):
    for r in (128, 96, 64, 48, 32, 16):
        if nc % r == 0:
            return r
    raise ValueError(f"unsupported chunk count {nc}")


def _s5_summary_kernel(u_ref, w_ref, ug_ref, o_ref, *, nc):
    q, nch = S5_CHUNK, S5_GROUP
    ngl = u_ref.shape[1] // nch
    rc = _s5_row_chunk(nc)

    def regroup(k, c):
        r0 = pl.multiple_of(k * rc, rc)
        steps = [u_ref[pl.ds(r0 * q + s, rc, stride=q), :] for s in range(q)]
        for gl in range(ngl):
            ug_ref[gl, pl.ds(r0, rc), :] = jnp.concatenate(
                [p[:, gl * nch:(gl + 1) * nch] for p in steps], axis=1).astype(BF16)
        return c

    lax.fori_loop(0, nc // rc, regroup, 0)
    for pr in range(ngl // 2):
        x = jnp.concatenate([ug_ref[2 * pr], ug_ref[2 * pr + 1]], axis=1)
        r = jnp.dot(x, w_ref[pr], preferred_element_type=F32)
        for qd in range(4):
            o_ref[qd, :, pr * S5_PAIR_LANES:(pr + 1) * S5_PAIR_LANES] = (
                r[:, qd * S5_PAIR_LANES:(qd + 1) * S5_PAIR_LANES])


def _s5_scan_kernel(s_ref, a_ref, h_ref, *, nb, n_lat, n_ctx):
    afr, afi = a_ref[0:1, :], a_ref[1:2, :]
    abr, abi = a_ref[2:3, :], a_ref[3:4, :]
    zero = jnp.zeros_like(afr)

    def step(rf, rb, carry):
        hfr, hfi, hbr, hbi = carry
        h_ref[0, pl.ds(rf, 1), :] = hfr
        h_ref[1, pl.ds(rf, 1), :] = hfi
        h_ref[2, pl.ds(rb, 1), :] = hbr
        h_ref[3, pl.ds(rb, 1), :] = hbi
        sfr, sfi = s_ref[0, pl.ds(rf, 1), :], s_ref[1, pl.ds(rf, 1), :]
        sbr, sbi = s_ref[2, pl.ds(rb, 1), :], s_ref[3, pl.ds(rb, 1), :]
        return (afr * hfr - afi * hfi + sfr, afr * hfi + afi * hfr + sfi,
                abr * hbr - abi * hbi + sbr, abr * hbi + abi * hbr + sbi)

    for b in range(nb):
        lat0 = b * n_lat
        ctx0 = nb * n_lat + b * n_ctx
        carry = lax.fori_loop(
            0, n_ctx, lambda i, c: step(ctx0 + i, ctx0 + n_ctx - 1 - i, c), (zero, zero, zero, zero))
        lax.fori_loop(0, n_lat, lambda i, c: step(lat0 + i, lat0 + n_lat - 1 - i, c), carry)


def _s5_output_kernel(ug_ref, t_ref, h_ref, w_ref, o_ref, y_ref, *, nc):
    q, nch = S5_CHUNK, S5_GROUP
    ngl = ug_ref.shape[0]
    rc = _s5_row_chunk(nc)
    for gl in range(ngl):
        pr = gl // 2
        hcat = jnp.concatenate(
            [h_ref[qd, :, pr * S5_PAIR_LANES:(pr + 1) * S5_PAIR_LANES] for qd in range(4)],
            axis=1).astype(BF16)
        y_ref[gl] = (jnp.dot(ug_ref[gl], t_ref[gl], preferred_element_type=F32)
                     + jnp.dot(hcat, w_ref[gl], preferred_element_type=F32))

    def ungroup(k, c):
        r0 = pl.multiple_of(k * rc, rc)
        ys = [y_ref[gl, pl.ds(r0, rc), :] for gl in range(ngl)]
        for s in range(q):
            o_ref[pl.ds(r0 * q + s, rc, stride=q), :] = jnp.concatenate(
                [y[:, s * nch:(s + 1) * nch] for y in ys], axis=1)
        return c

    lax.fori_loop(0, nc // rc, ungroup, 0)


def _s5_mixer(hz, weights, *, width, nb, n_lat, n_ctx):
    toep, wa, wc, a16 = weights
    t = hz.shape[0]
    q, nch = S5_CHUNK, S5_GROUP
    ngrp = width // nch
    ngl = V7X_LANES // nch
    nstrip = ngrp // ngl
    nc = t // q
    cw = q * nch
    lanes = ngrp * S5_STATE
    slanes = ngl * S5_STATE
    ug, summ = pl.pallas_call(
        functools.partial(_s5_summary_kernel, nc=nc),
        grid=(nstrip,),
        in_specs=[pl.BlockSpec((t, V7X_LANES), lambda j: (0, j)),
                  pl.BlockSpec((ngl // 2, 2 * cw, 4 * S5_PAIR_LANES), lambda j: (j, 0, 0))],
        out_specs=[pl.BlockSpec((ngl, nc, cw), lambda j: (j, 0, 0)),
                   pl.BlockSpec((4, nc, slanes), lambda j: (0, 0, j))],
        out_shape=[jax.ShapeDtypeStruct((ngrp, nc, cw), BF16),
                   jax.ShapeDtypeStruct((4, nc, lanes), F32)],
        compiler_params=_params(("arbitrary",)),
        name="s5_summary",
    )(hz, wa)
    wl = S5_SCAN_LANES
    carry = pl.pallas_call(
        functools.partial(_s5_scan_kernel, nb=nb, n_lat=n_lat // q, n_ctx=n_ctx // q),
        grid=(lanes // wl,),
        in_specs=[pl.BlockSpec((4, nc, wl), lambda j: (0, 0, j)),
                  pl.BlockSpec((8, wl), lambda j: (0, j))],
        out_specs=pl.BlockSpec((4, nc, wl), lambda j: (0, 0, j)),
        out_shape=jax.ShapeDtypeStruct((4, nc, lanes), F32),
        compiler_params=_params(("arbitrary",)),
        name="s5_scan",
    )(summ, a16)
    return pl.pallas_call(
        functools.partial(_s5_output_kernel, nc=nc),
        grid=(nstrip,),
        in_specs=[pl.BlockSpec((ngl, nc, cw), lambda j: (j, 0, 0)),
                  pl.BlockSpec((ngl, cw, cw), lambda j: (j, 0, 0)),
                  pl.BlockSpec((4, nc, slanes), lambda j: (0, 0, j)),
                  pl.BlockSpec((ngl, 4 * S5_PAIR_LANES, cw), lambda j: (j, 0, 0))],
        out_specs=pl.BlockSpec((t, V7X_LANES), lambda j: (0, j)),
        out_shape=jax.ShapeDtypeStruct((t, width), F32),
        scratch_shapes=[pltpu.VMEM((ngl, nc, cw), F32)],
        compiler_params=_params(("arbitrary",)),
        name="s5_output",
    )(ug, toep, carry, wc)


def _dwconv_tile(pad_ref, main, prev, nxt, first, last, cw_ref, cb, rows):
    pad_ref[0:HALO, :] = jnp.where(first, 0.0, prev)
    pad_ref[HALO:HALO + rows, :] = main
    pad_ref[HALO + rows:2 * HALO + rows, :] = jnp.where(last, 0.0, nxt)
    total = rows + 2 * HALO
    pieces = []
    for ct in range(pad_ref.shape[1] // V7X_LANES):
        sl = slice(ct * V7X_LANES, (ct + 1) * V7X_LANES)
        col = pad_ref[:, sl]
        acc = cb[:, sl]
        for k in range(CONV_W):
            shift = (CONV_PAD_LEFT - k) % total
            tap = col if shift == 0 else pltpu.roll(col, shift, axis=0)
            acc = acc + cw_ref[k:k + 1, sl] * tap[HALO:HALO + rows, :]
        pieces.append(acc)
    return jnp.concatenate(pieces, axis=1)


def _lru_kernel(v_ref, vp_ref, vn_ref, cw_ref, cb_ref, wa_ref, wx_ref, ba_ref, bx_ref, sp_ref,
                o_ref, pad_ref, a_ref, b_ref, h_ref, *, n_lat_tiles):
    d = pl.program_id(0)
    k = pl.program_id(2)
    tc = LRU_TILE
    j = jnp.where(d == 0, k - 1, n_lat_tiles - k)
    first = jnp.logical_or(k == 0, j == 0)
    last = jnp.logical_or(k == 0, j == n_lat_tiles - 1)
    vc = _dwconv_tile(pad_ref, v_ref[...], vp_ref[...], vn_ref[...], first, last, cw_ref,
                      cb_ref[...], tc)
    vcb = vc.astype(BF16)
    nblk = vc.shape[1] // LRU_GATE_BLOCK

    def gate(w_ref, bias):
        parts = [jnp.dot(vcb[:, m * LRU_GATE_BLOCK:(m + 1) * LRU_GATE_BLOCK], w_ref[m],
                         preferred_element_type=F32) for m in range(nblk)]
        return jax.nn.sigmoid(jnp.concatenate(parts, axis=1) + bias)

    r = gate(wa_ref, ba_ref[...])
    i = gate(wx_ref, bx_ref[...])
    a = jnp.exp(-LRU_C * r * sp_ref[...])
    a_ref[...] = a
    b_ref[...] = jnp.sqrt(1.0 - a * a) * (i * vc)

    @pl.when(k == 0)
    def _():
        h_ref[...] = jnp.zeros_like(h_ref)

    def body(t, h):
        tt = jnp.where(d == 0, t, tc - 1 - t)
        h = a_ref[pl.ds(tt, 1), :] * h + b_ref[pl.ds(tt, 1), :]
        o_ref[pl.ds(tt, 1), :] = h
        return h

    h_ref[0:1, :] = lax.fori_loop(0, tc, body, h_ref[0:1, :], unroll=8)


def _lru_scan(hz, col_block, conv_w, conv_b, wa, wx, ba, bx, sp, *, nb, n_lat, n_ctx):
    t = hz.shape[0]
    width = conv_w.shape[1]
    tc = LRU_TILE
    assert n_ctx == tc and n_lat % tc == 0
    nlt = n_lat // tc
    hb = tc // HALO
    n_halo_blocks = t // HALO

    def row_block(d, b, k):
        j = jnp.where(d == 0, k - 1, nlt - k)
        return jnp.where(k == 0, nb * nlt + b, b * nlt + j)

    cw = jnp.pad(conv_w, ((0, 8 - CONV_W), (0, 0)))
    nblk = width // LRU_GATE_BLOCK
    vec = lambda: pl.BlockSpec((None, 1, width), lambda d, b, k: (d, 0, 0))
    return pl.pallas_call(
        functools.partial(_lru_kernel, n_lat_tiles=nlt),
        grid=(2, nb, nlt + 1),
        in_specs=[pl.BlockSpec((tc, width), lambda d, b, k: (row_block(d, b, k), col_block)),
                  pl.BlockSpec((HALO, width),
                               lambda d, b, k: (jnp.maximum(row_block(d, b, k) * hb - 1, 0), col_block)),
                  pl.BlockSpec((HALO, width),
                               lambda d, b, k: (jnp.minimum(row_block(d, b, k) * hb + hb,
                                                            n_halo_blocks - 1), col_block)),
                  pl.BlockSpec((8, width), lambda d, b, k: (0, 0)),
                  pl.BlockSpec((1, width), lambda d, b, k: (0, 0)),
                  pl.BlockSpec((None, nblk, LRU_GATE_BLOCK, LRU_GATE_BLOCK), lambda d, b, k: (d, 0, 0, 0)),
                  pl.BlockSpec((None, nblk, LRU_GATE_BLOCK, LRU_GATE_BLOCK), lambda d, b, k: (d, 0, 0, 0)),
                  vec(), vec(), vec()],
        out_specs=pl.BlockSpec((None, tc, width), lambda d, b, k: (d, row_block(d, b, k), 0)),
        out_shape=jax.ShapeDtypeStruct((2, t, width), F32),
        scratch_shapes=[pltpu.VMEM((tc + 2 * HALO, width), F32),
                        pltpu.VMEM((tc, width), F32),
                        pltpu.VMEM((tc, width), F32),
                        pltpu.VMEM((8, width), F32)],
        compiler_params=_params(("arbitrary", "arbitrary", "arbitrary")),
        name="rglru_scan",
    )(hz, hz, hz, cw, conv_b.reshape(1, width), wa, wx, ba, bx, sp)


def _block_diag_gates(w):
    ndir, heads, hd, _ = w.shape
    per = LRU_GATE_BLOCK // hd
    w = w.reshape(ndir, heads // per, per, hd, hd)
    eye = jnp.eye(per, dtype=w.dtype)
    full = jnp.einsum('dmhij,hk->dmhikj', w, eye)
    return full.reshape(ndir, heads // per, per * hd, per * hd).astype(BF16)


def _out0_kernel(ys_ref, u_ref, g_ref, hf_ref, hb_ref, x_ref, m_ref, d_ref, gw_ref, gb_ref,
                 wt_ref, wb_ref, o_ref):
    y = ys_ref[...] + d_ref[...] * u_ref[...]
    z = _gelu(y)
    gate = jax.nn.sigmoid(jnp.dot(z.astype(BF16), gw_ref[...], preferred_element_type=F32) + gb_ref[...])
    a = z * gate
    r = _gelu(g_ref[...]) * (hf_ref[...] + hb_ref[...])
    dx = (jnp.dot(a.astype(BF16), wt_ref[...], preferred_element_type=F32)
          + jnp.dot(r.astype(BF16), wb_ref[...], preferred_element_type=F32))
    o_ref[...] = x_ref[...] + m_ref[2:3, :] * dx


def _out0(ys, hz, hfb, xa, slab, d_skip, glu_w, glu_b, w_out, *, tiles_per_batch):
    t, d = xa.shape
    width = ys.shape[1]
    tm = TOKEN_TILE // 2
    last_row = slab.shape[0] - 1
    tpb = tiles_per_batch * (TOKEN_TILE // tm)
    const = lambda shape: pl.BlockSpec(shape, lambda i: tuple(0 for _ in shape))
    return pl.pallas_call(
        _out0_kernel,
        grid=(t // tm,),
        in_specs=[pl.BlockSpec((tm, width), lambda i: (i, 0)),
                  pl.BlockSpec((tm, width), lambda i: (i, 0)),
                  pl.BlockSpec((tm, width), lambda i: (i, 1)),
                  pl.BlockSpec((None, tm, width), lambda i: (0, i, 0)),
                  pl.BlockSpec((None, tm, width), lambda i: (1, i, 0)),
                  pl.BlockSpec((tm, d), lambda i: (i, 0)),
                  pl.BlockSpec((None, 8, d), lambda i: (jnp.minimum(i // tpb, last_row), 0, 0)),
                  const((1, width)), const((width, width)), const((1, width)),
                  const((width, d)), const((width, d))],
        out_specs=pl.BlockSpec((tm, d), lambda i: (i, 0)),
        out_shape=jax.ShapeDtypeStruct((t, d), F32),
        compiler_params=_params(("arbitrary",)),
        name="hybrid_out",
    )(ys, hz, hz, hfb, hfb, xa, slab, d_skip.reshape(1, width), glu_w.astype(BF16),
      glu_b.reshape(1, width), w_out[:width].astype(BF16), w_out[width:].astype(BF16))


def _store_token_rows(dst_ref, val):
    rows, d = val.shape
    per = d // V7X_LANES
    for c in range(per):
        dst_ref[pl.ds(c, rows, stride=per), :] = val[:, c * V7X_LANES:(c + 1) * V7X_LANES]


def _load_token_rows(src_ref, start, rows, per):
    return [src_ref[pl.ds(start * per + c, rows, stride=per), :] for c in range(per)]


def _store_packed_token_rows(dst_ref, val):
    rows, d = val.shape
    half = d // (2 * V7X_LANES)
    bits = lambda v: pltpu.bitcast(v.astype(BF16).astype(F32), jnp.uint32)
    for c in range(half):
        lo = bits(val[:, c * V7X_LANES:(c + 1) * V7X_LANES])
        hi = bits(val[:, (c + half) * V7X_LANES:(c + half + 1) * V7X_LANES])
        dst_ref[pl.ds(c, rows, stride=half), :] = (lo >> 16) | hi


def _load_packed_token_rows(src_ref, rows, half):
    words = [src_ref[pl.ds(c, rows, stride=half), :] for c in range(half)]
    lo = [pltpu.bitcast(w << 16, F32).astype(BF16) for w in words]
    hi = [pltpu.bitcast(w & jnp.uint32(0xFFFF0000), F32).astype(BF16) for w in words]
    return jnp.concatenate(lo + hi, axis=1)


def _route_kernel(x_ref, g_ref, m_ref, w_ref, b_ref, h_ref, r_ref, cnt_ref, carry_ref):
    @pl.when(pl.program_id(0) == 0)
    def _():
        carry_ref[...] = jnp.zeros_like(carry_ref)

    h = _norm_mod(x_ref[...], g_ref[...], m_ref, 3, 4)
    _store_packed_token_rows(h_ref, h)
    h_hi = h.astype(BF16)
    h_lo = (h - h_hi.astype(F32)).astype(BF16)
    p = jnp.dot(h_hi, w_ref[...], preferred_element_type=F32)
    logits = (p[:, :ROUTE_LANES] + p[:, ROUTE_LANES:] + b_ref[...]
              + jnp.dot(h_lo, w_ref[:, :ROUTE_LANES], preferred_element_type=F32))
    lane = lax.broadcasted_iota(jnp.int32, logits.shape, 1).astype(F32)
    ninf = -jnp.inf
    big = float(ROUTE_LANES)
    lg = jnp.where(lane < MOE_GROUPS, logits, ninf)
    mg = jnp.max(lg, axis=-1, keepdims=True)
    gidx = jnp.min(jnp.where(lg == mg, lane, big), axis=-1, keepdims=True)
    p_top = 1.0 / jnp.sum(jnp.exp(lg - mg), axis=-1, keepdims=True)
    lo = MOE_GROUPS + gidx * MOE_PER_GROUP
    le = jnp.where(jnp.logical_and(lane >= lo, lane < lo + MOE_PER_GROUP), logits, ninf)
    v1 = jnp.max(le, axis=-1, keepdims=True)
    i1 = jnp.min(jnp.where(le == v1, lane, big), axis=-1, keepdims=True)
    le2 = jnp.where(lane == i1, ninf, le)
    v2 = jnp.max(le2, axis=-1, keepdims=True)
    i2 = jnp.min(jnp.where(le2 == v2, lane, big), axis=-1, keepdims=True)
    tt = jnp.exp(v2 - v1)
    g1 = p_top / (1.0 + tt)
    g2 = p_top * tt / (1.0 + tt)
    e1, e2 = i1 - MOE_GROUPS, i2 - MOE_GROUPS
    tm = h.shape[0]
    tri = (lax.broadcasted_iota(jnp.int32, (tm, tm), 1)
           < lax.broadcasted_iota(jnp.int32, (tm, tm), 0)).astype(BF16)
    hot1, hot2 = lane == e1, lane == e2
    before1 = jnp.dot(tri, hot1.astype(BF16), preferred_element_type=F32)
    before2 = jnp.dot(tri, hot2.astype(BF16), preferred_element_type=F32)
    tot1 = jnp.sum(hot1.astype(F32), axis=0, keepdims=True)
    tot2 = jnp.sum(hot2.astype(F32), axis=0, keepdims=True)
    carry = carry_ref[0:1, :]
    rank1 = jnp.sum(jnp.where(hot1, carry + before1, 0.0), axis=-1, keepdims=True)
    rank2 = jnp.sum(jnp.where(hot2, carry + tot1 + before2, 0.0), axis=-1, keepdims=True)
    carry = carry + tot1 + tot2
    carry_ref[0:1, :] = carry
    cnt_ref[...] = jnp.broadcast_to(carry, cnt_ref.shape)
    out = jnp.where(lane == 0, e1, 0.0)
    out = jnp.where(lane == 1, e2, out)
    out = jnp.where(lane == 2, g1, out)
    out = jnp.where(lane == 3, g2, out)
    out = jnp.where(lane == 4, rank1, out)
    out = jnp.where(lane == 5, rank2, out)
    r_ref[...] = out


def _route(xa, g, slab, wg, bg, we, be, *, tiles_per_batch):
    t, d = xa.shape
    tm = TOKEN_TILE
    half = d // (2 * V7X_LANES)
    nl = MOE_GROUPS + MOE_EXPERTS
    wr = jnp.pad(jnp.concatenate([wg, we], axis=1), ((0, 0), (0, ROUTE_LANES - nl)))
    wr_hi = wr.astype(BF16)
    wr = jnp.concatenate([wr_hi, (wr - wr_hi.astype(F32)).astype(BF16)], axis=1)
    br =jnp.pad(jnp.concatenate([bg, be], axis=0), (0, ROUTE_LANES - nl)).reshape(1, ROUTE_LANES)
    last_row = slab.shape[0] - 1
    return pl.pallas_call(
        _route_kernel,
        grid=(t // tm,),
        in_specs=[pl.BlockSpec((tm, d), lambda i: (i, 0)),
                  pl.BlockSpec((1, d), lambda i: (0, 0)),
                  pl.BlockSpec((None, 8, d), lambda i: (jnp.minimum(i // tiles_per_batch, last_row), 0, 0)),
                  pl.BlockSpec((d, 2 * ROUTE_LANES), lambda i: (0, 0)),
                  pl.BlockSpec((1, ROUTE_LANES), lambda i: (0, 0))],
        out_specs=[pl.BlockSpec((tm * half, V7X_LANES), lambda i: (i, 0)),
                   pl.BlockSpec((tm, ROUTE_LANES), lambda i: (i, 0)),
                   pl.BlockSpec((8, ROUTE_LANES), lambda i: (0, 0))],
        out_shape=[jax.ShapeDtypeStruct((t * half, V7X_LANES), jnp.uint32),
                   jax.ShapeDtypeStruct((t, ROUTE_LANES), F32),
                   jax.ShapeDtypeStruct((8, ROUTE_LANES), F32)],
        scratch_shapes=[pltpu.VMEM((8, ROUTE_LANES), F32)],
        compiler_params=_params(("arbitrary",)),
        name="moe_route",
    )(xa, g.reshape(1, d), slab, wr, br)


def _dispatch(route, counts):
    bm = MOE_BLOCK
    n_pairs = route.shape[0] * MOE_TOPK
    experts = route[:, 0:MOE_TOPK].astype(jnp.int32)
    rank = route[:, 4:4 + MOE_TOPK].astype(jnp.int32)
    cnt = counts[0, :MOE_EXPERTS].astype(jnp.int32)
    padded = (cnt + bm - 1) // bm * bm
    pad_end = jnp.cumsum(padded)
    dest = (pad_end - padded)[experts] + rank
    n_blocks = -(-n_pairs // bm) + MOE_EXPERTS
    first_slot = jnp.arange(n_blocks, dtype=jnp.int32) * bm
    block_e = jnp.minimum(jnp.sum((pad_end[None, :] <= first_slot[:, None]).astype(jnp.int32), axis=1),
                          MOE_EXPERTS - 1)
    n_used = (pad_end[-1] // bm).astype(jnp.int32).reshape(1)
    tail = jnp.maximum(pad_end - bm, 0).astype(jnp.int32)
    return dest, block_e, n_used, tail, n_blocks


def _scatter_kernel(tail_ref, nu_ref, d_ref, h_ref, xs_hbm, stage, zsrc, sem, *, n_steps, n_blocks, per):
    i = pl.program_id(0)
    tm = d_ref.shape[1] // MOE_TOPK
    zrows = zsrc.shape[0]

    def fill_at(slot):
        return pltpu.make_async_copy(zsrc, xs_hbm.at[pl.ds(slot * per, zrows), :], sem.at[2])

    def fill(e):
        return fill_at(tail_ref[e])

    def fill_unused(wait):
        def body(blk, c):
            cp = fill_at(blk * MOE_BLOCK)
            cp.wait() if wait else cp.start()
            return c
        lax.fori_loop(nu_ref[0], n_blocks, body, 0)

    def copy(step, r, k):
        return pltpu.make_async_copy(stage.at[lax.rem(step, 2), pl.ds(r * per, per), :],
                                     xs_hbm.at[pl.ds(d_ref[0, MOE_TOPK * r + k] * per, per), :],
                                     sem.at[lax.rem(step, 2)])

    def wait_step(step):
        for r in range(tm):
            for k in range(MOE_TOPK):
                copy(step, r, k).wait()

    @pl.when(i == 0)
    def _():
        zsrc[...] = jnp.zeros_like(zsrc)
        for e in range(MOE_EXPERTS):
            fill(e).start()
        fill_unused(False)
        for e in range(MOE_EXPERTS):
            fill(e).wait()
        fill_unused(True)

    stage[lax.rem(i, 2)] = h_ref[...]
    for r in range(tm):
        for k in range(MOE_TOPK):
            copy(i, r, k).start()

    @pl.when(i > 0)
    def _():
        wait_step(i - 1)

    @pl.when(i == n_steps - 1)
    def _():
        wait_step(i)


def _moe_scatter(h_rows, dest, tail, n_used, n_blocks, per):
    t = dest.shape[0]
    tm = TOKEN_TILE // 2
    n_steps = t // tm
    bm = MOE_BLOCK
    d3 = dest.reshape(n_steps, 1, MOE_TOPK * tm)
    grid_spec = pltpu.PrefetchScalarGridSpec(
        num_scalar_prefetch=2,
        grid=(n_steps,),
        in_specs=[pl.BlockSpec((None, 1, MOE_TOPK * tm), lambda i, tl, nu: (i, 0, 0),
                               memory_space=pltpu.SMEM),
                  pl.BlockSpec((tm * per, V7X_LANES), lambda i, tl, nu: (i, 0))],
        out_specs=pl.BlockSpec(memory_space=pl.ANY),
        scratch_shapes=[pltpu.VMEM((2, tm * per, V7X_LANES), h_rows.dtype),
                        pltpu.VMEM((bm * per, V7X_LANES), h_rows.dtype),
                        pltpu.SemaphoreType.DMA((3,))],
    )
    return pl.pallas_call(
        functools.partial(_scatter_kernel, n_steps=n_steps, n_blocks=n_blocks, per=per),
        grid_spec=grid_spec,
        out_shape=jax.ShapeDtypeStruct((n_blocks * bm * per, V7X_LANES), h_rows.dtype),
        compiler_params=_params(("arbitrary",)),
        name="moe_scatter",
    )(tail, n_used, d3, h_rows)


def _token_copy(src_hbm, tok, dst, slot, r, sem, per):
    return pltpu.make_async_copy(src_hbm.at[pl.ds(tok * per, per), :],
                                 dst.at[slot, pl.ds(r * per, per), :], sem.at[slot])


def _moe_kernel(be_ref, nu_ref, xs_ref, w1_ref, w3_ref, w2_ref, o_ref, w1b, w3b, w2b, prev_ref):
    b = pl.program_id(0)
    bm = MOE_BLOCK
    per = w1_ref.shape[0] // V7X_LANES

    @pl.when(b == 0)
    def _():
        prev_ref[0] = -1

    @pl.when(b < nu_ref[0])
    def _():
        e = be_ref[b]

        @pl.when(e != prev_ref[0])
        def _():
            w1b[...] = w1_ref[...].astype(BF16)
            w3b[...] = w3_ref[...].astype(BF16)
            w2b[...] = w2_ref[...].astype(BF16)
            prev_ref[0] = e

        x = _load_packed_token_rows(xs_ref, bm, per // 2)
        h1 = jnp.dot(x, w1b[...], preferred_element_type=F32)
        h3 = jnp.dot(x, w3b[...], preferred_element_type=F32)
        act = (_silu(h1) * h3).astype(BF16)
        _store_token_rows(o_ref, jnp.dot(act, w2b[...], preferred_element_type=F32))

    @pl.when(b >= nu_ref[0])
    def _():
        o_ref[...] = jnp.zeros_like(o_ref)


def _moe_experts(xs_rows, block_e, n_used, n_blocks, w1, w3, w2, layer):
    d, ff = w1.shape[2], w1.shape[3]
    per = d // V7X_LANES
    bm = MOE_BLOCK
    blk = lambda b, be, nu: jnp.minimum(b, jnp.maximum(nu[0] - 1, 0))
    grid_spec = pltpu.PrefetchScalarGridSpec(
        num_scalar_prefetch=2,
        grid=(n_blocks,),
        in_specs=[pl.BlockSpec((bm * per // 2, V7X_LANES), lambda b, be, nu: (blk(b, be, nu), 0)),
                  pl.BlockSpec((None, None, d, ff), lambda b, be, nu: (layer, be[blk(b, be, nu)], 0, 0)),
                  pl.BlockSpec((None, None, d, ff), lambda b, be, nu: (layer, be[blk(b, be, nu)], 0, 0)),
                  pl.BlockSpec((None, None, ff, d), lambda b, be, nu: (layer, be[blk(b, be, nu)], 0, 0))],
        out_specs=pl.BlockSpec((bm * per, V7X_LANES), lambda b, be, nu: (b, 0)),
        scratch_shapes=[pltpu.VMEM((d, ff), BF16),
                        pltpu.VMEM((d, ff), BF16),
                        pltpu.VMEM((ff, d), BF16),
                        pltpu.SMEM((1,), jnp.int32)],
    )
    return pl.pallas_call(
        _moe_kernel,
        grid_spec=grid_spec,
        out_shape=jax.ShapeDtypeStruct((n_blocks * bm * per, V7X_LANES), F32),
        compiler_params=_params(("arbitrary",)),
        name="moe_experts",
    )(block_e, n_used, xs_rows, w1, w3, w2)


def _combine_kernel(d_ref, dn_ref, yb_hbm, x_ref, r_ref, m_ref, *rest, n_tiles, n_lat_tiles):
    if n_lat_tiles < n_tiles:
        lat_ref, ctx_ref, ybuf, sem = rest
    else:
        lat_ref, ybuf, sem = rest
        ctx_ref = None
    i = pl.program_id(0)
    tm, d = x_ref.shape
    per = d // V7X_LANES
    nrow = MOE_TOPK * tm
    slot = lax.rem(i, 2)

    def start_rows(idx_ref, s):
        for r in range(nrow):
            _token_copy(yb_hbm, idx_ref[0, r], ybuf, s, r, sem, per).start()

    @pl.when(i == 0)
    def _():
        start_rows(d_ref, 0)

    @pl.when(i + 1 < n_tiles)
    def _():
        start_rows(dn_ref, 1 - slot)

    for r in range(nrow):
        _token_copy(yb_hbm, 0, ybuf, slot, r, sem, per).wait()
    r = r_ref[...]
    g0, g1 = r[:, 2:3], r[:, 3:4]
    y0 = _load_token_rows(ybuf.at[slot], 0, tm, per)
    y1 = _load_token_rows(ybuf.at[slot], tm, tm, per)
    def emit(o_ref):
        for c in range(per):
            sl = slice(c * V7X_LANES, (c + 1) * V7X_LANES)
            o_ref[pl.ds(c, tm, stride=per), :] = (
                x_ref[:, sl] + m_ref[5:6, sl] * (g0 * y0[c] + g1 * y1[c]))

    if ctx_ref is None:
        emit(lat_ref)
    else:
        pl.when(i < n_lat_tiles)(lambda: emit(lat_ref))
        pl.when(i >= n_lat_tiles)(lambda: emit(ctx_ref))


def _moe_combine(yb_rows, dest, xa, route, slab, *, tiles_per_batch, t_lat):
    t, d = xa.shape
    per = d // V7X_LANES
    tm = TOKEN_TILE // 2
    n_tiles = t // tm
    n_lat_tiles = t_lat // tm
    row_block = pl.BlockSpec((tm * per, V7X_LANES), lambda i: (jnp.minimum(i, n_lat_tiles - 1), 0))
    out_specs = [row_block]
    out_shape = [jax.ShapeDtypeStruct((t_lat * per, V7X_LANES), F32)]
    if n_lat_tiles < n_tiles:
        out_specs.append(pl.BlockSpec((tm * per, V7X_LANES),
                                      lambda i: (jnp.maximum(i - n_lat_tiles, 0), 0)))
        out_shape.append(jax.ShapeDtypeStruct(((t - t_lat) * per, V7X_LANES), F32))
    tpb = tiles_per_batch * (TOKEN_TILE // tm)
    last_row = slab.shape[0] - 1
    d3 = dest.reshape(n_tiles, tm, MOE_TOPK).transpose(0, 2, 1).reshape(n_tiles, 1, MOE_TOPK * tm)
    smem_block = lambda fn: pl.BlockSpec((None, 1, MOE_TOPK * tm), fn, memory_space=pltpu.SMEM)
    return pl.pallas_call(
        functools.partial(_combine_kernel, n_tiles=n_tiles, n_lat_tiles=n_lat_tiles),
        grid=(n_tiles,),
        in_specs=[smem_block(lambda i: (i, 0, 0)),
                  smem_block(lambda i: (jnp.minimum(i + 1, n_tiles - 1), 0, 0)),
                  pl.BlockSpec(memory_space=pl.ANY),
                  pl.BlockSpec((tm, d), lambda i: (i, 0)),
                  pl.BlockSpec((tm, ROUTE_LANES), lambda i: (i, 0)),
                  pl.BlockSpec((None, 8, d), lambda i: (jnp.minimum(i // tpb, last_row), 0, 0))],
        out_specs=out_specs,
        out_shape=out_shape,
        scratch_shapes=[pltpu.VMEM((2, MOE_TOPK * tm * per, V7X_LANES), F32),
                        pltpu.SemaphoreType.DMA((2,))],
        compiler_params=_params(("arbitrary",)),
        name="moe_combine",
    )(d3, d3, yb_rows, xa, route, slab)


def _moe_layer(xa, g, slab, wg, bg, we, be, w1, w3, w2, *, layer, tiles_per_batch, t_lat):
    per = xa.shape[1] // V7X_LANES
    h_rows, route, counts = _route(xa, g, slab, wg, bg, we, be, tiles_per_batch=tiles_per_batch)
    dest, block_e, n_used, tail, n_blocks = _dispatch(route, counts)
    xs_rows = _moe_scatter(h_rows, dest, tail, n_used, n_blocks, per // 2)
    yb_rows = _moe_experts(xs_rows, block_e, n_used, n_blocks, w1, w3, w2, layer)
    return _moe_combine(yb_rows, dest, xa, route, slab, tiles_per_batch=tiles_per_batch, t_lat=t_lat)


def _ssd_kernel(*refs, rev, n_ctx_chunks, n_lat_chunks):
    if rev:
        (xact_ref, bact_ref, cact_ref, dt_ref, dtb_ref, a_ref, yf_ref, z_ref, ng_ref,
         o_ref, xpair, ypair, cbs, bts, cs, acst, dtt, wtt, tott, state) = refs
    else:
        (xs_ref, xsp_ref, xsn_ref, bm_ref, bmp_ref, bmn_ref, cm_ref, cmp_ref, cmn_ref,
         dt_ref, cwx_ref, cbx_ref, cwb_ref, cbb_ref, cwc_ref, cbc_ref, dtb_ref, a_ref, dsk_ref,
         o_ref, xact_ref, bact_ref, cact_ref,
         padx, padb, padc, xpair, ypair, cbs, bts, cs, acst, dtt, wtt, tott, state) = refs
    k = pl.program_id(1)
    q = SSD_CHUNK
    nst = SSD_STATE
    npair = xact_ref.shape[1] // V7X_LANES
    ngrp = bact_ref.shape[1] // nst
    pairs_per_group = npair // ngrp
    if rev:
        bmat, cmat = bact_ref[...], cact_ref[...]
        for p in range(npair):
            xpair[p] = xact_ref[:, p * V7X_LANES:(p + 1) * V7X_LANES]
    else:
        in_ctx = k < n_ctx_chunks
        jc, jl = k, k - n_ctx_chunks
        first = jnp.where(in_ctx, jc == 0, jl == 0)
        last = jnp.where(in_ctx, jc == n_ctx_chunks - 1, jl == n_lat_chunks - 1)

        def conv_silu(pad_ref, main, prev, nxt, cw_ref, cb_ref):
            return _silu(_dwconv_tile(pad_ref, main[...], prev[...], nxt[...], first, last, cw_ref,
                                      cb_ref[...], q))

        xs = conv_silu(padx, xs_ref, xsp_ref, xsn_ref, cwx_ref, cbx_ref)
        bmat = conv_silu(padb, bm_ref, bmp_ref, bmn_ref, cwb_ref, cbb_ref)
        cmat = conv_silu(padc, cm_ref, cmp_ref, cmn_ref, cwc_ref, cbc_ref)
        bact_ref[...] = bmat
        cact_ref[...] = cmat
        for p in range(npair):
            xb = xs[:, p * V7X_LANES:(p + 1) * V7X_LANES].astype(BF16)
            xpair[p] = xb
            xact_ref[:, p * V7X_LANES:(p + 1) * V7X_LANES] = xb
    for g in range(ngrp):
        bg = bmat[:, g * nst:(g + 1) * nst]
        cg = cmat[:, g * nst:(g + 1) * nst].astype(BF16)
        cs[g] = cg
        cbs[g] = lax.dot_general(cg, bg.astype(BF16), (((1,), (1,)), ((), ())),
                                 preferred_element_type=F32)
        bts[g] = bg.T
    dtr = dt_ref[...] + dtb_ref[...]
    dt = jnp.maximum(dtr, 0.0) + jnp.log(1.0 + jnp.exp(-jnp.abs(dtr)))
    adt = dt * a_ref[...]
    row = lax.broadcasted_iota(jnp.int32, (q, q), 0)
    col = lax.broadcasted_iota(jnp.int32, (q, q), 1)
    causal = (row <= col) if rev else (row >= col)
    acs = jnp.dot(causal.astype(F32), adt, preferred_element_type=F32, precision=HIGHEST)
    acs_t = acs.T
    tot = acs_t[:, 0:1] if rev else acs_t[:, q - 1:q]
    acst[...] = acs_t
    dtt[...] = dt.T
    wtt[...] = jnp.exp(tot - acs_t) * dt.T
    tott[...] = jnp.broadcast_to(jnp.exp(tot), (tott.shape[0], q))

    @pl.when(k == 0)
    def _():
        state[...] = jnp.zeros_like(state)

    hoff = SSD_HEAD_DIM if rev else 0
    lane = lax.broadcasted_iota(jnp.int32, (q, V7X_LANES), 1)
    left = lane < SSD_HEAD_DIM

    def pair_body(p, c):
        g = p // pairs_per_group
        h0 = hoff + 2 * p
        x = xpair[p]
        zero = jnp.zeros_like(x)
        xbd = jnp.concatenate([jnp.where(left, x, zero), jnp.where(left, zero, x)], axis=0)
        cb = cbs[g]
        st = state[p]
        mms, wbs, scales, decs = [], [], [], []
        for m in range(2):
            arow = acst[pl.ds(h0 + m, 1), :]
            acol = jnp.broadcast_to(arow, (q, q)).T
            seg = jnp.where(causal, jnp.exp(acol - arow), 0.0) * dtt[pl.ds(h0 + m, 1), :]
            mms.append((cb * seg).astype(BF16))
            scales.append(jnp.exp(acol))
            wbs.append((bts[g] * wtt[pl.ds(h0 + m, 1), :]).astype(BF16))
            decs.append(tott[pl.ds(h0 + m, 1), :])
        intra = jnp.dot(jnp.concatenate(mms, axis=1), xbd, preferred_element_type=F32)
        new = jnp.dot(jnp.concatenate(wbs, axis=1), xbd, preferred_element_type=F32)
        inter = jnp.dot(cs[g], st.astype(BF16), preferred_element_type=F32)
        ypair[p] = intra + inter * jnp.where(left, scales[0], scales[1])
        state[p] = st * jnp.where(left, decs[0], decs[1]) + new
        return c

    lax.fori_loop(0, npair, pair_body, 0, unroll=SSD_PAIR_UNROLL)
    if not rev:
        for p in range(npair):
            sl = slice(p * V7X_LANES, (p + 1) * V7X_LANES)
            o_ref[:, sl] = ypair[p] + dsk_ref[:, sl] * xs[:, sl]
        return
    sq = jnp.zeros((q, V7X_LANES), F32)
    for p in range(npair):
        sl = slice(p * V7X_LANES, (p + 1) * V7X_LANES)
        gated = (ypair[p] + yf_ref[:, sl]) * _silu(z_ref[:, sl])
        ypair[p] = gated
        sq = sq + gated * gated
    scale = lax.rsqrt(jnp.sum(sq, axis=-1, keepdims=True) / (npair * V7X_LANES) + RMS_EPS)
    for p in range(npair):
        sl = slice(p * V7X_LANES, (p + 1) * V7X_LANES)
        o_ref[:, sl] = (ypair[p] * scale * ng_ref[:, sl]).astype(BF16)


def _ssd_scan(hz, conv_w, conv_b, dt_bias, a_neg, d_skip_lanes, *, rev, nb, n_lat, n_ctx, inner,
              fwd=None, norm_g=None):
    t = hz.shape[0]
    q = SSD_CHUNK
    bc = SSD_GROUPS * SSD_STATE
    ncc, ncl = n_ctx // q, n_lat // q
    hb = q // HALO
    n_halo_blocks = t // HALO
    nheads2 = dt_bias.size
    assert nheads2 == V7X_LANES and q == V7X_LANES and SSD_STATE == V7X_LANES
    npair = inner // V7X_LANES
    xs_col = inner // inner
    b_col = 2 * inner // bc
    c_col = b_col + 1
    dt_col = (2 * inner + 2 * bc) // nheads2

    def chunk(b, k):
        kk = k - ncc
        if rev:
            jc, jl = ncc - 1 - k, ncl - 1 - kk
        else:
            jc, jl = k, kk
        return jnp.where(k < ncc, nb * ncl + b * ncc + jc, b * ncl + jl)

    def main(width, colb):
        return pl.BlockSpec((q, width), lambda b, k: (chunk(b, k), colb))

    def prev(width, colb):
        return pl.BlockSpec((HALO, width), lambda b, k: (jnp.maximum(chunk(b, k) * hb - 1, 0), colb))

    def nxt(width, colb):
        return pl.BlockSpec((HALO, width),
                            lambda b, k: (jnp.minimum(chunk(b, k) * hb + hb, n_halo_blocks - 1), colb))

    const = lambda shape: pl.BlockSpec(shape, lambda b, k: (0, 0))
    shared_scratch = [pltpu.VMEM((npair, q, V7X_LANES), BF16),
                      pltpu.VMEM((npair, q, V7X_LANES), F32),
                      pltpu.VMEM((SSD_GROUPS, q, q), F32),
                      pltpu.VMEM((SSD_GROUPS, SSD_STATE, q), F32),
                      pltpu.VMEM((SSD_GROUPS, q, SSD_STATE), BF16),
                      pltpu.VMEM((nheads2, q), F32),
                      pltpu.VMEM((nheads2, q), F32),
                      pltpu.VMEM((nheads2, q), F32),
                      pltpu.VMEM((nheads2, q), F32),
                      pltpu.VMEM((npair, SSD_STATE, V7X_LANES), F32)]
    kern = functools.partial(_ssd_kernel, rev=rev, n_ctx_chunks=ncc, n_lat_chunks=ncl)
    dtb, an = dt_bias.reshape(1, nheads2), a_neg.reshape(1, nheads2)
    if rev:
        y_fwd, xact, bact, cact = fwd
        return pl.pallas_call(
            kern,
            grid=(nb, ncc + ncl),
            in_specs=[main(inner, 0), main(bc, 0), main(bc, 0), main(nheads2, dt_col),
                      const((1, nheads2)), const((1, nheads2)),
                      main(inner, 0), main(inner, 0), const((1, inner))],
            out_specs=main(inner, 0),
            out_shape=jax.ShapeDtypeStruct((t, inner), BF16),
            scratch_shapes=shared_scratch,
            compiler_params=_params(("arbitrary", "arbitrary")),
            name="ssd_scan_bwd",
        )(xact, bact, cact, hz, dtb, an, y_fwd, hz, norm_g.reshape(1, inner))
    cw = jnp.pad(conv_w, ((0, 8 - CONV_W), (0, 0)))
    cb = conv_b.reshape(1, -1)
    return pl.pallas_call(
        kern,
        grid=(nb, ncc + ncl),
        in_specs=[main(inner, xs_col), prev(inner, xs_col), nxt(inner, xs_col),
                  main(bc, b_col), prev(bc, b_col), nxt(bc, b_col),
                  main(bc, c_col), prev(bc, c_col), nxt(bc, c_col),
                  main(nheads2, dt_col),
                  const((8, inner)), const((1, inner)),
                  const((8, bc)), const((1, bc)),
                  const((8, bc)), const((1, bc)),
                  const((1, nheads2)), const((1, nheads2)), const((1, inner))],
        out_specs=[main(inner, 0), main(inner, 0), main(bc, 0), main(bc, 0)],
        out_shape=[jax.ShapeDtypeStruct((t, inner), F32), jax.ShapeDtypeStruct((t, inner), BF16),
                   jax.ShapeDtypeStruct((t, bc), F32), jax.ShapeDtypeStruct((t, bc), F32)],
        scratch_shapes=[pltpu.VMEM((q + 2 * HALO, inner), F32),
                        pltpu.VMEM((q + 2 * HALO, bc), F32),
                        pltpu.VMEM((q + 2 * HALO, bc), F32)] + shared_scratch,
        compiler_params=_params(("arbitrary", "arbitrary")),
        name="ssd_scan_fwd",
    )(hz, hz, hz, hz, hz, hz, hz, hz, hz, hz,
      cw[:, :inner], cb[:, :inner], cw[:, inner:inner + bc], cb[:, inner:inner + bc],
      cw[:, inner + bc:], cb[:, inner + bc:], dtb, an, d_skip_lanes)


def _out1_kernel(yn_ref, x_ref, m_ref, w_ref, o_ref):
    o_ref[...] = x_ref[...] + m_ref[2:3, :] * jnp.dot(yn_ref[...], w_ref[...], preferred_element_type=F32)


def _out1(yn, xc, slab, w_out, *, t_lat, tiles_per_batch):
    t, d = t_lat, xc.shape[1]
    inner = yn.shape[1]
    tm = TOKEN_TILE // 2
    tpb = tiles_per_batch * (TOKEN_TILE // tm)
    last_row = slab.shape[0] - 1
    return pl.pallas_call(
        _out1_kernel,
        grid=(t // tm,),
        in_specs=[pl.BlockSpec((tm, inner), lambda i: (i, 0)),
                  pl.BlockSpec((tm, d), lambda i: (i, 0)),
                  pl.BlockSpec((None, 8, d), lambda i: (jnp.minimum(i // tpb, last_row), 0, 0)),
                  pl.BlockSpec((inner, d), lambda i: (0, 0))],
        out_specs=pl.BlockSpec((tm, d), lambda i: (i, 0)),
        out_shape=jax.ShapeDtypeStruct((t, d), F32),
        compiler_params=_params(("arbitrary",)),
        name="ssd_out",
    )(yn, xc, slab, w_out.astype(BF16))


def _colmajor_kernel(lat_ref, ctx_ref, o_ref, *, n_lat_steps):
    i = pl.program_id(0)
    n_tok = o_ref.shape[0]
    per = o_ref.shape[1] // V7X_LANES

    def emit(src2d):
        for c, piece in enumerate(_load_token_rows(src2d, 0, n_tok, per)):
            o_ref[:, c * V7X_LANES:(c + 1) * V7X_LANES] = piece

    pl.when(i < n_lat_steps)(lambda: emit(lat_ref.reshape(n_tok * per, V7X_LANES)))
    pl.when(i >= n_lat_steps)(lambda: emit(ctx_ref))


def _to_column_major(lat_rows, ctx_rows, *, nb, rows, d):
    per = d // V7X_LANES
    n_lat_steps = nb * GRID_W
    n_ctx_steps = ctx_rows.shape[0] // (rows * per)
    lat4 = lat_rows.reshape(nb, rows, GRID_W * per, V7X_LANES)

    def lat_index(i):
        j = jnp.minimum(i, n_lat_steps - 1)
        return (j // GRID_W, 0, j % GRID_W, 0)

    return pl.pallas_call(
        functools.partial(_colmajor_kernel, n_lat_steps=n_lat_steps),
        grid=(n_lat_steps + n_ctx_steps,),
        in_specs=[pl.BlockSpec((None, rows, per, V7X_LANES), lat_index),
                  pl.BlockSpec((rows * per, V7X_LANES), lambda i: (jnp.maximum(i - n_lat_steps, 0), 0))],
        out_specs=pl.BlockSpec((rows, d), lambda i: (i, 0)),
        out_shape=jax.ShapeDtypeStruct(((n_lat_steps + n_ctx_steps) * rows, d), F32),
        compiler_params=_params(("arbitrary",)),
        name="to_column_major",
    )(lat4, ctx_rows)


def _final_kernel(x_ref, g_ref, o_ref):
    d = o_ref.shape[1]
    per = d // V7X_LANES
    nr = x_ref.shape[1] // per
    src = x_ref.reshape(GRID_W * nr * per, V7X_LANES)
    for rr in range(nr):
        pieces = [src[pl.ds(rr * per + c, GRID_W, stride=nr * per), :] for c in range(per)]
        ss = pieces[0] * pieces[0]
        for p in pieces[1:]:
            ss = ss + p * p
        scale = lax.rsqrt(jnp.sum(ss, axis=-1, keepdims=True) / d + RMS_EPS)
        for c, p in enumerate(pieces):
            sl = slice(c * V7X_LANES, (c + 1) * V7X_LANES)
            o_ref[rr * GRID_W:(rr + 1) * GRID_W, sl] = p * scale * g_ref[:, sl]


def _final_norm_row_major(xl_rows, g, *, nb, rows, d):
    per = d // V7X_LANES
    nr = math.gcd(rows, FINAL_ROWS_PER_STEP)
    x4 = xl_rows.reshape(nb, GRID_W, rows * per, V7X_LANES)
    return pl.pallas_call(
        _final_kernel,
        grid=(nb, rows // nr),
        in_specs=[pl.BlockSpec((None, GRID_W, nr * per, V7X_LANES), lambda b, r: (b, 0, r, 0)),
                  pl.BlockSpec((1, d), lambda b, r: (0, 0))],
        out_specs=pl.BlockSpec((nr * GRID_W, d), lambda b, r: (b * (rows // nr) + r, 0)),
        out_shape=jax.ShapeDtypeStruct((nb * rows * GRID_W, d), F32),
        compiler_params=_params(("arbitrary", "arbitrary")),
        name="final_norm",
    )(x4, g.reshape(1, d))


def kernel(x, c, ctx, c_ctx, norm_mix_g, norm_ffn_g, mod_w, mod_b, hy_w_in, hy_w_out, s5_lam_re, s5_lam_im, s5_log_dt, s5_b_re, s5_b_im, s5_c_re, s5_c_im, s5_d, s5_glu_w, s5_glu_b, lru_conv_w, lru_conv_b, lru_wa, lru_ba, lru_wx, lru_bx, lru_lam, ssd_w_in, ssd_conv_w, ssd_conv_b, ssd_dt_bias, ssd_a_log, ssd_d, ssd_norm_g, ssd_w_out, moe_wg, moe_bg, moe_we, moe_be, moe_w1, moe_w3, moe_w2, final_norm_g):
    nb, n_lat, d = x.shape
    n_ctx = ctx.shape[1]
    depth = mod_w.shape[0]
    assert depth == 2 and nb + 1 <= 8
    assert n_lat % TOKEN_TILE == 0 and (nb * n_ctx) % TOKEN_TILE == 0
    t_lat = nb * n_lat
    rows = n_lat // GRID_W
    tiles_per_batch = n_lat // TOKEN_TILE
    n_cond = nb + 1

    xa = jnp.concatenate([x.reshape(t_lat, d), ctx.reshape(nb * n_ctx, d)], axis=0)
    cvec = jnp.zeros((8, d), F32).at[:nb].set(c).at[nb].set(c_ctx)
    mods = _mod_vectors(cvec, mod_w, mod_b)
    slab0 = _mod_slab(mods[0], n_cond, d)
    slab1 = _mod_slab(mods[1], n_cond, d)

    width = s5_d.shape[1]
    hz = _norm_matmul(xa, norm_mix_g[0], slab0, hy_w_in[0].astype(BF16), shift_row=0, scale_row=1,
                      tn=width, n_lat=n_lat)
    s5w = _s5_weights(s5_lam_re[0], s5_lam_im[0], s5_log_dt[0], s5_b_re[0], s5_b_im[0],
                      s5_c_re[0], s5_c_im[0])
    ys = _s5_mixer(hz, s5w, width=width, nb=nb, n_lat=n_lat, n_ctx=n_ctx)
    sp = jax.nn.softplus(-lru_lam[0].astype(F32)).reshape(2, 1, width)
    hfb = _lru_scan(hz, 2, lru_conv_w[0], lru_conv_b[0], _block_diag_gates(lru_wa[0]),
                    _block_diag_gates(lru_wx[0]), lru_ba[0].reshape(2, 1, width),
                    lru_bx[0].reshape(2, 1, width), sp, nb=nb, n_lat=n_lat, n_ctx=n_ctx)
    xa = _out0(ys, hz, hfb, xa, slab0, s5_d[0], s5_glu_w[0], s5_glu_b[0], hy_w_out[0],
               tiles_per_batch=tiles_per_batch)
    lat_rows, ctx_rows = _moe_layer(xa, norm_ffn_g[0], slab0, moe_wg[0], moe_bg[0], moe_we[0],
                                    moe_be[0], moe_w1, moe_w3, moe_w2, layer=0,
                                    tiles_per_batch=tiles_per_batch, t_lat=t_lat)

    xc = _to_column_major(lat_rows, ctx_rows, nb=nb, rows=rows, d=d)
    inner = ssd_norm_g.shape[1]
    hz1 = _norm_matmul(xc, norm_mix_g[1], slab1, ssd_w_in[0].astype(BF16), shift_row=0, scale_row=1,
                       tn=ssd_w_in.shape[2] // 9, n_lat=n_lat)
    a_neg = -jnp.exp(ssd_a_log[0].astype(F32))
    dsk = jnp.repeat(ssd_d[0], SSD_HEAD_DIM).reshape(1, inner)
    ssd_args = dict(nb=nb, n_lat=n_lat, n_ctx=n_ctx, inner=inner)
    fwd = _ssd_scan(hz1, ssd_conv_w[0], ssd_conv_b[0], ssd_dt_bias[0], a_neg, dsk, rev=False, **ssd_args)
    yn = _ssd_scan(hz1, ssd_conv_w[0], ssd_conv_b[0], ssd_dt_bias[0], a_neg, dsk, rev=True,
                   fwd=fwd, norm_g=ssd_norm_g[0], **ssd_args)
    xl = _out1(yn, xc, slab1, ssd_w_out[0], t_lat=t_lat, tiles_per_batch=tiles_per_batch)
    (xl_rows,) = _moe_layer(xl, norm_ffn_g[1], slab1, moe_wg[1], moe_bg[1], moe_we[1], moe_be[1],
                            moe_w1, moe_w3, moe_w2, layer=1, tiles_per_batch=tiles_per_batch,
                            t_lat=t_lat)
    out = _final_norm_row_major(xl_rows, final_norm_g, nb=nb, rows=rows, d=d)
    return out.reshape(nb, n_lat, d)
```

```python
import functools
import math

import jax
import jax.numpy as jnp
from jax import lax
from jax.experimental import pallas as pl
from jax.experimental.pallas import tpu as pltpu

F32 = jnp.float32
BF16 = jnp.bfloat16
HIGHEST = lax.Precision.HIGHEST

GRID_W = 64
N_MOD = 6
RMS_EPS = 1e-6
CONV_W = 4
CONV_PAD_LEFT = CONV_W // 2
S5_GROUP = 16
S5_STATE = 64
LRU_HEADS = 16
LRU_C = 8.0
SSD_HEAD_DIM = 64
SSD_GROUPS = 8
SSD_STATE = 128
SSD_CHUNK = 128
MOE_GROUPS = 4
MOE_PER_GROUP = 8
MOE_EXPERTS = MOE_GROUPS * MOE_PER_GROUP
MOE_TOPK = 2

V7X_LANES = 128
V7X_SUBLANES = 8
V7X_MXU_DIM = 256
V7X_VMEM_LIMIT_BYTES = 60000 * 1024

TOKEN_TILE = 512
IN_PROJ_TILE = 1024
S5_CHUNK = V7X_MXU_DIM // S5_GROUP
S5_PAIR_LANES = 2 * S5_STATE
S5_SCAN_LANES = 512
LRU_TILE = 256
LRU_GATE_BLOCK = V7X_MXU_DIM
HALO = V7X_SUBLANES
MOE_BLOCK = 256
SSD_PAIR_UNROLL = 8
FINAL_ROWS_PER_STEP = 4
ROUTE_LANES = V7X_LANES


def _params(sem):
    return pltpu.CompilerParams(dimension_semantics=sem, vmem_limit_bytes=V7X_VMEM_LIMIT_BYTES)


def _silu(v):
    return v * jax.nn.sigmoid(v)


def _gelu(v):
    return jax.nn.gelu(v, approximate=True)


def _mod_kernel(c_ref, w_ref, b_ref, o_ref):
    s = _silu(c_ref[...])
    o_ref[...] = jnp.dot(s, w_ref[...], preferred_element_type=F32, precision=HIGHEST) + b_ref[...]


def _mod_vectors(cvec, mod_w, mod_b):
    depth, d, n = mod_w.shape
    tn = n // 8
    return pl.pallas_call(
        _mod_kernel,
        grid=(depth, n // tn),
        in_specs=[pl.BlockSpec((8, d), lambda l, j: (0, 0)),
                  pl.BlockSpec((None, d, tn), lambda l, j: (l, 0, j)),
                  pl.BlockSpec((None, 1, tn), lambda l, j: (l, 0, j))],
        out_specs=pl.BlockSpec((None, 8, tn), lambda l, j: (l, 0, j)),
        out_shape=jax.ShapeDtypeStruct((depth, 8, n), F32),
        compiler_params=_params(("arbitrary", "arbitrary")),
        name="mod_vectors",
    )(cvec, mod_w, mod_b.reshape(depth, 1, n))


def _mod_slab(mods_layer, n_rows, d):
    m = mods_layer[:n_rows].reshape(n_rows, N_MOD, d)
    return jnp.pad(m, ((0, 0), (0, 8 - N_MOD), (0, 0)))


def _norm_mod(x, g, m_ref, shift_row, scale_row):
    ms = jnp.mean(x * x, axis=-1, keepdims=True)
    y = x * lax.rsqrt(ms + RMS_EPS) * g
    return y * (1.0 + m_ref[scale_row:scale_row + 1, :]) + m_ref[shift_row:shift_row + 1, :]


def _norm_mm_kernel(x_ref, g_ref, m_ref, w_ref, o_ref, xn_ref, *, shift_row, scale_row):
    @pl.when(pl.program_id(1) == 0)
    def _():
        xn_ref[...] = _norm_mod(x_ref[...], g_ref[...], m_ref, shift_row, scale_row).astype(BF16)

    o_ref[...] = jnp.dot(xn_ref[...], w_ref[...], preferred_element_type=F32)


def _norm_matmul(xa, g, slab, w, *, shift_row, scale_row, tn, n_lat):
    t, d = xa.shape
    n = w.shape[1]
    tm = math.gcd(IN_PROJ_TILE, n_lat)
    tiles_per_batch = n_lat // tm
    last_row = slab.shape[0] - 1
    return pl.pallas_call(
        functools.partial(_norm_mm_kernel, shift_row=shift_row, scale_row=scale_row),
        grid=(pl.cdiv(t, tm), n // tn),
        in_specs=[pl.BlockSpec((tm, d), lambda i, j: (i, 0)),
                  pl.BlockSpec((1, d), lambda i, j: (0, 0)),
                  pl.BlockSpec((None, 8, d),
                               lambda i, j: (jnp.minimum(i // tiles_per_batch, last_row), 0, 0)),
                  pl.BlockSpec((d, tn), lambda i, j: (0, j))],
        out_specs=pl.BlockSpec((tm, tn), lambda i, j: (i, j)),
        out_shape=jax.ShapeDtypeStruct((t, n), F32),
        scratch_shapes=[pltpu.VMEM((tm, d), BF16)],
        compiler_params=_params(("arbitrary", "arbitrary")),
        name="norm_matmul",
    )(xa, g.reshape(1, d), slab, w)


def _s5_weights(lam_re, lam_im, log_dt, b_re, b_im, c_re, c_im):
    q = S5_CHUNK
    ngrp, nst = lam_re.shape[1], lam_re.shape[2]
    nch = b_re.shape[-1]
    lr, li = lam_re.astype(F32), lam_im.astype(F32)
    dt = jnp.exp(log_dt.astype(F32))[..., None]
    mag = jnp.exp(lr * dt)
    a_re, a_im = mag * jnp.cos(li * dt), mag * jnp.sin(li * dt)
    den = lr * lr + li * li
    k_re = ((a_re - 1) * lr + a_im * li) / den
    k_im = (a_im * lr - (a_re - 1) * li) / den
    bb_re = k_re[..., None] * b_re - k_im[..., None] * b_im
    bb_im = k_re[..., None] * b_im + k_im[..., None] * b_re
    ks = jnp.arange(q + 1, dtype=F32)[:, None, None, None]
    pmag = jnp.exp(ks * (lr * dt))
    pw_re, pw_im = pmag * jnp.cos(ks * (li * dt)), pmag * jnp.sin(ks * (li * dt))
    ab_re = pw_re[..., None] * bb_re - pw_im[..., None] * bb_im
    ab_im = pw_re[..., None] * bb_im + pw_im[..., None] * bb_re
    contract = lambda c, ab: jnp.sum(ab.transpose(1, 2, 0, 4, 3)[:, :, :, :, None, :]
                                     * c[:, :, None, None, :, :], axis=-1)
    kmat = contract(c_re, ab_re) - contract(c_im, ab_im)
    s_idx = jnp.arange(q)[:, None]
    t_idx = jnp.arange(q)[None, :]
    lag_f = t_idx - s_idx
    lag_b = s_idx - t_idx
    kf = jnp.where((lag_f >= 0)[None, :, :, None, None], kmat[0][:, jnp.clip(lag_f, 0, q - 1)], 0.0)
    kb = jnp.where((lag_b >= 0)[None, :, :, None, None], kmat[1][:, jnp.clip(lag_b, 0, q - 1)], 0.0)
    toep = (kf + kb).transpose(0, 1, 3, 2, 4).reshape(ngrp, q * nch, q * nch)
    sf_re = ab_re[q - 1 - jnp.arange(q), 0]
    sf_im = ab_im[q - 1 - jnp.arange(q), 0]
    sb_re = ab_re[jnp.arange(q), 1]
    sb_im = ab_im[jnp.arange(q), 1]
    summ = jnp.stack([sf_re, sf_im, sb_re, sb_im], 0)
    summ = summ.transpose(2, 1, 4, 0, 3).reshape(ngrp, q * nch, 4, nst)
    npair = ngrp // 2
    summ = summ.reshape(npair, 2, q * nch, 4, nst)
    zero = jnp.zeros_like(summ[:, 0])
    wa = jnp.concatenate([jnp.concatenate([summ[:, 0], zero], -1),
                          jnp.concatenate([zero, summ[:, 1]], -1)], 1)
    wa = wa.reshape(npair, 2 * q * nch, 4 * 2 * nst)
    cp_re = c_re[None] * pw_re[:, :, :, None, :] - c_im[None] * pw_im[:, :, :, None, :]
    cp_im = c_re[None] * pw_im[:, :, :, None, :] + c_im[None] * pw_re[:, :, :, None, :]
    tf = jnp.arange(q) + 1
    tb = q - jnp.arange(q)
    cr = jnp.stack([cp_re[tf, 0], -cp_im[tf, 0], cp_re[tb, 1], -cp_im[tb, 1]], 0)
    cr = cr.transpose(2, 0, 4, 1, 3).reshape(ngrp, 4, nst, q * nch)
    cr = cr.reshape(npair, 2, 4, nst, q * nch)
    zc = jnp.zeros_like(cr[:, 0])
    wc0 = jnp.concatenate([cr[:, 0], zc], 2)
    wc1 = jnp.concatenate([zc, cr[:, 1]], 2)
    wc = jnp.stack([wc0, wc1], 1).reshape(ngrp, 4 * 2 * nst, q * nch)
    a16 = jnp.stack([pw_re[q, 0], pw_im[q, 0], pw_re[q, 1], pw_im[q, 1]], 0)
    a16 = jnp.pad(a16.reshape(4, ngrp * nst), ((0, 4), (0, 0)))
    return toep.astype(BF16), wa.astype(BF16), wc.astype(BF16), a16


def _s5_row_chunk(nc):
    for r in (128, 96, 64, 48, 32, 16):
        if nc % r == 0:
            return r
    raise ValueError(f"unsupported chunk count {nc}")


def _s5_summary_kernel(u_ref, w_ref, ug_ref, o_ref, *, nc):
    q, nch = S5_CHUNK, S5_GROUP
    ngl = u_ref.shape[1] // nch
    rc = _s5_row_chunk(nc)

    def regroup(k, c):
        r0 = pl.multiple_of(k * rc, rc)
        steps = [u_ref[pl.ds(r0 * q + s, rc, stride=q), :] for s in range(q)]
        for gl in range(ngl):
            ug_ref[gl, pl.ds(r0, rc), :] = jnp.concatenate(
                [p[:, gl * nch:(gl + 1) * nch] for p in steps], axis=1).astype(BF16)
        return c

    lax.fori_loop(0, nc // rc, regroup, 0)
    for pr in range(ngl // 2):
        x = jnp.concatenate([ug_ref[2 * pr], ug_ref[2 * pr + 1]], axis=1)
        r = jnp.dot(x, w_ref[pr], preferred_element_type=F32)
        for qd in range(4):
            o_ref[qd, :, pr * S5_PAIR_LANES:(pr + 1) * S5_PAIR_LANES] = (
                r[:, qd * S5_PAIR_LANES:(qd + 1) * S5_PAIR_LANES])


def _s5_scan_kernel(s_ref, a_ref, h_ref, *, nb, n_lat, n_ctx):
    afr, afi = a_ref[0:1, :], a_ref[1:2, :]
    abr, abi = a_ref[2:3, :], a_ref[3:4, :]
    zero = jnp.zeros_like(afr)

    def step(rf, rb, carry):
        hfr, hfi, hbr, hbi = carry
        h_ref[0, pl.ds(rf, 1), :] = hfr
        h_ref[1, pl.ds(rf, 1), :] = hfi
        h_ref[2, pl.ds(rb, 1), :] = hbr
        h_ref[3, pl.ds(rb, 1), :] = hbi
        sfr, sfi = s_ref[0, pl.ds(rf, 1), :], s_ref[1, pl.ds(rf, 1), :]
        sbr, sbi = s_ref[2, pl.ds(rb, 1), :], s_ref[3, pl.ds(rb, 1), :]
        return (afr * hfr - afi * hfi + sfr, afr * hfi + afi * hfr + sfi,
                abr * hbr - abi * hbi + sbr, abr * hbi + abi * hbr + sbi)

    def step_all(i, carry, base, n):
        out = ()
        for b in range(nb):
            out += step(base[b] + i, base[b] + n - 1 - i, carry[4 * b:4 * b + 4])
        return out

    ctx_base = [nb * n_lat + b * n_ctx for b in range(nb)]
    lat_base = [b * n_lat for b in range(nb)]
    carry = lax.fori_loop(0, n_ctx, lambda i, c: step_all(i, c, ctx_base, n_ctx), (zero,) * (4 * nb))
    lax.fori_loop(0, n_lat, lambda i, c: step_all(i, c, lat_base, n_lat), carry)


def _s5_output_kernel(ug_ref, t_ref, h_ref, w_ref, o_ref, y_ref, *, nc):
    q, nch = S5_CHUNK, S5_GROUP
    ngl = ug_ref.shape[0]
    rc = _s5_row_chunk(nc)
    for gl in range(ngl):
        pr = gl // 2
        hcat = jnp.concatenate(
            [h_ref[qd, :, pr * S5_PAIR_LANES:(pr + 1) * S5_PAIR_LANES] for qd in range(4)],
            axis=1).astype(BF16)
        y_ref[gl] = (jnp.dot(ug_ref[gl], t_ref[gl], preferred_element_type=F32)
                     + jnp.dot(hcat, w_ref[gl], preferred_element_type=F32))

    def ungroup(k, c):
        r0 = pl.multiple_of(k * rc, rc)
        ys = [y_ref[gl, pl.ds(r0, rc), :] for gl in range(ngl)]
        for s in range(q):
            o_ref[pl.ds(r0 * q + s, rc, stride=q), :] = jnp.concatenate(
                [y[:, s * nch:(s + 1) * nch] for y in ys], axis=1)
        return c

    lax.fori_loop(0, nc // rc, ungroup, 0)


def _s5_mixer(hz, weights, *, width, nb, n_lat, n_ctx):
    toep, wa, wc, a16 = weights
    t = hz.shape[0]
    q, nch = S5_CHUNK, S5_GROUP
    ngrp = width // nch
    ngl = V7X_LANES // nch
    nstrip = ngrp // ngl
    nc = t // q
    cw = q * nch
    lanes = ngrp * S5_STATE
    slanes = ngl * S5_STATE
    ug, summ = pl.pallas_call(
        functools.partial(_s5_summary_kernel, nc=nc),
        grid=(nstrip,),
        in_specs=[pl.BlockSpec((t, V7X_LANES), lambda j: (0, j)),
                  pl.BlockSpec((ngl // 2, 2 * cw, 4 * S5_PAIR_LANES), lambda j: (j, 0, 0))],
        out_specs=[pl.BlockSpec((ngl, nc, cw), lambda j: (j, 0, 0)),
                   pl.BlockSpec((4, nc, slanes), lambda j: (0, 0, j))],
        out_shape=[jax.ShapeDtypeStruct((ngrp, nc, cw), BF16),
                   jax.ShapeDtypeStruct((4, nc, lanes), F32)],
        compiler_params=_params(("arbitrary",)),
        name="s5_summary",
    )(hz, wa)
    wl = S5_SCAN_LANES
    carry = pl.pallas_call(
        functools.partial(_s5_scan_kernel, nb=nb, n_lat=n_lat // q, n_ctx=n_ctx // q),
        grid=(lanes // wl,),
        in_specs=[pl.BlockSpec((4, nc, wl), lambda j: (0, 0, j)),
                  pl.BlockSpec((8, wl), lambda j: (0, j))],
        out_specs=pl.BlockSpec((4, nc, wl), lambda j: (0, 0, j)),
        out_shape=jax.ShapeDtypeStruct((4, nc, lanes), F32),
        compiler_params=_params(("arbitrary",)),
        name="s5_scan",
    )(summ, a16)
    return pl.pallas_call(
        functools.partial(_s5_output_kernel, nc=nc),
        grid=(nstrip,),
        in_specs=[pl.BlockSpec((ngl, nc, cw), lambda j: (j, 0, 0)),
                  pl.BlockSpec((ngl, cw, cw), lambda j: (j, 0, 0)),
                  pl.BlockSpec((4, nc, slanes), lambda j: (0, 0, j)),
                  pl.BlockSpec((ngl, 4 * S5_PAIR_LANES, cw), lambda j: (j, 0, 0))],
        out_specs=pl.BlockSpec((t, V7X_LANES), lambda j: (0, j)),
        out_shape=jax.ShapeDtypeStruct((t, width), F32),
        scratch_shapes=[pltpu.VMEM((ngl, nc, cw), F32)],
        compiler_params=_params(("arbitrary",)),
        name="s5_output",
    )(ug, toep, carry, wc)


def _dwconv_tile(pad_ref, main, prev, nxt, first, last, cw_ref, cb, rows):
    pad_ref[0:HALO, :] = jnp.where(first, 0.0, prev)
    pad_ref[HALO:HALO + rows, :] = main
    pad_ref[HALO + rows:2 * HALO + rows, :] = jnp.where(last, 0.0, nxt)
    total = rows + 2 * HALO
    pieces = []
    for ct in range(pad_ref.shape[1] // V7X_LANES):
        sl = slice(ct * V7X_LANES, (ct + 1) * V7X_LANES)
        col = pad_ref[:, sl]
        acc = cb[:, sl]
        for k in range(CONV_W):
            shift = (CONV_PAD_LEFT - k) % total
            tap = col if shift == 0 else pltpu.roll(col, shift, axis=0)
            acc = acc + cw_ref[k:k + 1, sl] * tap[HALO:HALO + rows, :]
        pieces.append(acc)
    return jnp.concatenate(pieces, axis=1)


def _lru_kernel(v_ref, vp_ref, vn_ref, cw_ref, cb_ref, wa_ref, wx_ref, ba_ref, bx_ref, sp_ref,
                o_ref, pad_ref, a_ref, b_ref, h_ref, *, n_lat_tiles):
    d = pl.program_id(0)
    k = pl.program_id(2)
    tc = LRU_TILE
    j = jnp.where(d == 0, k - 1, n_lat_tiles - k)
    first = jnp.logical_or(k == 0, j == 0)
    last = jnp.logical_or(k == 0, j == n_lat_tiles - 1)
    vc = _dwconv_tile(pad_ref, v_ref[...], vp_ref[...], vn_ref[...], first, last, cw_ref,
                      cb_ref[...], tc)
    vcb = vc.astype(BF16)
    nblk = vc.shape[1] // LRU_GATE_BLOCK

    def gate(w_ref, bias):
        parts = [jnp.dot(vcb[:, m * LRU_GATE_BLOCK:(m + 1) * LRU_GATE_BLOCK], w_ref[m],
                         preferred_element_type=F32) for m in range(nblk)]
        return jax.nn.sigmoid(jnp.concatenate(parts, axis=1) + bias)

    r = gate(wa_ref, ba_ref[...])
    i = gate(wx_ref, bx_ref[...])
    a = jnp.exp(-LRU_C * r * sp_ref[...])
    a_ref[...] = a
    b_ref[...] = jnp.sqrt(1.0 - a * a) * (i * vc)

    @pl.when(k == 0)
    def _():
        h_ref[...] = jnp.zeros_like(h_ref)

    def body(t, h):
        tt = jnp.where(d == 0, t, tc - 1 - t)
        h = a_ref[pl.ds(tt, 1), :] * h + b_ref[pl.ds(tt, 1), :]
        o_ref[pl.ds(tt, 1), :] = h
        return h

    h_ref[0:1, :] = lax.fori_loop(0, tc, body, h_ref[0:1, :], unroll=8)


def _lru_scan(hz, col_block, conv_w, conv_b, wa, wx, ba, bx, sp, *, nb, n_lat, n_ctx):
    t = hz.shape[0]
    width = conv_w.shape[1]
    tc = LRU_TILE
    assert n_ctx == tc and n_lat % tc == 0
    nlt = n_lat // tc
    hb = tc // HALO
    n_halo_blocks = t // HALO

    def row_block(d, b, k):
        j = jnp.where(d == 0, k - 1, nlt - k)
        return jnp.where(k == 0, nb * nlt + b, b * nlt + j)

    cw = jnp.pad(conv_w, ((0, 8 - CONV_W), (0, 0)))
    nblk = width // LRU_GATE_BLOCK
    vec = lambda: pl.BlockSpec((None, 1, width), lambda d, b, k: (d, 0, 0))
    return pl.pallas_call(
        functools.partial(_lru_kernel, n_lat_tiles=nlt),
        grid=(2, nb, nlt + 1),
        in_specs=[pl.BlockSpec((tc, width), lambda d, b, k: (row_block(d, b, k), col_block)),
                  pl.BlockSpec((HALO, width),
                               lambda d, b, k: (jnp.maximum(row_block(d, b, k) * hb - 1, 0), col_block)),
                  pl.BlockSpec((HALO, width),
                               lambda d, b, k: (jnp.minimum(row_block(d, b, k) * hb + hb,
                                                            n_halo_blocks - 1), col_block)),
                  pl.BlockSpec((8, width), lambda d, b, k: (0, 0)),
                  pl.BlockSpec((1, width), lambda d, b, k: (0, 0)),
                  pl.BlockSpec((None, nblk, LRU_GATE_BLOCK, LRU_GATE_BLOCK), lambda d, b, k: (d, 0, 0, 0)),
                  pl.BlockSpec((None, nblk, LRU_GATE_BLOCK, LRU_GATE_BLOCK), lambda d, b, k: (d, 0, 0, 0)),
                  vec(), vec(), vec()],
        out_specs=pl.BlockSpec((None, tc, width), lambda d, b, k: (d, row_block(d, b, k), 0)),
        out_shape=jax.ShapeDtypeStruct((2, t, width), F32),
        scratch_shapes=[pltpu.VMEM((tc + 2 * HALO, width), F32),
                        pltpu.VMEM((tc, width), F32),
                        pltpu.VMEM((tc, width), F32),
                        pltpu.VMEM((8, width), F32)],
        compiler_params=_params(("arbitrary", "arbitrary", "arbitrary")),
        name="rglru_scan",
    )(hz, hz, hz, cw, conv_b.reshape(1, width), wa, wx, ba, bx, sp)


def _block_diag_gates(w):
    ndir, heads, hd, _ = w.shape
    per = LRU_GATE_BLOCK // hd
    w = w.reshape(ndir, heads // per, per, hd, hd)
    eye = jnp.eye(per, dtype=w.dtype)
    full = jnp.einsum('dmhij,hk->dmhikj', w, eye)
    return full.reshape(ndir, heads // per, per * hd, per * hd).astype(BF16)


def _out0_kernel(ys_ref, u_ref, g_ref, hf_ref, hb_ref, x_ref, m_ref, d_ref, gw_ref, gb_ref,
                 wt_ref, wb_ref, o_ref):
    y = ys_ref[...] + d_ref[...] * u_ref[...]
    z = _gelu(y)
    gate = jax.nn.sigmoid(jnp.dot(z.astype(BF16), gw_ref[...], preferred_element_type=F32) + gb_ref[...])
    a = z * gate
    r = _gelu(g_ref[...]) * (hf_ref[...] + hb_ref[...])
    dx = (jnp.dot(a.astype(BF16), wt_ref[...], preferred_element_type=F32)
          + jnp.dot(r.astype(BF16), wb_ref[...], preferred_element_type=F32))
    o_ref[...] = x_ref[...] + m_ref[2:3, :] * dx


def _out0(ys, hz, hfb, xa, slab, d_skip, glu_w, glu_b, w_out, *, tiles_per_batch):
    t, d = xa.shape
    width = ys.shape[1]
    tm = TOKEN_TILE // 2
    last_row = slab.shape[0] - 1
    tpb = tiles_per_batch * (TOKEN_TILE // tm)
    const = lambda shape: pl.BlockSpec(shape, lambda i: tuple(0 for _ in shape))
    return pl.pallas_call(
        _out0_kernel,
        grid=(t // tm,),
        in_specs=[pl.BlockSpec((tm, width), lambda i: (i, 0)),
                  pl.BlockSpec((tm, width), lambda i: (i, 0)),
                  pl.BlockSpec((tm, width), lambda i: (i, 1)),
                  pl.BlockSpec((None, tm, width), lambda i: (0, i, 0)),
                  pl.BlockSpec((None, tm, width), lambda i: (1, i, 0)),
                  pl.BlockSpec((tm, d), lambda i: (i, 0)),
                  pl.BlockSpec((None, 8, d), lambda i: (jnp.minimum(i // tpb, last_row), 0, 0)),
                  const((1, width)), const((width, width)), const((1, width)),
                  const((width, d)), const((width, d))],
        out_specs=pl.BlockSpec((tm, d), lambda i: (i, 0)),
        out_shape=jax.ShapeDtypeStruct((t, d), F32),
        compiler_params=_params(("arbitrary",)),
        name="hybrid_out",
    )(ys, hz, hz, hfb, hfb, xa, slab, d_skip.reshape(1, width), glu_w.astype(BF16),
      glu_b.reshape(1, width), w_out[:width].astype(BF16), w_out[width:].astype(BF16))


def _store_token_rows(dst_ref, val):
    rows, d = val.shape
    per = d // V7X_LANES
    for c in range(per):
        dst_ref[pl.ds(c, rows, stride=per), :] = val[:, c * V7X_LANES:(c + 1) * V7X_LANES]


def _load_token_rows(src_ref, start, rows, per):
    return [src_ref[pl.ds(start * per + c, rows, stride=per), :] for c in range(per)]


def _store_packed_token_rows(dst_ref, val):
    rows, d = val.shape
    half = d // (2 * V7X_LANES)
    bits = lambda v: pltpu.bitcast(v.astype(BF16).astype(F32), jnp.uint32)
    for c in range(half):
        lo = bits(val[:, c * V7X_LANES:(c + 1) * V7X_LANES])
        hi = bits(val[:, (c + half) * V7X_LANES:(c + half + 1) * V7X_LANES])
        dst_ref[pl.ds(c, rows, stride=half), :] = (lo >> 16) | hi


def _load_packed_token_rows(src_ref, rows, half):
    words = [src_ref[pl.ds(c, rows, stride=half), :] for c in range(half)]
    lo = [pltpu.bitcast(w << 16, F32).astype(BF16) for w in words]
    hi = [pltpu.bitcast(w & jnp.uint32(0xFFFF0000), F32).astype(BF16) for w in words]
    return jnp.concatenate(lo + hi, axis=1)


def _route_kernel(x_ref, g_ref, m_ref, w_ref, b_ref, h_ref, r_ref, cnt_ref, carry_ref):
    @pl.when(pl.program_id(0) == 0)
    def _():
        carry_ref[...] = jnp.zeros_like(carry_ref)

    h = _norm_mod(x_ref[...], g_ref[...], m_ref, 3, 4)
    _store_packed_token_rows(h_ref, h)
    h_hi = h.astype(BF16)
    h_lo = (h - h_hi.astype(F32)).astype(BF16)
    p = jnp.dot(h_hi, w_ref[...], preferred_element_type=F32)
    logits = (p[:, :ROUTE_LANES] + p[:, ROUTE_LANES:] + b_ref[...]
              + jnp.dot(h_lo, w_ref[:, :ROUTE_LANES], preferred_element_type=F32))
    lane = lax.broadcasted_iota(jnp.int32, logits.shape, 1).astype(F32)
    ninf = -jnp.inf
    big = float(ROUTE_LANES)
    lg = jnp.where(lane < MOE_GROUPS, logits, ninf)
    mg = jnp.max(lg, axis=-1, keepdims=True)
    gidx = jnp.min(jnp.where(lg == mg, lane, big), axis=-1, keepdims=True)
    p_top = 1.0 / jnp.sum(jnp.exp(lg - mg), axis=-1, keepdims=True)
    lo = MOE_GROUPS + gidx * MOE_PER_GROUP
    le = jnp.where(jnp.logical_and(lane >= lo, lane < lo + MOE_PER_GROUP), logits, ninf)
    v1 = jnp.max(le, axis=-1, keepdims=True)
    i1 = jnp.min(jnp.where(le == v1, lane, big), axis=-1, keepdims=True)
    le2 = jnp.where(lane == i1, ninf, le)
    v2 = jnp.max(le2, axis=-1, keepdims=True)
    i2 = jnp.min(jnp.where(le2 == v2, lane, big), axis=-1, keepdims=True)
    tt = jnp.exp(v2 - v1)
    g1 = p_top / (1.0 + tt)
    g2 = p_top * tt / (1.0 + tt)
    e1, e2 = i1 - MOE_GROUPS, i2 - MOE_GROUPS
    tm = h.shape[0]
    tri = (lax.broadcasted_iota(jnp.int32, (tm, tm), 1)
           < lax.broadcasted_iota(jnp.int32, (tm, tm), 0)).astype(BF16)
    hot1, hot2 = lane == e1, lane == e2
    before1 = jnp.dot(tri, hot1.astype(BF16), preferred_element_type=F32)
    before2 = jnp.dot(tri, hot2.astype(BF16), preferred_element_type=F32)
    tot1 = jnp.sum(hot1.astype(F32), axis=0, keepdims=True)
    tot2 = jnp.sum(hot2.astype(F32), axis=0, keepdims=True)
    carry = carry_ref[0:1, :]
    rank1 = jnp.sum(jnp.where(hot1, carry + before1, 0.0), axis=-1, keepdims=True)
    rank2 = jnp.sum(jnp.where(hot2, carry + tot1 + before2, 0.0), axis=-1, keepdims=True)
    carry = carry + tot1 + tot2
    carry_ref[0:1, :] = carry
    cnt_ref[...] = jnp.broadcast_to(carry, cnt_ref.shape)
    out = jnp.where(lane == 0, e1, 0.0)
    out = jnp.where(lane == 1, e2, out)
    out = jnp.where(lane == 2, g1, out)
    out = jnp.where(lane == 3, g2, out)
    out = jnp.where(lane == 4, rank1, out)
    out = jnp.where(lane == 5, rank2, out)
    r_ref[...] = out


def _route(xa, g, slab, wg, bg, we, be, *, tiles_per_batch):
    t, d = xa.shape
    tm = TOKEN_TILE
    half = d // (2 * V7X_LANES)
    nl = MOE_GROUPS + MOE_EXPERTS
    wr = jnp.pad(jnp.concatenate([wg, we], axis=1), ((0, 0), (0, ROUTE_LANES - nl)))
    wr_hi = wr.astype(BF16)
    wr = jnp.concatenate([wr_hi, (wr - wr_hi.astype(F32)).astype(BF16)], axis=1)
    br =jnp.pad(jnp.concatenate([bg, be], axis=0), (0, ROUTE_LANES - nl)).reshape(1, ROUTE_LANES)
    last_row = slab.shape[0] - 1
    return pl.pallas_call(
        _route_kernel,
        grid=(t // tm,),
        in_specs=[pl.BlockSpec((tm, d), lambda i: (i, 0)),
                  pl.BlockSpec((1, d), lambda i: (0, 0)),
                  pl.BlockSpec((None, 8, d), lambda i: (jnp.minimum(i // tiles_per_batch, last_row), 0, 0)),
                  pl.BlockSpec((d, 2 * ROUTE_LANES), lambda i: (0, 0)),
                  pl.BlockSpec((1, ROUTE_LANES), lambda i: (0, 0))],
        out_specs=[pl.BlockSpec((tm * half, V7X_LANES), lambda i: (i, 0)),
                   pl.BlockSpec((tm, ROUTE_LANES), lambda i: (i, 0)),
                   pl.BlockSpec((8, ROUTE_LANES), lambda i: (0, 0))],
        out_shape=[jax.ShapeDtypeStruct((t * half, V7X_LANES), jnp.uint32),
                   jax.ShapeDtypeStruct((t, ROUTE_LANES), F32),
                   jax.ShapeDtypeStruct((8, ROUTE_LANES), F32)],
        scratch_shapes=[pltpu.VMEM((8, ROUTE_LANES), F32)],
        compiler_params=_params(("arbitrary",)),
        name="moe_route",
    )(xa, g.reshape(1, d), slab, wr, br)


def _dispatch(route, counts):
    bm = MOE_BLOCK
    n_pairs = route.shape[0] * MOE_TOPK
    experts = route[:, 0:MOE_TOPK].astype(jnp.int32)
    rank = route[:, 4:4 + MOE_TOPK].astype(jnp.int32)
    cnt = counts[0, :MOE_EXPERTS].astype(jnp.int32)
    padded = (cnt + bm - 1) // bm * bm
    pad_end = jnp.cumsum(padded)
    dest = (pad_end - padded)[experts] + rank
    n_blocks = -(-n_pairs // bm) + MOE_EXPERTS
    first_slot = jnp.arange(n_blocks, dtype=jnp.int32) * bm
    block_e = jnp.minimum(jnp.sum((pad_end[None, :] <= first_slot[:, None]).astype(jnp.int32), axis=1),
                          MOE_EXPERTS - 1)
    n_used = (pad_end[-1] // bm).astype(jnp.int32).reshape(1)
    tail = jnp.maximum(pad_end - bm, 0).astype(jnp.int32)
    return dest, block_e, n_used, tail, n_blocks


def _scatter_kernel(tail_ref, nu_ref, d_ref, h_ref, xs_hbm, stage, zsrc, sem, *, n_steps, n_blocks, per):
    i = pl.program_id(0)
    tm = d_ref.shape[1] // MOE_TOPK
    zrows = zsrc.shape[0]

    def fill_at(slot):
        return pltpu.make_async_copy(zsrc, xs_hbm.at[pl.ds(slot * per, zrows), :], sem.at[2])

    def fill(e):
        return fill_at(tail_ref[e])

    def fill_unused(wait):
        def body(blk, c):
            cp = fill_at(blk * MOE_BLOCK)
            cp.wait() if wait else cp.start()
            return c
        lax.fori_loop(nu_ref[0], n_blocks, body, 0)

    def copy(step, r, k):
        return pltpu.make_async_copy(stage.at[lax.rem(step, 2), pl.ds(r * per, per), :],
                                     xs_hbm.at[pl.ds(d_ref[0, MOE_TOPK * r + k] * per, per), :],
                                     sem.at[lax.rem(step, 2)])

    def wait_step(step):
        for r in range(tm):
            for k in range(MOE_TOPK):
                copy(step, r, k).wait()

    @pl.when(i == 0)
    def _():
        zsrc[...] = jnp.zeros_like(zsrc)
        for e in range(MOE_EXPERTS):
            fill(e).start()
        fill_unused(False)
        for e in range(MOE_EXPERTS):
            fill(e).wait()
        fill_unused(True)

    stage[lax.rem(i, 2)] = h_ref[...]
    for r in range(tm):
        for k in range(MOE_TOPK):
            copy(i, r, k).start()

    @pl.when(i > 0)
    def _():
        wait_step(i - 1)

    @pl.when(i == n_steps - 1)
    def _():
        wait_step(i)


def _moe_scatter(h_rows, dest, tail, n_used, n_blocks, per):
    t = dest.shape[0]
    tm = TOKEN_TILE // 2
    n_steps = t // tm
    bm = MOE_BLOCK
    d3 = dest.reshape(n_steps, 1, MOE_TOPK * tm)
    grid_spec = pltpu.PrefetchScalarGridSpec(
        num_scalar_prefetch=2,
        grid=(n_steps,),
        in_specs=[pl.BlockSpec((None, 1, MOE_TOPK * tm), lambda i, tl, nu: (i, 0, 0),
                               memory_space=pltpu.SMEM),
                  pl.BlockSpec((tm * per, V7X_LANES), lambda i, tl, nu: (i, 0))],
        out_specs=pl.BlockSpec(memory_space=pl.ANY),
        scratch_shapes=[pltpu.VMEM((2, tm * per, V7X_LANES), h_rows.dtype),
                        pltpu.VMEM((bm * per, V7X_LANES), h_rows.dtype),
                        pltpu.SemaphoreType.DMA((3,))],
    )
    return pl.pallas_call(
        functools.partial(_scatter_kernel, n_steps=n_steps, n_blocks=n_blocks, per=per),
        grid_spec=grid_spec,
        out_shape=jax.ShapeDtypeStruct((n_blocks * bm * per, V7X_LANES), h_rows.dtype),
        compiler_params=_params(("arbitrary",)),
        name="moe_scatter",
    )(tail, n_used, d3, h_rows)


def _token_copy(src_hbm, tok, dst, slot, r, sem, per):
    return pltpu.make_async_copy(src_hbm.at[pl.ds(tok * per, per), :],
                                 dst.at[slot, pl.ds(r * per, per), :], sem.at[slot])


def _moe_kernel(be_ref, nu_ref, xs_ref, w1_ref, w3_ref, w2_ref, o_ref, w1b, w3b, w2b, prev_ref):
    b = pl.program_id(0)
    bm = MOE_BLOCK
    per = w1_ref.shape[0] // V7X_LANES

    @pl.when(b == 0)
    def _():
        prev_ref[0] = -1

    @pl.when(b < nu_ref[0])
    def _():
        e = be_ref[b]

        @pl.when(e != prev_ref[0])
        def _():
            w1b[...] = w1_ref[...].astype(BF16)
            w3b[...] = w3_ref[...].astype(BF16)
            w2b[...] = w2_ref[...].astype(BF16)
            prev_ref[0] = e

        x = _load_packed_token_rows(xs_ref, bm, per // 2)
        h1 = jnp.dot(x, w1b[...], preferred_element_type=F32)
        h3 = jnp.dot(x, w3b[...], preferred_element_type=F32)
        act = (_silu(h1) * h3).astype(BF16)
        _store_token_rows(o_ref, jnp.dot(act, w2b[...], preferred_element_type=F32))

    @pl.when(b >= nu_ref[0])
    def _():
        o_ref[...] = jnp.zeros_like(o_ref)


def _moe_experts(xs_rows, block_e, n_used, n_blocks, w1, w3, w2, layer):
    d, ff = w1.shape[2], w1.shape[3]
    per = d // V7X_LANES
    bm = MOE_BLOCK
    blk = lambda b, be, nu: jnp.minimum(b, jnp.maximum(nu[0] - 1, 0))
    grid_spec = pltpu.PrefetchScalarGridSpec(
        num_scalar_prefetch=2,
        grid=(n_blocks,),
        in_specs=[pl.BlockSpec((bm * per // 2, V7X_LANES), lambda b, be, nu: (blk(b, be, nu), 0)),
                  pl.BlockSpec((None, None, d, ff), lambda b, be, nu: (layer, be[blk(b, be, nu)], 0, 0)),
                  pl.BlockSpec((None, None, d, ff), lambda b, be, nu: (layer, be[blk(b, be, nu)], 0, 0)),
                  pl.BlockSpec((None, None, ff, d), lambda b, be, nu: (layer, be[blk(b, be, nu)], 0, 0))],
        out_specs=pl.BlockSpec((bm * per, V7X_LANES), lambda b, be, nu: (b, 0)),
        scratch_shapes=[pltpu.VMEM((d, ff), BF16),
                        pltpu.VMEM((d, ff), BF16),
                        pltpu.VMEM((ff, d), BF16),
                        pltpu.SMEM((1,), jnp.int32)],
    )
    return pl.pallas_call(
        _moe_kernel,
        grid_spec=grid_spec,
        out_shape=jax.ShapeDtypeStruct((n_blocks * bm * per, V7X_LANES), F32),
        compiler_params=_params(("arbitrary",)),
        name="moe_experts",
    )(block_e, n_used, xs_rows, w1, w3, w2)


def _combine_kernel(d_ref, dn_ref, yb_hbm, x_ref, r_ref, m_ref, *rest, n_tiles, n_lat_tiles):
    if n_lat_tiles < n_tiles:
        lat_ref, ctx_ref, ybuf, sem = rest
    else:
        lat_ref, ybuf, sem = rest
        ctx_ref = None
    i = pl.program_id(0)
    tm, d = x_ref.shape
    per = d // V7X_LANES
    nrow = MOE_TOPK * tm
    slot = lax.rem(i, 2)

    def start_rows(idx_ref, s):
        for r in range(nrow):
            _token_copy(yb_hbm, idx_ref[0, r], ybuf, s, r, sem, per).start()

    @pl.when(i == 0)
    def _():
        start_rows(d_ref, 0)

    @pl.when(i + 1 < n_tiles)
    def _():
        start_rows(dn_ref, 1 - slot)

    for r in range(nrow):
        _token_copy(yb_hbm, 0, ybuf, slot, r, sem, per).wait()
    r = r_ref[...]
    g0, g1 = r[:, 2:3], r[:, 3:4]
    y0 = _load_token_rows(ybuf.at[slot], 0, tm, per)
    y1 = _load_token_rows(ybuf.at[slot], tm, tm, per)
    def emit(o_ref):
        for c in range(per):
            sl = slice(c * V7X_LANES, (c + 1) * V7X_LANES)
            o_ref[pl.ds(c, tm, stride=per), :] = (
                x_ref[:, sl] + m_ref[5:6, sl] * (g0 * y0[c] + g1 * y1[c]))

    if ctx_ref is None:
        emit(lat_ref)
    else:
        pl.when(i < n_lat_tiles)(lambda: emit(lat_ref))
        pl.when(i >= n_lat_tiles)(lambda: emit(ctx_ref))


def _moe_combine(yb_rows, dest, xa, route, slab, *, tiles_per_batch, t_lat):
    t, d = xa.shape
    per = d // V7X_LANES
    tm = TOKEN_TILE // 2
    n_tiles = t // tm
    n_lat_tiles = t_lat // tm
    row_block = pl.BlockSpec((tm * per, V7X_LANES), lambda i: (jnp.minimum(i, n_lat_tiles - 1), 0))
    out_specs = [row_block]
    out_shape = [jax.ShapeDtypeStruct((t_lat * per, V7X_LANES), F32)]
    if n_lat_tiles < n_tiles:
        out_specs.append(pl.BlockSpec((tm * per, V7X_LANES),
                                      lambda i: (jnp.maximum(i - n_lat_tiles, 0), 0)))
        out_shape.append(jax.ShapeDtypeStruct(((t - t_lat) * per, V7X_LANES), F32))
    tpb = tiles_per_batch * (TOKEN_TILE // tm)
    last_row = slab.shape[0] - 1
    d3 = dest.reshape(n_tiles, tm, MOE_TOPK).transpose(0, 2, 1).reshape(n_tiles, 1, MOE_TOPK * tm)
    smem_block = lambda fn: pl.BlockSpec((None, 1, MOE_TOPK * tm), fn, memory_space=pltpu.SMEM)
    return pl.pallas_call(
        functools.partial(_combine_kernel, n_tiles=n_tiles, n_lat_tiles=n_lat_tiles),
        grid=(n_tiles,),
        in_specs=[smem_block(lambda i: (i, 0, 0)),
                  smem_block(lambda i: (jnp.minimum(i + 1, n_tiles - 1), 0, 0)),
                  pl.BlockSpec(memory_space=pl.ANY),
                  pl.BlockSpec((tm, d), lambda i: (i, 0)),
                  pl.BlockSpec((tm, ROUTE_LANES), lambda i: (i, 0)),
                  pl.BlockSpec((None, 8, d), lambda i: (jnp.minimum(i // tpb, last_row), 0, 0))],
        out_specs=out_specs,
        out_shape=out_shape,
        scratch_shapes=[pltpu.VMEM((2, MOE_TOPK * tm * per, V7X_LANES), F32),
                        pltpu.SemaphoreType.DMA((2,))],
        compiler_params=_params(("arbitrary",)),
        name="moe_combine",
    )(d3, d3, yb_rows, xa, route, slab)


def _moe_layer(xa, g, slab, wg, bg, we, be, w1, w3, w2, *, layer, tiles_per_batch, t_lat):
    per = xa.shape[1] // V7X_LANES
    h_rows, route, counts = _route(xa, g, slab, wg, bg, we, be, tiles_per_batch=tiles_per_batch)
    dest, block_e, n_used, tail, n_blocks = _dispatch(route, counts)
    xs_rows = _moe_scatter(h_rows, dest, tail, n_used, n_blocks, per // 2)
    yb_rows = _moe_experts(xs_rows, block_e, n_used, n_blocks, w1, w3, w2, layer)
    return _moe_combine(yb_rows, dest, xa, route, slab, tiles_per_batch=tiles_per_batch, t_lat=t_lat)


def _ssd_kernel(*refs, rev, n_ctx_chunks, n_lat_chunks):
    if rev:
        (xact_ref, bact_ref, cact_ref, dt_ref, dtb_ref, a_ref, yf_ref, z_ref, ng_ref,
         o_ref, xpair, ypair, cbs, bts, cs, acst, dtt, wtt, tott, state) = refs
    else:
        (xs_ref, xsp_ref, xsn_ref, bm_ref, bmp_ref, bmn_ref, cm_ref, cmp_ref, cmn_ref,
         dt_ref, cwx_ref, cbx_ref, cwb_ref, cbb_ref, cwc_ref, cbc_ref, dtb_ref, a_ref, dsk_ref,
         o_ref, xact_ref, bact_ref, cact_ref,
         padx, padb, padc, xpair, ypair, cbs, bts, cs, acst, dtt, wtt, tott, state) = refs
    k = pl.program_id(1)
    q = SSD_CHUNK
    nst = SSD_STATE
    npair = xact_ref.shape[1] // V7X_LANES
    ngrp = bact_ref.shape[1] // nst
    pairs_per_group = npair // ngrp
    if rev:
        bmat, cmat = bact_ref[...], cact_ref[...]
        for p in range(npair):
            xpair[p] = xact_ref[:, p * V7X_LANES:(p + 1) * V7X_LANES]
    else:
        in_ctx = k < n_ctx_chunks
        jc, jl = k, k - n_ctx_chunks
        first = jnp.where(in_ctx, jc == 0, jl == 0)
        last = jnp.where(in_ctx, jc == n_ctx_chunks - 1, jl == n_lat_chunks - 1)

        def conv_silu(pad_ref, main, prev, nxt, cw_ref, cb_ref):
            return _silu(_dwconv_tile(pad_ref, main[...], prev[...], nxt[...], first, last, cw_ref,
                                      cb_ref[...], q))

        xs = conv_silu(padx, xs_ref, xsp_ref, xsn_ref, cwx_ref, cbx_ref)
        bmat = conv_silu(padb, bm_ref, bmp_ref, bmn_ref, cwb_ref, cbb_ref)
        cmat = conv_silu(padc, cm_ref, cmp_ref, cmn_ref, cwc_ref, cbc_ref)
        bact_ref[...] = bmat
        cact_ref[...] = cmat
        for p in range(npair):
            xb = xs[:, p * V7X_LANES:(p + 1) * V7X_LANES].astype(BF16)
            xpair[p] = xb
            xact_ref[:, p * V7X_LANES:(p + 1) * V7X_LANES] = xb
    for g in range(ngrp):
        bg = bmat[:, g * nst:(g + 1) * nst]
        cg = cmat[:, g * nst:(g + 1) * nst].astype(BF16)
        cs[g] = cg
        cbs[g] = lax.dot_general(cg, bg.astype(BF16), (((1,), (1,)), ((), ())),
                                 preferred_element_type=F32)
        bts[g] = bg.T
    dtr = dt_ref[...] + dtb_ref[...]
    dt = jnp.maximum(dtr, 0.0) + jnp.log(1.0 + jnp.exp(-jnp.abs(dtr)))
    adt = dt * a_ref[...]
    row = lax.broadcasted_iota(jnp.int32, (q, q), 0)
    col = lax.broadcasted_iota(jnp.int32, (q, q), 1)
    causal = (row <= col) if rev else (row >= col)
    acs = jnp.dot(causal.astype(F32), adt, preferred_element_type=F32, precision=HIGHEST)
    acs_t = acs.T
    tot = acs_t[:, 0:1] if rev else acs_t[:, q - 1:q]
    acst[...] = acs_t
    dtt[...] = dt.T
    wtt[...] = jnp.exp(tot - acs_t) * dt.T
    tott[...] = jnp.broadcast_to(jnp.exp(tot), (tott.shape[0], q))

    @pl.when(k == 0)
    def _():
        state[...] = jnp.zeros_like(state)

    hoff = SSD_HEAD_DIM if rev else 0
    lane = lax.broadcasted_iota(jnp.int32, (q, V7X_LANES), 1)
    left = lane < SSD_HEAD_DIM

    def pair_body(p, c):
        g = p // pairs_per_group
        h0 = hoff + 2 * p
        x = xpair[p]
        zero = jnp.zeros_like(x)
        xbd = jnp.concatenate([jnp.where(left, x, zero), jnp.where(left, zero, x)], axis=0)
        cb = cbs[g]
        st = state[p]
        mms, wbs, scales, decs = [], [], [], []
        for m in range(2):
            arow = acst[pl.ds(h0 + m, 1), :]
            acol = jnp.broadcast_to(arow, (q, q)).T
            seg = jnp.where(causal, jnp.exp(acol - arow), 0.0) * dtt[pl.ds(h0 + m, 1), :]
            mms.append((cb * seg).astype(BF16))
            scales.append(jnp.exp(acol))
            wbs.append((bts[g] * wtt[pl.ds(h0 + m, 1), :]).astype(BF16))
            decs.append(tott[pl.ds(h0 + m, 1), :])
        intra = jnp.dot(jnp.concatenate(mms, axis=1), xbd, preferred_element_type=F32)
        new = jnp.dot(jnp.concatenate(wbs, axis=1), xbd, preferred_element_type=F32)
        inter = jnp.dot(cs[g], st.astype(BF16), preferred_element_type=F32)
        ypair[p] = intra + inter * jnp.where(left, scales[0], scales[1])
        state[p] = st * jnp.where(left, decs[0], decs[1]) + new
        return c

    lax.fori_loop(0, npair, pair_body, 0, unroll=SSD_PAIR_UNROLL)
    if not rev:
        for p in range(npair):
            sl = slice(p * V7X_LANES, (p + 1) * V7X_LANES)
            o_ref[:, sl] = ypair[p] + dsk_ref[:, sl] * xs[:, sl]
        return
    sq = jnp.zeros((q, V7X_LANES), F32)
    for p in range(npair):
        sl = slice(p * V7X_LANES, (p + 1) * V7X_LANES)
        gated = (ypair[p] + yf_ref[:, sl]) * _silu(z_ref[:, sl])
        ypair[p] = gated
        sq = sq + gated * gated
    scale = lax.rsqrt(jnp.sum(sq, axis=-1, keepdims=True) / (npair * V7X_LANES) + RMS_EPS)
    for p in range(npair):
        sl = slice(p * V7X_LANES, (p + 1) * V7X_LANES)
        o_ref[:, sl] = (ypair[p] * scale * ng_ref[:, sl]).astype(BF16)


def _ssd_scan(hz, conv_w, conv_b, dt_bias, a_neg, d_skip_lanes, *, rev, nb, n_lat, n_ctx, inner,
              fwd=None, norm_g=None):
    t = hz.shape[0]
    q = SSD_CHUNK
    bc = SSD_GROUPS * SSD_STATE
    ncc, ncl = n_ctx // q, n_lat // q
    hb = q // HALO
    n_halo_blocks = t // HALO
    nheads2 = dt_bias.size
    assert nheads2 == V7X_LANES and q == V7X_LANES and SSD_STATE == V7X_LANES
    npair = inner // V7X_LANES
    xs_col = inner // inner
    b_col = 2 * inner // bc
    c_col = b_col + 1
    dt_col = (2 * inner + 2 * bc) // nheads2

    def chunk(b, k):
        kk = k - ncc
        if rev:
            jc, jl = ncc - 1 - k, ncl - 1 - kk
        else:
            jc, jl = k, kk
        return jnp.where(k < ncc, nb * ncl + b * ncc + jc, b * ncl + jl)

    def main(width, colb):
        return pl.BlockSpec((q, width), lambda b, k: (chunk(b, k), colb))

    def prev(width, colb):
        return pl.BlockSpec((HALO, width), lambda b, k: (jnp.maximum(chunk(b, k) * hb - 1, 0), colb))

    def nxt(width, colb):
        return pl.BlockSpec((HALO, width),
                            lambda b, k: (jnp.minimum(chunk(b, k) * hb + hb, n_halo_blocks - 1), colb))

    const = lambda shape: pl.BlockSpec(shape, lambda b, k: (0, 0))
    shared_scratch = [pltpu.VMEM((npair, q, V7X_LANES), BF16),
                      pltpu.VMEM((npair, q, V7X_LANES), F32),
                      pltpu.VMEM((SSD_GROUPS, q, q), F32),
                      pltpu.VMEM((SSD_GROUPS, SSD_STATE, q), F32),
                      pltpu.VMEM((SSD_GROUPS, q, SSD_STATE), BF16),
                      pltpu.VMEM((nheads2, q), F32),
                      pltpu.VMEM((nheads2, q), F32),
                      pltpu.VMEM((nheads2, q), F32),
                      pltpu.VMEM((nheads2, q), F32),
                      pltpu.VMEM((npair, SSD_STATE, V7X_LANES), F32)]
    kern = functools.partial(_ssd_kernel, rev=rev, n_ctx_chunks=ncc, n_lat_chunks=ncl)
    dtb, an = dt_bias.reshape(1, nheads2), a_neg.reshape(1, nheads2)
    if rev:
        y_fwd, xact, bact, cact = fwd
        return pl.pallas_call(
            kern,
            grid=(nb, ncc + ncl),
            in_specs=[main(inner, 0), main(bc, 0), main(bc, 0), main(nheads2, dt_col),
                      const((1, nheads2)), const((1, nheads2)),
                      main(inner, 0), main(inner, 0), const((1, inner))],
            out_specs=main(inner, 0),
            out_shape=jax.ShapeDtypeStruct((t, inner), BF16),
            scratch_shapes=shared_scratch,
            compiler_params=_params(("arbitrary", "arbitrary")),
            name="ssd_scan_bwd",
        )(xact, bact, cact, hz, dtb, an, y_fwd, hz, norm_g.reshape(1, inner))
    cw = jnp.pad(conv_w, ((0, 8 - CONV_W), (0, 0)))
    cb = conv_b.reshape(1, -1)
    return pl.pallas_call(
        kern,
        grid=(nb, ncc + ncl),
        in_specs=[main(inner, xs_col), prev(inner, xs_col), nxt(inner, xs_col),
                  main(bc, b_col), prev(bc, b_col), nxt(bc, b_col),
                  main(bc, c_col), prev(bc, c_col), nxt(bc, c_col),
                  main(nheads2, dt_col),
                  const((8, inner)), const((1, inner)),
                  const((8, bc)), const((1, bc)),
                  const((8, bc)), const((1, bc)),
                  const((1, nheads2)), const((1, nheads2)), const((1, inner))],
        out_specs=[main(inner, 0), main(inner, 0), main(bc, 0), main(bc, 0)],
        out_shape=[jax.ShapeDtypeStruct((t, inner), F32), jax.ShapeDtypeStruct((t, inner), BF16),
                   jax.ShapeDtypeStruct((t, bc), F32), jax.ShapeDtypeStruct((t, bc), F32)],
        scratch_shapes=[pltpu.VMEM((q + 2 * HALO, inner), F32),
                        pltpu.VMEM((q + 2 * HALO, bc), F32),
                        pltpu.VMEM((q + 2 * HALO, bc), F32)] + shared_scratch,
        compiler_params=_params(("arbitrary", "arbitrary")),
        name="ssd_scan_fwd",
    )(hz, hz, hz, hz, hz, hz, hz, hz, hz, hz,
      cw[:, :inner], cb[:, :inner], cw[:, inner:inner + bc], cb[:, inner:inner + bc],
      cw[:, inner + bc:], cb[:, inner + bc:], dtb, an, d_skip_lanes)


def _out1_kernel(yn_ref, x_ref, m_ref, w_ref, o_ref):
    o_ref[...] = x_ref[...] + m_ref[2:3, :] * jnp.dot(yn_ref[...], w_ref[...], preferred_element_type=F32)


def _out1(yn, xc, slab, w_out, *, t_lat, tiles_per_batch):
    t, d = t_lat, xc.shape[1]
    inner = yn.shape[1]
    tm = TOKEN_TILE // 2
    tpb = tiles_per_batch * (TOKEN_TILE // tm)
    last_row = slab.shape[0] - 1
    return pl.pallas_call(
        _out1_kernel,
        grid=(t // tm,),
        in_specs=[pl.BlockSpec((tm, inner), lambda i: (i, 0)),
                  pl.BlockSpec((tm, d), lambda i: (i, 0)),
                  pl.BlockSpec((None, 8, d), lambda i: (jnp.minimum(i // tpb, last_row), 0, 0)),
                  pl.BlockSpec((inner, d), lambda i: (0, 0))],
        out_specs=pl.BlockSpec((tm, d), lambda i: (i, 0)),
        out_shape=jax.ShapeDtypeStruct((t, d), F32),
        compiler_params=_params(("arbitrary",)),
        name="ssd_out",
    )(yn, xc, slab, w_out.astype(BF16))


def _colmajor_kernel(lat_ref, ctx_ref, o_ref, *, n_lat_steps):
    i = pl.program_id(0)
    n_tok = o_ref.shape[0]
    per = o_ref.shape[1] // V7X_LANES

    def emit(src2d):
        for c, piece in enumerate(_load_token_rows(src2d, 0, n_tok, per)):
            o_ref[:, c * V7X_LANES:(c + 1) * V7X_LANES] = piece

    pl.when(i < n_lat_steps)(lambda: emit(lat_ref.reshape(n_tok * per, V7X_LANES)))
    pl.when(i >= n_lat_steps)(lambda: emit(ctx_ref))


def _to_column_major(lat_rows, ctx_rows, *, nb, rows, d):
    per = d // V7X_LANES
    n_lat_steps = nb * GRID_W
    n_ctx_steps = ctx_rows.shape[0] // (rows * per)
    lat4 = lat_rows.reshape(nb, rows, GRID_W * per, V7X_LANES)

    def lat_index(i):
        j = jnp.minimum(i, n_lat_steps - 1)
        return (j // GRID_W, 0, j % GRID_W, 0)

    return pl.pallas_call(
        functools.partial(_colmajor_kernel, n_lat_steps=n_lat_steps),
        grid=(n_lat_steps + n_ctx_steps,),
        in_specs=[pl.BlockSpec((None, rows, per, V7X_LANES), lat_index),
                  pl.BlockSpec((rows * per, V7X_LANES), lambda i: (jnp.maximum(i - n_lat_steps, 0), 0))],
        out_specs=pl.BlockSpec((rows, d), lambda i: (i, 0)),
        out_shape=jax.ShapeDtypeStruct(((n_lat_steps + n_ctx_steps) * rows, d), F32),
        compiler_params=_params(("arbitrary",)),
        name="to_column_major",
    )(lat4, ctx_rows)


def _final_kernel(x_ref, g_ref, o_ref):
    d = o_ref.shape[1]
    per = d // V7X_LANES
    nr = x_ref.shape[1] // per
    src = x_ref.reshape(GRID_W * nr * per, V7X_LANES)
    for rr in range(nr):
        pieces = [src[pl.ds(rr * per + c, GRID_W, stride=nr * per), :] for c in range(per)]
        ss = pieces[0] * pieces[0]
        for p in pieces[1:]:
            ss = ss + p * p
        scale = lax.rsqrt(jnp.sum(ss, axis=-1, keepdims=True) / d + RMS_EPS)
        for c, p in enumerate(pieces):
            sl = slice(c * V7X_LANES, (c + 1) * V7X_LANES)
            o_ref[rr * GRID_W:(rr + 1) * GRID_W, sl] = p * scale * g_ref[:, sl]


def _final_norm_row_major(xl_rows, g, *, nb, rows, d):
    per = d // V7X_LANES
    nr = math.gcd(rows, FINAL_ROWS_PER_STEP)
    x4 = xl_rows.reshape(nb, GRID_W, rows * per, V7X_LANES)
    return pl.pallas_call(
        _final_kernel,
        grid=(nb, rows // nr),
        in_specs=[pl.BlockSpec((None, GRID_W, nr * per, V7X_LANES), lambda b, r: (b, 0, r, 0)),
                  pl.BlockSpec((1, d), lambda b, r: (0, 0))],
        out_specs=pl.BlockSpec((nr * GRID_W, d), lambda b, r: (b * (rows // nr) + r, 0)),
        out_shape=jax.ShapeDtypeStruct((nb * rows * GRID_W, d), F32),
        compiler_params=_params(("arbitrary", "arbitrary")),
        name="final_norm",
    )(x4, g.reshape(1, d))


def kernel(x, c, ctx, c_ctx, norm_mix_g, norm_ffn_g, mod_w, mod_b, hy_w_in, hy_w_out, s5_lam_re, s5_lam_im, s5_log_dt, s5_b_re, s5_b_im, s5_c_re, s5_c_im, s5_d, s5_glu_w, s5_glu_b, lru_conv_w, lru_conv_b, lru_wa, lru_ba, lru_wx, lru_bx, lru_lam, ssd_w_in, ssd_conv_w, ssd_conv_b, ssd_dt_bias, ssd_a_log, ssd_d, ssd_norm_g, ssd_w_out, moe_wg, moe_bg, moe_we, moe_be, moe_w1, moe_w3, moe_w2, final_norm_g):
    nb, n_lat, d = x.shape
    n_ctx = ctx.shape[1]
    depth = mod_w.shape[0]
    assert depth == 2 and nb + 1 <= 8
    assert n_lat % TOKEN_TILE == 0 and (nb * n_ctx) % TOKEN_TILE == 0
    t_lat = nb * n_lat
    rows = n_lat // GRID_W
    tiles_per_batch = n_lat // TOKEN_TILE
    n_cond = nb + 1

    xa = jnp.concatenate([x.reshape(t_lat, d), ctx.reshape(nb * n_ctx, d)], axis=0)
    cvec = jnp.zeros((8, d), F32).at[:nb].set(c).at[nb].set(c_ctx)
    mods = _mod_vectors(cvec, mod_w, mod_b)
    slab0 = _mod_slab(mods[0], n_cond, d)
    slab1 = _mod_slab(mods[1], n_cond, d)

    width = s5_d.shape[1]
    hz = _norm_matmul(xa, norm_mix_g[0], slab0, hy_w_in[0].astype(BF16), shift_row=0, scale_row=1,
                      tn=width, n_lat=n_lat)
    s5w = _s5_weights(s5_lam_re[0], s5_lam_im[0], s5_log_dt[0], s5_b_re[0], s5_b_im[0],
                      s5_c_re[0], s5_c_im[0])
    ys = _s5_mixer(hz, s5w, width=width, nb=nb, n_lat=n_lat, n_ctx=n_ctx)
    sp = jax.nn.softplus(-lru_lam[0].astype(F32)).reshape(2, 1, width)
    hfb = _lru_scan(hz, 2, lru_conv_w[0], lru_conv_b[0], _block_diag_gates(lru_wa[0]),
                    _block_diag_gates(lru_wx[0]), lru_ba[0].reshape(2, 1, width),
                    lru_bx[0].reshape(2, 1, width), sp, nb=nb, n_lat=n_lat, n_ctx=n_ctx)
    xa = _out0(ys, hz, hfb, xa, slab0, s5_d[0], s5_glu_w[0], s5_glu_b[0], hy_w_out[0],
               tiles_per_batch=tiles_per_batch)
    lat_rows, ctx_rows = _moe_layer(xa, norm_ffn_g[0], slab0, moe_wg[0], moe_bg[0], moe_we[0],
                                    moe_be[0], moe_w1, moe_w3, moe_w2, layer=0,
                                    tiles_per_batch=tiles_per_batch, t_lat=t_lat)

    xc = _to_column_major(lat_rows, ctx_rows, nb=nb, rows=rows, d=d)
    inner = ssd_norm_g.shape[1]
    hz1 = _norm_matmul(xc, norm_mix_g[1], slab1, ssd_w_in[0].astype(BF16), shift_row=0, scale_row=1,
                       tn=ssd_w_in.shape[2] // 9, n_lat=n_lat)
    a_neg = -jnp.exp(ssd_a_log[0].astype(F32))
    dsk = jnp.repeat(ssd_d[0], SSD_HEAD_DIM).reshape(1, inner)
    ssd_args = dict(nb=nb, n_lat=n_lat, n_ctx=n_ctx, inner=inner)
    fwd = _ssd_scan(hz1, ssd_conv_w[0], ssd_conv_b[0], ssd_dt_bias[0], a_neg, dsk, rev=False, **ssd_args)
    yn = _ssd_scan(hz1, ssd_conv_w[0], ssd_conv_b[0], ssd_dt_bias[0], a_neg, dsk, rev=True,
                   fwd=fwd, norm_g=ssd_norm_g[0], **ssd_args)
    xl = _out1(yn, xc, slab1, ssd_w_out[0], t_lat=t_lat, tiles_per_batch=tiles_per_batch)
    (xl_rows,) = _moe_layer(xl, norm_ffn_g[1], slab1, moe_wg[1], moe_bg[1], moe_we[1], moe_be[1],
                            moe_w1, moe_w3, moe_w2, layer=1, tiles_per_batch=tiles_per_batch,
                            t_lat=t_lat)
    out = _final_norm_row_major(xl_rows, final_norm_g, nb=nb, rows=rows, d=d)
    return out.reshape(nb, n_lat, d)
```
